```python
import jax
import jax.numpy as jnp
from jax import lax
import numpy as np

D_MODEL = 1024
BATCH = 4
SEQ = 8192
DEPTH = 1

CHUNK = 64
MIX_WIDTH = D_MODEL
GLA_HEADS = 4
GLA_DV = (MIX_WIDTH // 2) // GLA_HEADS
GLA_DK = GLA_DV // 2
GLA_GATE_RANK = 16
GLA_GATE_NORM = 16.0
RET_HEADS = 4
RET_DV = (MIX_WIDTH // 2) // RET_HEADS
RET_DK = RET_DV // 2
ROPE_BASE = 10000.0
N_EXPERTS = 32
TOP_K = 4
D_FF = D_MODEL
SWIGLU_ALPHA = 1.702
SWIGLU_LIMIT = 7.0
MOE_BLOCK = 256
LN_EPS = 1e-5
MAX_STREAM_OFFSET = 4096
DEEPNORM_ALPHA = (2.0 * DEPTH) ** 0.25
DEEPNORM_BETA = (8.0 * DEPTH) ** -0.25
PROJ_SIZES = (GLA_HEADS * GLA_DK, GLA_HEADS * GLA_DK, GLA_HEADS * GLA_DV, GLA_HEADS * GLA_DV, GLA_GATE_RANK,
              RET_HEADS * RET_DK, RET_HEADS * RET_DK, RET_HEADS * RET_DV, RET_HEADS * RET_DV)
PROJ_WIDTH = sum(PROJ_SIZES)
VALUE_BLOCKS = (2, 7)

kernel_name = 'hybrid_gla_retention_moe_deepnorm'


def layer_norm(x, g, b):
    xf = x.astype(jnp.float32)
    mu = jnp.mean(xf, -1, keepdims=True)
    var = jnp.mean(jnp.square(xf - mu), -1, keepdims=True)
    y = (xf - mu) * lax.rsqrt(var + LN_EPS) * g.astype(jnp.float32) + b.astype(jnp.float32)
    return y.astype(x.dtype)


def split_cols(t, sizes):
    idx = np.cumsum(sizes)[:-1].tolist()
    return jnp.split(t, idx, axis=-1)


def rotary(t, positions):
    half = t.shape[-1] // 2
    inv_freq = 1.0 / (ROPE_BASE ** jnp.linspace(0.0, 1.0, half, dtype=jnp.float32))
    ang = positions.astype(jnp.float32)[:, :, None, None] * inv_freq
    cos, sin = jnp.cos(ang), jnp.sin(ang)
    t1, t2 = t[..., :half], t[..., half:]
    return jnp.concatenate([t1 * cos - t2 * sin, t1 * sin + t2 * cos], axis=-1)


def chunk_decay_linear_attention(q, k, v, log_a):
    bsz, seq, heads, dk = q.shape
    dv = v.shape[-1]
    n = seq // CHUNK
    q = q.reshape(bsz, n, CHUNK, heads, dk)
    k = k.reshape(bsz, n, CHUNK, heads, dk)
    v = v.reshape(bsz, n, CHUNK, heads, dv)
    b = jnp.cumsum(log_a.reshape(bsz, n, CHUNK, heads, dk), axis=2)
    b_last = b[:, :, -1:]
    eb, enb = jnp.exp(b), jnp.exp(-b)
    qe = q * eb
    s_lo = jnp.einsum('bnihd,bnjhd->bnhij', qe, k * enb)
    s_up = jnp.einsum('bnihd,bnjhd->bnhij', q * enb, k * eb)
    lower = jnp.arange(CHUNK)[:, None] >= jnp.arange(CHUNK)[None, :]
    scores = jnp.where(lower, s_lo, s_up)
    o_intra = jnp.einsum('bnhij,bnjhe->bnihe', scores, v)
    kv = jnp.einsum('bnjhd,bnjhe->bnhde', k * jnp.exp(b_last - b), v)
    dec = jnp.exp(b_last[:, :, 0])

    def step(state, inp):
        kv_c, dec_c = inp
        return state * dec_c[..., None] + kv_c, state

    init = jnp.zeros((bsz, heads, dk, dv), jnp.float32)
    _, prev = lax.scan(step, init, (jnp.moveaxis(kv, 1, 0), jnp.moveaxis(dec, 1, 0)))
    prev = jnp.moveaxis(prev, 0, 1)
    o_inter = jnp.einsum('bnihd,bnhde->bnihe', qe, prev)
    return (o_intra + o_inter).reshape(bsz, seq, heads, dv)


def head_rms_norm(o, g):
    o = o * lax.rsqrt(jnp.mean(jnp.square(o), -1, keepdims=True) + LN_EPS)
    return (o * g).reshape(o.shape[0], o.shape[1], -1)


def head_group_norm(o, g, b):
    mu = jnp.mean(o, -1, keepdims=True)
    var = jnp.mean(jnp.square(o - mu), -1, keepdims=True)
    o = ((o - mu) * lax.rsqrt(var + LN_EPS)).reshape(o.shape[0], o.shape[1], -1)
    return o * g + b


def moe_ffn(h, router_w, router_b, w1, b1, w2, b2):
    bsz, seq, d = h.shape
    xt = h.reshape(-1, d)
    t = xt.shape[0]
    tk = t * TOP_K
    logits = (xt @ router_w + router_b).astype(jnp.float32)
    top_val, top_idx = lax.top_k(logits, TOP_K)
    gates = jax.nn.softmax(top_val, axis=-1)
    flat_e = top_idx.reshape(-1).astype(jnp.int32)
    flat_tok = jnp.repeat(jnp.arange(t, dtype=jnp.int32), TOP_K)
    flat_gate = gates.reshape(-1)
    order = jnp.argsort(flat_e)
    sorted_e = flat_e[order]
    counts = jnp.bincount(flat_e, length=N_EXPERTS).astype(jnp.int32)
    padded = ((counts + MOE_BLOCK - 1) // MOE_BLOCK) * MOE_BLOCK
    start_sorted = jnp.cumsum(counts) - counts
    pad_end = jnp.cumsum(padded)
    pad_start = pad_end - padded
    rank = jnp.arange(tk, dtype=jnp.int32) - start_sorted[sorted_e]
    dest = (pad_start[sorted_e] + rank).astype(jnp.int32)
    n_slots = ((tk + MOE_BLOCK - 1) // MOE_BLOCK) * MOE_BLOCK + N_EXPERTS * MOE_BLOCK
    n_blocks = n_slots // MOE_BLOCK
    slot_tok = jnp.zeros((n_slots,), jnp.int32).at[dest].set(flat_tok[order])
    slot_gate = jnp.zeros((n_slots,), jnp.float32).at[dest].set(flat_gate[order])
    block_start = jnp.arange(n_blocks, dtype=jnp.int32) * MOE_BLOCK
    block_expert = jnp.minimum(jnp.searchsorted(pad_end, block_start, side='right'),
                               N_EXPERTS - 1).astype(jnp.int32)

    def expert_block(args):
        tok, e = args
        xb = xt[tok]
        hh = xb @ w1[e] + b1[e]
        x_glu = jnp.minimum(hh[:, :D_FF], SWIGLU_LIMIT)
        x_lin = jnp.clip(hh[:, D_FF:], -SWIGLU_LIMIT, SWIGLU_LIMIT)
        act = x_glu * jax.nn.sigmoid(SWIGLU_ALPHA * x_glu) * (x_lin + 1.0)
        return act @ w2[e] + b2[e]

    y = lax.map(expert_block, (slot_tok.reshape(n_blocks, MOE_BLOCK), block_expert))
    y = y.reshape(n_slots, d) * slot_gate[:, None].astype(y.dtype)
    out = jnp.zeros((t, d), y.dtype).at[slot_tok].add(y)
    return out.reshape(bsz, seq, d)


def setup_inputs(seed: int = 0) -> dict:
    key = jax.random.key(seed)
    ks = jax.random.split(key, 24)
    f32 = jnp.float32

    def nrm(k, shape, scale):
        return jax.random.normal(k, shape, f32) * scale

    x = nrm(ks[0], (BATCH, SEQ, D_MODEL), 1.0)
    offsets = jax.random.randint(ks[1], (BATCH, 1), 0, MAX_STREAM_OFFSET, dtype=jnp.int32)
    positions = (offsets + jnp.arange(SEQ, dtype=jnp.int32)[None, :]).astype(jnp.int32)
    ln_in_g = 1.0 + nrm(ks[2], (D_MODEL,), 0.02)
    ln_in_b = nrm(ks[3], (D_MODEL,), 0.02)
    col_scale = jnp.concatenate([jnp.full((s,), DEEPNORM_BETA if i in VALUE_BLOCKS else 1.0, f32)
                                 for i, s in enumerate(PROJ_SIZES)])
    w_in = nrm(ks[4], (DEPTH, D_MODEL, PROJ_WIDTH), D_MODEL ** -0.5) * col_scale
    gla_gate_w = nrm(ks[5], (DEPTH, GLA_GATE_RANK, GLA_HEADS * GLA_DK), GLA_GATE_RANK ** -0.5)
    gla_gate_b = nrm(ks[6], (DEPTH, GLA_HEADS * GLA_DK), 0.1)
    gla_norm_g = 1.0 + nrm(ks[7], (DEPTH, GLA_DV), 0.02)
    ret_norm_g = 1.0 + nrm(ks[8], (DEPTH, RET_HEADS * RET_DV), 0.02)
    ret_norm_b = nrm(ks[9], (DEPTH, RET_HEADS * RET_DV), 0.02)
    w_out = nrm(ks[10], (DEPTH, MIX_WIDTH, D_MODEL), MIX_WIDTH ** -0.5 * DEEPNORM_BETA)
    ln1_g = 1.0 + nrm(ks[11], (DEPTH, D_MODEL), 0.02)
    ln1_b = nrm(ks[12], (DEPTH, D_MODEL), 0.02)
    router_w = nrm(ks[13], (DEPTH, D_MODEL, N_EXPERTS), D_MODEL ** -0.5)
    router_b = nrm(ks[14], (DEPTH, N_EXPERTS), 0.01)
    moe_w1 = nrm(ks[15], (DEPTH, N_EXPERTS, D_MODEL, 2 * D_FF), D_MODEL ** -0.5 * DEEPNORM_BETA)
    moe_b1 = nrm(ks[16], (DEPTH, N_EXPERTS, 2 * D_FF), 0.02)
    moe_w2 = nrm(ks[17], (DEPTH, N_EXPERTS, D_FF, D_MODEL), D_FF ** -0.5 * DEEPNORM_BETA)
    moe_b2 = nrm(ks[18], (DEPTH, N_EXPERTS, D_MODEL), 0.02)
    ln2_g = 1.0 + nrm(ks[19], (DEPTH, D_MODEL), 0.02)
    ln2_b = nrm(ks[20], (DEPTH, D_MODEL), 0.02)
    return {'x': x, 'positions': positions, 'ln_in_g': ln_in_g, 'ln_in_b': ln_in_b,
            'w_in': w_in, 'gla_gate_w': gla_gate_w, 'gla_gate_b': gla_gate_b,
            'gla_norm_g': gla_norm_g, 'ret_norm_g': ret_norm_g, 'ret_norm_b': ret_norm_b,
            'w_out': w_out, 'ln1_g': ln1_g, 'ln1_b': ln1_b, 'router_w': router_w,
            'router_b': router_b, 'moe_w1': moe_w1, 'moe_b1': moe_b1, 'moe_w2': moe_w2,
            'moe_b2': moe_b2, 'ln2_g': ln2_g, 'ln2_b': ln2_b}


def reference(x, positions, ln_in_g, ln_in_b, w_in, gla_gate_w, gla_gate_b, gla_norm_g,
              ret_norm_g, ret_norm_b, w_out, ln1_g, ln1_b, router_w, router_b,
              moe_w1, moe_b1, moe_w2, moe_b2, ln2_g, ln2_b):
    bsz, seq, _ = x.shape
    f32 = jnp.float32
    log_gamma = jnp.log1p(-jnp.exp2(-5.0 - jnp.arange(RET_HEADS, dtype=f32)))
    h = layer_norm(x, ln_in_g, ln_in_b)
    for l in range(DEPTH):
        proj = (h @ w_in[l]).astype(f32)
        gq, gk, gv, gg, glr, rq, rk, rv, rg = split_cols(proj, PROJ_SIZES)
        q = gq.reshape(bsz, seq, GLA_HEADS, GLA_DK) * GLA_DK ** -0.5
        k = gk.reshape(bsz, seq, GLA_HEADS, GLA_DK)
        v = gv.reshape(bsz, seq, GLA_HEADS, GLA_DV)
        log_a = jax.nn.log_sigmoid(glr @ gla_gate_w[l].astype(f32) + gla_gate_b[l].astype(f32)) / GLA_GATE_NORM
        log_a = log_a.reshape(bsz, seq, GLA_HEADS, GLA_DK)
        o_gla = chunk_decay_linear_attention(q, k, v, log_a)
        o_gla = head_rms_norm(o_gla, gla_norm_g[l].astype(f32)) * jax.nn.silu(gg)
        q = rotary(rq.reshape(bsz, seq, RET_HEADS, RET_DK), positions)
        k = rotary(rk.reshape(bsz, seq, RET_HEADS, RET_DK), positions) * RET_DK ** -0.5
        v = rv.reshape(bsz, seq, RET_HEADS, RET_DV)
        log_d = jnp.broadcast_to(log_gamma[None, None, :, None], q.shape)
        o_ret = chunk_decay_linear_attention(q, k, v, log_d)
        o_ret = head_group_norm(o_ret, ret_norm_g[l].astype(f32), ret_norm_b[l].astype(f32)) * jax.nn.silu(rg)
        mix = jnp.concatenate([o_gla, o_ret], axis=-1).astype(x.dtype) @ w_out[l]
        h = layer_norm(DEEPNORM_ALPHA * h + mix, ln1_g[l], ln1_b[l])
        ffn = moe_ffn(h, router_w[l], router_b[l], moe_w1[l], moe_b1[l], moe_w2[l], moe_b2[l])
        h = layer_norm(DEEPNORM_ALPHA * h + ffn.astype(h.dtype), ln2_g[l], ln2_b[l])
    return h
```

```python
import functools
import math

import numpy as np
import jax
import jax.numpy as jnp
from jax import lax
from jax.experimental import pallas as pl
from jax.experimental.pallas import tpu as pltpu

F32 = jnp.float32
MXU_DTYPE = jnp.bfloat16

D_MODEL = 1024
CHUNK = 64
HEADS = 4
DK = 64
DV = 128
GATE_RANK = 16
GATE_NORM = 16.0
ROPE_BASE = 10000.0
N_EXPERTS = 32
TOP_K = 4
D_FF = 1024
SWIGLU_ALPHA = 1.702
SWIGLU_LIMIT = 7.0
MOE_BLOCK = 256
LN_EPS = 1e-5
DEPTH = 1
DEEPNORM_ALPHA = (2.0 * DEPTH) ** 0.25

LANES = 128
VMEM_LIMIT = 56 * 1024 * 1024

C_GQ, C_GK, C_GV, C_GG, C_GLR = 0, 256, 512, 1024, 1536
C_RQ, C_RK, C_RV, C_RG = 1664, 1920, 2176, 2688
PROJ_W = 3200

LOG_GAMMA = [math.log1p(-(2.0 ** (-5.0 - h))) for h in range(HEADS)]


def _dot(a, b):
    return jnp.dot(a, b, preferred_element_type=F32)


def _dot_nt(a, b):
    return lax.dot_general(a, b, (((1,), (1,)), ((), ())), preferred_element_type=F32)


def _dot_tn(a, b):
    return lax.dot_general(a, b, (((0,), (0,)), ((), ())), preferred_element_type=F32)


def _mx(a):
    return a.astype(MXU_DTYPE)


def _layer_norm(x, g, b):
    mu = jnp.mean(x, -1, keepdims=True)
    xc = x - mu
    var = jnp.mean(xc * xc, -1, keepdims=True)
    return xc * lax.rsqrt(var + LN_EPS) * g + b


def _silu(x):
    return x * (1.0 / (1.0 + jnp.exp(-x)))


def _ln_proj_kernel(x_ref, g_ref, b_ref, w_ref, h_ref, p_ref):
    h = _layer_norm(x_ref[...], g_ref[...], b_ref[...])
    h_ref[...] = h
    p_ref[...] = _dot(_mx(h), w_ref[...])


def _ln_proj_call(x2, g, b, w, tm):
    t = x2.shape[0]
    return pl.pallas_call(
        _ln_proj_kernel,
        grid=(t // tm,),
        in_specs=[
            pl.BlockSpec((tm, D_MODEL), lambda i: (i, 0)),
            pl.BlockSpec((1, D_MODEL), lambda i: (0, 0)),
            pl.BlockSpec((1, D_MODEL), lambda i: (0, 0)),
            pl.BlockSpec((D_MODEL, PROJ_W), lambda i: (0, 0)),
        ],
        out_specs=[
            pl.BlockSpec((tm, D_MODEL), lambda i: (i, 0)),
            pl.BlockSpec((tm, PROJ_W), lambda i: (i, 0)),
        ],
        out_shape=[
            jax.ShapeDtypeStruct((t, D_MODEL), F32),
            jax.ShapeDtypeStruct((t, PROJ_W), F32),
        ],
        compiler_params=pltpu.CompilerParams(
            dimension_semantics=("arbitrary",), vmem_limit_bytes=VMEM_LIMIT),
        name="ln_proj",
    )(x2, g, b, w)


def _mixer_kernel(p_ref, pos_ref, gw_ref, gb_ref, gn_ref, rng_ref, rnb_ref, invf_ref, sgn_ref,
                  o_ref, st_ref, *, n_chunks):
    @pl.when(pl.program_id(1) == 0)
    def _():
        st_ref[...] = jnp.zeros_like(st_ref)

    row = lax.broadcasted_iota(jnp.int32, (CHUNK, CHUNK), 0)
    col = lax.broadcasted_iota(jnp.int32, (CHUNK, CHUNK), 1)
    lower = row >= col
    tri = lower.astype(MXU_DTYPE)
    dist = jnp.abs(row - col).astype(F32)
    rowf = lax.broadcasted_iota(jnp.int32, (CHUNK, LANES), 0).astype(F32)
    lane = lax.broadcasted_iota(jnp.int32, (CHUNK, LANES), 1)
    gla_half = [((lane >> 6) & 1) == i for i in range(2)]
    ret_half = [((lane >> 5) & 1) == i for i in range(2)]
    ret_d = [jnp.exp(LOG_GAMMA[h] * dist) for h in range(HEADS)]
    ret_eb = [jnp.exp(LOG_GAMMA[h] * (rowf + 1.0)) for h in range(HEADS)]
    ret_ek = [jnp.exp(LOG_GAMMA[h] * (CHUNK - 1.0 - rowf)) for h in range(HEADS)]
    ret_dec = [math.exp(LOG_GAMMA[h] * CHUNK) for h in range(HEADS)]

    def chunk_body(c, carry):
        r0 = pl.multiple_of(c * CHUNK, CHUNK)
        rows = pl.ds(r0, CHUNK)

        z = _dot(_mx(p_ref[rows, C_GLR:C_GLR + LANES]), gw_ref[...]) + gb_ref[...]
        la = (jnp.minimum(z, 0.0) - jnp.log1p(jnp.exp(-jnp.abs(z)))) * (1.0 / GATE_NORM)
        la_hi = _mx(la)
        r1 = la - la_hi.astype(F32)
        la_mid = _mx(r1)
        la_lo = _mx(r1 - la_mid.astype(F32))
        b = _dot(tri, la_hi) + _dot(tri, la_mid) + _dot(tri, la_lo)
        b_last = b[CHUNK - 1:CHUNK, :]
        eb = jnp.exp(b)
        enb = jnp.exp(-b)
        ekv = jnp.exp(b_last - b)
        dec = jnp.exp(b_last)
        q = p_ref[rows, C_GQ:C_GQ + 256] * (DK ** -0.5)
        k = p_ref[rows, C_GK:C_GK + 256]
        qe, qn = q * eb, q * enb
        ke, kn, kk = k * eb, k * enb, k * ekv
        for h in range(HEADS):
            p, half = h // 2, h % 2
            ls = slice(p * LANES, (p + 1) * LANES)
            m = gla_half[half]
            qe_m = _mx(jnp.where(m, qe[:, ls], 0.0))
            qn_m = _mx(jnp.where(m, qn[:, ls], 0.0))
            s_lo = _dot_nt(qe_m, _mx(kn[:, ls]))
            s_up = _dot_nt(qn_m, _mx(ke[:, ls]))
            sc = jnp.where(lower, s_lo, s_up)
            v = _mx(p_ref[rows, C_GV + h * DV:C_GV + (h + 1) * DV])
            st = st_ref[h]
            o = _dot(_mx(sc), v) + _dot_nt(qe_m, _mx(st))
            st_ref[h] = st * dec[:, ls] + _dot_tn(v, _mx(kk[:, ls]))
            o = o * lax.rsqrt(jnp.mean(o * o, -1, keepdims=True) + LN_EPS) * gn_ref[...]
            o = o * _silu(p_ref[rows, C_GG + h * DV:C_GG + (h + 1) * DV])
            o_ref[rows, h * DV:(h + 1) * DV] = o.astype(o_ref.dtype)

        ang = pos_ref[rows, :].astype(F32) * invf_ref[...]
        cs = jnp.cos(ang)
        sn = jnp.sin(ang) * sgn_ref[...]
        for p in range(2):
            ls = slice(p * LANES, (p + 1) * LANES)
            tq = p_ref[rows, C_RQ + p * LANES:C_RQ + (p + 1) * LANES]
            tk = p_ref[rows, C_RK + p * LANES:C_RK + (p + 1) * LANES] * (DK ** -0.5)
            rq = tq * cs + pltpu.roll(tq, LANES // 2, 1) * sn
            rk = tk * cs + pltpu.roll(tk, LANES // 2, 1) * sn
            rk_m = _mx(rk)
            for half in range(2):
                h = 2 * p + half
                hs = HEADS + h
                q_m = _mx(jnp.where(ret_half[half], rq, 0.0))
                s = _dot_nt(q_m, rk_m) * ret_d[h]
                v = _mx(p_ref[rows, C_RV + h * DV:C_RV + (h + 1) * DV])
                st = st_ref[hs]
                o = _dot(_mx(s), v) + ret_eb[h] * _dot_nt(q_m, _mx(st))
                st_ref[hs] = st * ret_dec[h] + _dot_tn(v, _mx(rk * ret_ek[h]))
                mu = jnp.mean(o, -1, keepdims=True)
                oc = o - mu
                var = jnp.mean(oc * oc, -1, keepdims=True)
                o = oc * lax.rsqrt(var + LN_EPS) * rng_ref[:, h * DV:(h + 1) * DV] \
                    + rnb_ref[:, h * DV:(h + 1) * DV]
                o = o * _silu(p_ref[rows, C_RG + h * DV:C_RG + (h + 1) * DV])
                o_ref[rows, (HEADS + h) * DV:(HEADS + h + 1) * DV] = o.astype(o_ref.dtype)
        return carry

    lax.fori_loop(0, n_chunks, chunk_body, 0)


def _mixer_call(proj, pos, gw, gb, gn, rng, rnb, invf, sgn, bsz, seq, ts):
    t = bsz * seq
    nst = seq // ts
    const = lambda b, s: (0, 0)
    return pl.pallas_call(
        functools.partial(_mixer_kernel, n_chunks=ts // CHUNK),
        grid=(bsz, nst),
        in_specs=[
            pl.BlockSpec((ts, PROJ_W), lambda b, s: (b * nst + s, 0)),
            pl.BlockSpec((ts, 1), lambda b, s: (b * nst + s, 0)),
            pl.BlockSpec((LANES, 256), const),
            pl.BlockSpec((1, 256), const),
            pl.BlockSpec((1, DV), const),
            pl.BlockSpec((1, HEADS * DV), const),
            pl.BlockSpec((1, HEADS * DV), const),
            pl.BlockSpec((1, LANES), const),
            pl.BlockSpec((1, LANES), const),
        ],
        out_specs=pl.BlockSpec((ts, D_MODEL), lambda b, s: (b * nst + s, 0)),
        out_shape=jax.ShapeDtypeStruct((t, D_MODEL), MXU_DTYPE),
        scratch_shapes=[pltpu.VMEM((2 * HEADS, DV, LANES), F32)],
        compiler_params=pltpu.CompilerParams(
            dimension_semantics=("arbitrary", "arbitrary"), vmem_limit_bytes=VMEM_LIMIT),
        name="mixers",
    )(proj, pos, gw, gb, gn, rng, rnb, invf, sgn)


def _post_mix_kernel(o_ref, h0_ref, wo_ref, g_ref, b_ref, rw_ref, rb_ref,
                     h1_ref, ri_ref, rg_ref, cnt_ref, tril_ref, carry_ref, *, tm):
    @pl.when(pl.program_id(0) == 0)
    def _():
        r = lax.broadcasted_iota(jnp.int32, (tm, tm), 0)
        c = lax.broadcasted_iota(jnp.int32, (tm, tm), 1)
        tril_ref[...] = (r > c).astype(MXU_DTYPE)
        carry_ref[...] = jnp.zeros_like(carry_ref)

    mix = _dot(o_ref[...], wo_ref[...])
    h1 = _layer_norm(DEEPNORM_ALPHA * h0_ref[...] + mix, g_ref[...], b_ref[...])
    h1_ref[...] = h1

    lane = lax.broadcasted_iota(jnp.int32, (tm, LANES), 1)
    lane_f = lane.astype(F32)
    logits = _dot(_mx(h1), rw_ref[...]) + rb_ref[...]
    l = jnp.where(lane < N_EXPERTS, logits, -jnp.inf)
    vals, idxs = [], []
    for _ in range(TOP_K):
        m = jnp.max(l, -1, keepdims=True)
        i = jnp.min(jnp.where(l == m, lane_f, float(LANES)), -1, keepdims=True)
        vals.append(m)
        idxs.append(i)
        l = jnp.where(lane_f == i, -jnp.inf, l)
    exps = [jnp.exp(v - vals[0]) for v in vals]
    inv = 1.0 / (exps[0] + exps[1] + exps[2] + exps[3])

    onehot = jnp.zeros((tm, LANES), F32)
    for i in idxs:
        onehot = onehot + (lane_f == i).astype(F32)
    before = _dot(tril_ref[...], _mx(onehot)) + carry_ref[...]
    packed_i = jnp.zeros((tm, LANES), F32)
    packed_g = jnp.zeros((tm, LANES), F32)
    for k in range(TOP_K):
        rank = jnp.sum(jnp.where(lane_f == idxs[k], before, 0.0), -1, keepdims=True)
        packed_i = jnp.where(lane == k, idxs[k], packed_i)
        packed_i = jnp.where(lane == TOP_K + k, rank, packed_i)
        packed_g = jnp.where(lane == k, exps[k] * inv, packed_g)
    ri_ref[...] = packed_i.astype(jnp.int32)
    rg_ref[...] = packed_g
    carry_ref[...] = carry_ref[...] + jnp.sum(onehot, 0, keepdims=True)
    cnt_ref[...] = carry_ref[...]


def _post_mix_call(o, h0, wo, g, b, rw, rb, tm):
    t = o.shape[0]
    const = lambda i: (0, 0)
    return pl.pallas_call(
        functools.partial(_post_mix_kernel, tm=tm),
        grid=(t // tm,),
        in_specs=[
            pl.BlockSpec((tm, D_MODEL), lambda i: (i, 0)),
            pl.BlockSpec((tm, D_MODEL), lambda i: (i, 0)),
            pl.BlockSpec((D_MODEL, D_MODEL), const),
            pl.BlockSpec((1, D_MODEL), const),
            pl.BlockSpec((1, D_MODEL), const),
            pl.BlockSpec((D_MODEL, LANES), const),
            pl.BlockSpec((1, LANES), const),
        ],
        out_specs=[
            pl.BlockSpec((tm, D_MODEL), lambda i: (i, 0)),
            pl.BlockSpec((tm, LANES), lambda i: (i, 0)),
            pl.BlockSpec((tm, LANES), lambda i: (i, 0)),
            pl.BlockSpec((1, LANES), const),
        ],
        out_shape=[
            jax.ShapeDtypeStruct((t, D_MODEL), F32),
            jax.ShapeDtypeStruct((t, LANES), jnp.int32),
            jax.ShapeDtypeStruct((t, LANES), F32),
            jax.ShapeDtypeStruct((1, LANES), F32),
        ],
        scratch_shapes=[pltpu.VMEM((tm, tm), MXU_DTYPE), pltpu.VMEM((1, LANES), F32)],
        compiler_params=pltpu.CompilerParams(
            dimension_semantics=("arbitrary",), vmem_limit_bytes=VMEM_LIMIT),
        name="post_mix_router",
    )(o, h0, wo, g, b, rw, rb)


def _moe_kernel(be_ref, nv_ref, src_ref, dst_ref, h1_hbm, w1_ref, b1_ref, w2_ref, b2_ref,
                y_hbm, xbuf, ybuf, sem_in, sem_out):
    g = pl.program_id(0)
    nv = nv_ref[g]

    @pl.when(g == 0)
    def _():
        xbuf[...] = jnp.zeros_like(xbuf)

    def row_in(r):
        return pltpu.make_async_copy(
            h1_hbm.at[pl.ds(src_ref[0, 0, r], 1), :], xbuf.at[pl.ds(r, 1), :], sem_in)

    def row_out(r):
        return pltpu.make_async_copy(
            ybuf.at[pl.ds(r, 1), :], y_hbm.at[pl.ds(dst_ref[0, 0, r], 1), :], sem_out)

    @pl.when(nv > 0)
    def _():
        def start_in(r, c):
            row_in(r).start()
            return c
        lax.fori_loop(0, nv, start_in, 0)

        def wait_in(r, c):
            row_in(r).wait()
            return c
        lax.fori_loop(0, nv, wait_in, 0)

        hh = _dot(_mx(xbuf[...]), w1_ref[0]) + b1_ref[0]
        x_glu = jnp.minimum(hh[:, :D_FF], SWIGLU_LIMIT)
        x_lin = jnp.clip(hh[:, D_FF:], -SWIGLU_LIMIT, SWIGLU_LIMIT)
        act = x_glu * (1.0 / (1.0 + jnp.exp(-SWIGLU_ALPHA * x_glu))) * (x_lin + 1.0)
        ybuf[...] = _dot(_mx(act), w2_ref[0]) + b2_ref[0]

        def start_out(r, c):
            row_out(r).start()
            return c
        lax.fori_loop(0, nv, start_out, 0)

        def wait_out(r, c):
            row_out(r).wait()
            return c
        lax.fori_loop(0, nv, wait_out, 0)


def _moe_call(block_expert, nvalid, slot_src, slot_dst, h1, w1, b1, w2, b2, n_out_rows):
    n_blocks = block_expert.shape[0]
    grid_spec = pltpu.PrefetchScalarGridSpec(
        num_scalar_prefetch=2,
        grid=(n_blocks,),
        in_specs=[
            pl.BlockSpec((1, 1, MOE_BLOCK), lambda g, be, nv: (g, 0, 0), memory_space=pltpu.SMEM),
            pl.BlockSpec((1, 1, MOE_BLOCK), lambda g, be, nv: (g, 0, 0), memory_space=pltpu.SMEM),
            pl.BlockSpec(memory_space=pl.ANY),
            pl.BlockSpec((1, D_MODEL, 2 * D_FF), lambda g, be, nv: (be[g], 0, 0)),
            pl.BlockSpec((1, 1, 2 * D_FF), lambda g, be, nv: (be[g], 0, 0)),
            pl.BlockSpec((1, D_FF, D_MODEL), lambda g, be, nv: (be[g], 0, 0)),
            pl.BlockSpec((1, 1, D_MODEL), lambda g, be, nv: (be[g], 0, 0)),
        ],
        out_specs=pl.BlockSpec(memory_space=pl.ANY),
        scratch_shapes=[
            pltpu.VMEM((MOE_BLOCK, D_MODEL), F32),
            pltpu.VMEM((MOE_BLOCK, D_MODEL), F32),
            pltpu.SemaphoreType.DMA(()),
            pltpu.SemaphoreType.DMA(()),
        ],
    )
    return pl.pallas_call(
        _moe_kernel,
        grid_spec=grid_spec,
        out_shape=jax.ShapeDtypeStruct((n_out_rows, D_MODEL), F32),
        compiler_params=pltpu.CompilerParams(
            dimension_semantics=("arbitrary",), vmem_limit_bytes=VMEM_LIMIT),
        name="moe_experts",
    )(block_expert, nvalid, slot_src, slot_dst, h1, w1, b1, w2, b2)


def _combine_kernel(y_ref, gate_ref, h1_ref, g_ref, b_ref, out_ref):
    gates = gate_ref[...]
    acc = DEEPNORM_ALPHA * h1_ref[...]
    for k in range(TOP_K):
        acc = acc + gates[:, k:k + 1] * y_ref[k]
    out_ref[...] = _layer_norm(acc, g_ref[...], b_ref[...])


def _combine_call(y, gates, h1, g, b, tm):
    t = h1.shape[0]
    const = lambda i: (0, 0)
    return pl.pallas_call(
        _combine_kernel,
        grid=(t // tm,),
        in_specs=[
            pl.BlockSpec((TOP_K, tm, D_MODEL), lambda i: (0, i, 0)),
            pl.BlockSpec((tm, LANES), lambda i: (i, 0)),
            pl.BlockSpec((tm, D_MODEL), lambda i: (i, 0)),
            pl.BlockSpec((1, D_MODEL), const),
            pl.BlockSpec((1, D_MODEL), const),
        ],
        out_specs=pl.BlockSpec((tm, D_MODEL), lambda i: (i, 0)),
        out_shape=jax.ShapeDtypeStruct((t, D_MODEL), F32),
        compiler_params=pltpu.CompilerParams(
            dimension_semantics=("arbitrary",), vmem_limit_bytes=VMEM_LIMIT),
        name="combine_ln2",
    )(y, gates, h1, g, b)


def _relayout_w_in(w):
    sizes = (256, 256, 512, 512, GATE_RANK, 256, 256, 512, 512)
    offs = np.concatenate([[0], np.cumsum(sizes)])
    gq, gk, gv, gg, glr, rq, rk, rv, rg = [w[:, offs[i]:offs[i + 1]] for i in range(9)]
    perm = np.zeros((256,), np.int32)
    for p in range(2):
        for l in range(LANES):
            part, hh, f = l // 64, (l % 64) // 32, l % 32
            perm[p * LANES + l] = (2 * p + hh) * DK + part * (DK // 2) + f
    glr = jnp.pad(glr, ((0, 0), (0, LANES - GATE_RANK)))
    return jnp.concatenate([gq, gk, gv, gg, glr, rq[:, perm], rk[:, perm], rv, rg], axis=1)


def kernel(x, positions, ln_in_g, ln_in_b, w_in, gla_gate_w, gla_gate_b, gla_norm_g, ret_norm_g,
           ret_norm_b, w_out, ln1_g, ln1_b, router_w, router_b, moe_w1, moe_b1, moe_w2, moe_b2,
           ln2_g, ln2_b):
    bsz, seq, d = x.shape
    assert d == D_MODEL and seq % CHUNK == 0 and w_in.shape[0] == DEPTH == 1
    t = bsz * seq
    ts = min(512, seq)
    tm = min(512, t)
    assert seq % ts == 0 and t % tm == 0
    row = lambda v: v.reshape(1, -1).astype(F32)

    w_p = _relayout_w_in(w_in[0]).astype(MXU_DTYPE)
    h0, proj = _ln_proj_call(x.reshape(t, d), row(ln_in_g), row(ln_in_b), w_p, tm)

    gw = jnp.pad(gla_gate_w[0], ((0, LANES - GATE_RANK), (0, 0))).astype(MXU_DTYPE)
    half = DK // 2
    inv_freq = 1.0 / (ROPE_BASE ** np.linspace(0.0, 1.0, half, dtype=np.float32))
    invf = jnp.asarray(np.tile(inv_freq, LANES // half).reshape(1, LANES), F32)
    sgn = jnp.asarray(np.where(np.arange(LANES) < LANES // 2, -1.0, 1.0).reshape(1, LANES), F32)
    o = _mixer_call(proj, positions.reshape(t, 1), gw, row(gla_gate_b[0]), row(gla_norm_g[0]),
                    row(ret_norm_g[0]), row(ret_norm_b[0]), invf, sgn, bsz, seq, ts)

    rw = jnp.pad(router_w[0], ((0, 0), (0, LANES - N_EXPERTS))).astype(MXU_DTYPE)
    rb = jnp.pad(router_b[0], (0, LANES - N_EXPERTS)).reshape(1, LANES).astype(F32)
    h1, r_i, r_g, cnt = _post_mix_call(o, h0, w_out[0].astype(MXU_DTYPE), row(ln1_g[0]),
                                       row(ln1_b[0]), rw, rb, tm)

    e = r_i[:, :TOP_K]
    rank = r_i[:, TOP_K:2 * TOP_K]
    counts = cnt[0, :N_EXPERTS].astype(jnp.int32)
    padded = ((counts + MOE_BLOCK - 1) // MOE_BLOCK) * MOE_BLOCK
    pad_end = jnp.cumsum(padded)
    pad_start = pad_end - padded
    tk = t * TOP_K
    n_slots = ((tk + MOE_BLOCK - 1) // MOE_BLOCK) * MOE_BLOCK + N_EXPERTS * MOE_BLOCK
    n_blocks = n_slots // MOE_BLOCK
    dest = (pad_start[e] + rank).reshape(-1)
    tok = jnp.broadcast_to(jnp.arange(t, dtype=jnp.int32)[:, None], (t, TOP_K))
    kk = jnp.arange(TOP_K, dtype=jnp.int32)[None, :]
    slot_src = jnp.zeros((n_slots,), jnp.int32).at[dest].set(tok.reshape(-1))
    slot_dst = jnp.zeros((n_slots,), jnp.int32).at[dest].set((kk * t + tok).reshape(-1))
    block_start = jnp.arange(n_blocks, dtype=jnp.int32) * MOE_BLOCK
    block_expert = jnp.minimum(jnp.searchsorted(pad_end, block_start, side='right'),
                               N_EXPERTS - 1).astype(jnp.int32)
    nvalid = jnp.clip(counts[block_expert] - (block_start - pad_start[block_expert]),
                      0, MOE_BLOCK).astype(jnp.int32)

    y = _moe_call(block_expert, nvalid, slot_src.reshape(n_blocks, 1, MOE_BLOCK),
                  slot_dst.reshape(n_blocks, 1, MOE_BLOCK), h1,
                  moe_w1[0].astype(MXU_DTYPE), moe_b1[0].reshape(N_EXPERTS, 1, 2 * D_FF),
                  moe_w2[0].astype(MXU_DTYPE), moe_b2[0].reshape(N_EXPERTS, 1, D_MODEL), tk)

    out = _combine_call(y.reshape(TOP_K, t, d), r_g, h1, row(ln2_g[0]), row(ln2_b[0]), min(256, t))
    return out.reshape(bsz, seq, d)
```

```python
import functools
import math

import numpy as np
import jax
import jax.numpy as jnp
from jax import lax
from jax.experimental import pallas as pl
from jax.experimental.pallas import tpu as pltpu

F32 = jnp.float32
MXU_DTYPE = jnp.bfloat16

D_MODEL = 1024
CHUNK = 64
HEADS = 4
DK = 64
DV = 128
GATE_RANK = 16
GATE_NORM = 16.0
ROPE_BASE = 10000.0
N_EXPERTS = 32
TOP_K = 4
D_FF = 1024
SWIGLU_ALPHA = 1.702
SWIGLU_LIMIT = 7.0
MOE_BLOCK = 256
DMA_UNROLL = 8
LN_EPS = 1e-5
DEPTH = 1
DEEPNORM_ALPHA = (2.0 * DEPTH) ** 0.25

LANES = 128
VMEM_LIMIT = 56 * 1024 * 1024

C_GQ, C_GK, C_GV, C_GG, C_GLR = 0, 256, 512, 1024, 1536
C_RQ, C_RK, C_RV, C_RG = 1664, 1920, 2176, 2688
PROJ_W = 3200

LOG_GAMMA = [math.log1p(-(2.0 ** (-5.0 - h))) for h in range(HEADS)]


def _dot(a, b):
    return jnp.dot(a, b, preferred_element_type=F32)


def _dot_nt(a, b):
    return lax.dot_general(a, b, (((1,), (1,)), ((), ())), preferred_element_type=F32)


def _dot_tn(a, b):
    return lax.dot_general(a, b, (((0,), (0,)), ((), ())), preferred_element_type=F32)


def _mx(a):
    return a.astype(MXU_DTYPE)


def _layer_norm(x, g, b):
    mu = jnp.mean(x, -1, keepdims=True)
    xc = x - mu
    var = jnp.mean(xc * xc, -1, keepdims=True)
    return xc * lax.rsqrt(var + LN_EPS) * g + b


def _silu(x):
    return x * (1.0 / (1.0 + jnp.exp(-x)))


TILE_ROWS = D_MODEL // LANES


def _load_token_tiles(ref, n):
    return jnp.concatenate([ref[pl.ds(j, n, stride=TILE_ROWS), :] for j in range(TILE_ROWS)], axis=1)


def _store_token_tiles(ref, rows):
    n = rows.shape[0]
    for j in range(TILE_ROWS):
        ref[pl.ds(j, n, stride=TILE_ROWS), :] = rows[:, j * LANES:(j + 1) * LANES]


def _ln_proj_kernel(x_ref, g_ref, b_ref, w_ref, h_ref, p_ref):
    h = _layer_norm(x_ref[...], g_ref[...], b_ref[...])
    h_ref[...] = h
    p_ref[...] = _dot(_mx(h), w_ref[...])


def _ln_proj_call(x2, g, b, w, tm):
    t = x2.shape[0]
    return pl.pallas_call(
        _ln_proj_kernel,
        grid=(t // tm,),
        in_specs=[
            pl.BlockSpec((tm, D_MODEL), lambda i: (i, 0)),
            pl.BlockSpec((1, D_MODEL), lambda i: (0, 0)),
            pl.BlockSpec((1, D_MODEL), lambda i: (0, 0)),
            pl.BlockSpec((D_MODEL, PROJ_W), lambda i: (0, 0)),
        ],
        out_specs=[
            pl.BlockSpec((tm, D_MODEL), lambda i: (i, 0)),
            pl.BlockSpec((tm, PROJ_W), lambda i: (i, 0)),
        ],
        out_shape=[
            jax.ShapeDtypeStruct((t, D_MODEL), F32),
            jax.ShapeDtypeStruct((t, PROJ_W), F32),
        ],
        compiler_params=pltpu.CompilerParams(
            dimension_semantics=("arbitrary",), vmem_limit_bytes=VMEM_LIMIT),
        name="ln_proj",
    )(x2, g, b, w)


def _mixer_kernel(p_ref, pos_ref, gw_ref, gb_ref, gn_ref, rng_ref, rnb_ref, invf_ref, sgn_ref,
                  o_ref, st_ref, *, n_chunks):
    @pl.when(pl.program_id(1) == 0)
    def _():
        st_ref[...] = jnp.zeros_like(st_ref)

    row = lax.broadcasted_iota(jnp.int32, (CHUNK, CHUNK), 0)
    col = lax.broadcasted_iota(jnp.int32, (CHUNK, CHUNK), 1)
    lower = row >= col
    tri = lower.astype(MXU_DTYPE)
    dist = jnp.abs(row - col).astype(F32)
    rowf = lax.broadcasted_iota(jnp.int32, (CHUNK, LANES), 0).astype(F32)
    lane = lax.broadcasted_iota(jnp.int32, (CHUNK, LANES), 1)
    gla_half = [((lane >> 6) & 1) == i for i in range(2)]
    ret_half = [((lane >> 5) & 1) == i for i in range(2)]
    ret_d = [jnp.exp(LOG_GAMMA[h] * dist) for h in range(HEADS)]
    ret_eb = [jnp.exp(LOG_GAMMA[h] * (rowf + 1.0)) for h in range(HEADS)]
    ret_ek = [jnp.exp(LOG_GAMMA[h] * (CHUNK - 1.0 - rowf)) for h in range(HEADS)]
    ret_dec = [math.exp(LOG_GAMMA[h] * CHUNK) for h in range(HEADS)]

    def chunk_body(c, carry):
        r0 = pl.multiple_of(c * CHUNK, CHUNK)
        rows = pl.ds(r0, CHUNK)

        z = _dot(_mx(p_ref[rows, C_GLR:C_GLR + LANES]), gw_ref[...]) + gb_ref[...]
        la = (jnp.minimum(z, 0.0) - jnp.log1p(jnp.exp(-jnp.abs(z)))) * (1.0 / GATE_NORM)
        la_hi = _mx(la)
        r1 = la - la_hi.astype(F32)
        la_mid = _mx(r1)
        la_lo = _mx(r1 - la_mid.astype(F32))
        b = _dot(tri, la_hi) + _dot(tri, la_mid) + _dot(tri, la_lo)
        b_last = b[CHUNK - 1:CHUNK, :]
        eb = jnp.exp(b)
        enb = jnp.exp(-b)
        ekv = jnp.exp(b_last - b)
        dec = jnp.exp(b_last)
        q = p_ref[rows, C_GQ:C_GQ + 256] * (DK ** -0.5)
        k = p_ref[rows, C_GK:C_GK + 256]
        qe, qn = q * eb, q * enb
        ke, kn, kk = k * eb, k * enb, k * ekv
        for h in range(HEADS):
            p, half = h // 2, h % 2
            ls = slice(p * LANES, (p + 1) * LANES)
            m = gla_half[half]
            qe_m = _mx(jnp.where(m, qe[:, ls], 0.0))
            qn_m = _mx(jnp.where(m, qn[:, ls], 0.0))
            s_lo = _dot_nt(qe_m, _mx(kn[:, ls]))
            s_up = _dot_nt(qn_m, _mx(ke[:, ls]))
            sc = jnp.where(lower, s_lo, s_up)
            v = _mx(p_ref[rows, C_GV + h * DV:C_GV + (h + 1) * DV])
            st = st_ref[h]
            o = _dot(_mx(sc), v) + _dot_nt(qe_m, _mx(st))
            st_ref[h] = st * dec[:, ls] + _dot_tn(v, _mx(kk[:, ls]))
            o = o * lax.rsqrt(jnp.mean(o * o, -1, keepdims=True) + LN_EPS) * gn_ref[...]
            o = o * _silu(p_ref[rows, C_GG + h * DV:C_GG + (h + 1) * DV])
            o_ref[rows, h * DV:(h + 1) * DV] = o.astype(o_ref.dtype)

        ang = pos_ref[rows, :].astype(F32) * invf_ref[...]
        cs = jnp.cos(ang)
        sn = jnp.sin(ang) * sgn_ref[...]
        for p in range(2):
            ls = slice(p * LANES, (p + 1) * LANES)
            tq = p_ref[rows, C_RQ + p * LANES:C_RQ + (p + 1) * LANES]
            tk = p_ref[rows, C_RK + p * LANES:C_RK + (p + 1) * LANES] * (DK ** -0.5)
            rq = tq * cs + pltpu.roll(tq, LANES // 2, 1) * sn
            rk = tk * cs + pltpu.roll(tk, LANES // 2, 1) * sn
            rk_m = _mx(rk)
            for half in range(2):
                h = 2 * p + half
                hs = HEADS + h
                q_m = _mx(jnp.where(ret_half[half], rq, 0.0))
                s = _dot_nt(q_m, rk_m) * ret_d[h]
                v = _mx(p_ref[rows, C_RV + h * DV:C_RV + (h + 1) * DV])
                st = st_ref[hs]
                o = _dot(_mx(s), v) + ret_eb[h] * _dot_nt(q_m, _mx(st))
                st_ref[hs] = st * ret_dec[h] + _dot_tn(v, _mx(rk * ret_ek[h]))
                mu = jnp.mean(o, -1, keepdims=True)
                oc = o - mu
                var = jnp.mean(oc * oc, -1, keepdims=True)
                o = oc * lax.rsqrt(var + LN_EPS) * rng_ref[:, h * DV:(h + 1) * DV] \
                    + rnb_ref[:, h * DV:(h + 1) * DV]
                o = o * _silu(p_ref[rows, C_RG + h * DV:C_RG + (h + 1) * DV])
                o_ref[rows, (HEADS + h) * DV:(HEADS + h + 1) * DV] = o.astype(o_ref.dtype)
        return carry

    lax.fori_loop(0, n_chunks, chunk_body, 0)


def _mixer_call(proj, pos, gw, gb, gn, rng, rnb, invf, sgn, bsz, seq, ts):
    t = bsz * seq
    nst = seq // ts
    const = lambda b, s: (0, 0)
    return pl.pallas_call(
        functools.partial(_mixer_kernel, n_chunks=ts // CHUNK),
        grid=(bsz, nst),
        in_specs=[
            pl.BlockSpec((ts, PROJ_W), lambda b, s: (b * nst + s, 0)),
            pl.BlockSpec((ts, 1), lambda b, s: (b * nst + s, 0)),
            pl.BlockSpec((LANES, 256), const),
            pl.BlockSpec((1, 256), const),
            pl.BlockSpec((1, DV), const),
            pl.BlockSpec((1, HEADS * DV), const),
            pl.BlockSpec((1, HEADS * DV), const),
            pl.BlockSpec((1, LANES), const),
            pl.BlockSpec((1, LANES), const),
        ],
        out_specs=pl.BlockSpec((ts, D_MODEL), lambda b, s: (b * nst + s, 0)),
        out_shape=jax.ShapeDtypeStruct((t, D_MODEL), MXU_DTYPE),
        scratch_shapes=[pltpu.VMEM((2 * HEADS, DV, LANES), F32)],
        compiler_params=pltpu.CompilerParams(
            dimension_semantics=("arbitrary", "arbitrary"), vmem_limit_bytes=VMEM_LIMIT),
        name="mixers",
    )(proj, pos, gw, gb, gn, rng, rnb, invf, sgn)


def _post_mix_kernel(o_ref, h0_ref, wo_ref, g_ref, b_ref, rw_ref, rb_ref,
                     h1_ref, h1t_ref, ri_ref, rg_ref, cnt_ref, tril_ref, carry_ref, *, tm):
    @pl.when(pl.program_id(0) == 0)
    def _():
        r = lax.broadcasted_iota(jnp.int32, (tm, tm), 0)
        c = lax.broadcasted_iota(jnp.int32, (tm, tm), 1)
        tril_ref[...] = (r > c).astype(MXU_DTYPE)
        carry_ref[...] = jnp.zeros_like(carry_ref)

    mix = _dot(o_ref[...], wo_ref[...])
    h1 = _layer_norm(DEEPNORM_ALPHA * h0_ref[...] + mix, g_ref[...], b_ref[...])
    h1_ref[...] = h1
    _store_token_tiles(h1t_ref, h1)

    lane = lax.broadcasted_iota(jnp.int32, (tm, LANES), 1)
    lane_f = lane.astype(F32)
    logits = _dot(_mx(h1), rw_ref[...]) + rb_ref[...]
    l = jnp.where(lane < N_EXPERTS, logits, -jnp.inf)
    vals, idxs = [], []
    for _ in range(TOP_K):
        m = jnp.max(l, -1, keepdims=True)
        i = jnp.min(jnp.where(l == m, lane_f, float(LANES)), -1, keepdims=True)
        vals.append(m)
        idxs.append(i)
        l = jnp.where(lane_f == i, -jnp.inf, l)
    exps = [jnp.exp(v - vals[0]) for v in vals]
    inv = 1.0 / (exps[0] + exps[1] + exps[2] + exps[3])

    onehot = jnp.zeros((tm, LANES), F32)
    for i in idxs:
        onehot = onehot + (lane_f == i).astype(F32)
    before = _dot(tril_ref[...], _mx(onehot)) + carry_ref[...]
    packed_i = jnp.zeros((tm, LANES), F32)
    packed_g = jnp.zeros((tm, LANES), F32)
    for k in range(TOP_K):
        rank = jnp.sum(jnp.where(lane_f == idxs[k], before, 0.0), -1, keepdims=True)
        packed_i = jnp.where(lane == k, idxs[k], packed_i)
        packed_i = jnp.where(lane == TOP_K + k, rank, packed_i)
        packed_g = jnp.where(lane == k, exps[k] * inv, packed_g)
    ri_ref[...] = packed_i.astype(jnp.int32)
    rg_ref[...] = packed_g
    carry_ref[...] = carry_ref[...] + jnp.sum(onehot, 0, keepdims=True)
    cnt_ref[...] = carry_ref[...]


def _post_mix_call(o, h0, wo, g, b, rw, rb, tm):
    t = o.shape[0]
    const = lambda i: (0, 0)
    return pl.pallas_call(
        functools.partial(_post_mix_kernel, tm=tm),
        grid=(t // tm,),
        in_specs=[
            pl.BlockSpec((tm, D_MODEL), lambda i: (i, 0)),
            pl.BlockSpec((tm, D_MODEL), lambda i: (i, 0)),
            pl.BlockSpec((D_MODEL, D_MODEL), const),
            pl.BlockSpec((1, D_MODEL), const),
            pl.BlockSpec((1, D_MODEL), const),
            pl.BlockSpec((D_MODEL, LANES), const),
            pl.BlockSpec((1, LANES), const),
        ],
        out_specs=[
            pl.BlockSpec((tm, D_MODEL), lambda i: (i, 0)),
            pl.BlockSpec((tm * TILE_ROWS, LANES), lambda i: (i, 0)),
            pl.BlockSpec((tm, LANES), lambda i: (i, 0)),
            pl.BlockSpec((tm, LANES), lambda i: (i, 0)),
            pl.BlockSpec((1, LANES), const),
        ],
        out_shape=[
            jax.ShapeDtypeStruct((t, D_MODEL), F32),
            jax.ShapeDtypeStruct((t * TILE_ROWS, LANES), F32),
            jax.ShapeDtypeStruct((t, LANES), jnp.int32),
            jax.ShapeDtypeStruct((t, LANES), F32),
            jax.ShapeDtypeStruct((1, LANES), F32),
        ],
        scratch_shapes=[pltpu.VMEM((tm, tm), MXU_DTYPE), pltpu.VMEM((1, LANES), F32)],
        compiler_params=pltpu.CompilerParams(
            dimension_semantics=("arbitrary",), vmem_limit_bytes=VMEM_LIMIT),
        name="post_mix_router",
    )(o, h0, wo, g, b, rw, rb)


def _moe_kernel(be_ref, src0_ref, src_ref, dst_ref, dstl_ref, h1_hbm,
                w1a_ref, b1a_ref, w2a_ref, b2a_ref, w1b_ref, b1b_ref, w2b_ref, b2b_ref,
                y_hbm, xa, xb, ya, yb, sem):
    i = pl.program_id(0)
    last = pl.num_programs(0) - 1

    def tile(ref, t):
        return ref.at[pl.ds(pl.multiple_of(t * TILE_ROWS, TILE_ROWS), TILE_ROWS), :]

    def row_in(tok, xbuf, r, s):
        return pltpu.make_async_copy(tile(h1_hbm, tok), tile(xbuf, r), sem.at[s])

    def row_out(ybuf, r, row, s):
        return pltpu.make_async_copy(tile(ybuf, r), tile(y_hbm, row), sem.at[s])

    def gather(tbl, off, xbuf, s):
        def body(it, c):
            for u in range(DMA_UNROLL):
                r = it * DMA_UNROLL + u
                row_in(tbl[0, 0, off + r], xbuf, r, s).start()
            return c
        lax.fori_loop(0, MOE_BLOCK // DMA_UNROLL, body, 0)

    def scatter(tbl, off, ybuf, s):
        def body(it, c):
            for u in range(DMA_UNROLL):
                r = it * DMA_UNROLL + u
                row_out(ybuf, r, tbl[0, 0, off + r], s).start()
            return c
        lax.fori_loop(0, MOE_BLOCK // DMA_UNROLL, body, 0)

    def wait_in(xbuf, s):
        for _ in range(MOE_BLOCK):
            row_in(0, xbuf, 0, s).wait()

    def wait_out(ybuf, s):
        for _ in range(MOE_BLOCK):
            row_out(ybuf, 0, 0, s).wait()

    def ffn(xbuf, ybuf, w1, b1, w2, b2):
        hh = _dot(_mx(_load_token_tiles(xbuf, MOE_BLOCK)), w1[0]) + b1[0]
        x_glu = jnp.minimum(hh[:, :D_FF], SWIGLU_LIMIT)
        x_lin = jnp.clip(hh[:, D_FF:], -SWIGLU_LIMIT, SWIGLU_LIMIT)
        act = x_glu * (1.0 / (1.0 + jnp.exp(-SWIGLU_ALPHA * x_glu))) * (x_lin + 1.0)
        _store_token_tiles(ybuf, _dot(_mx(act), w2[0]) + b2[0])

    @pl.when(i == 0)
    def _():
        yb[...] = jnp.zeros_like(yb)
        gather(src0_ref, 0, xa, 0)

    wait_in(xa, 0)

    @pl.when(i > 0)
    def _():
        wait_out(ya, 2)

    gather(src_ref, 0, xb, 1)
    scatter(dst_ref, 0, yb, 3)
    ffn(xa, ya, w1a_ref, b1a_ref, w2a_ref, b2a_ref)

    wait_in(xb, 1)
    wait_out(yb, 3)
    gather(src_ref, MOE_BLOCK, xa, 0)
    scatter(dst_ref, MOE_BLOCK, ya, 2)
    ffn(xb, yb, w1b_ref, b1b_ref, w2b_ref, b2b_ref)

    @pl.when(i == last)
    def _():
        wait_in(xa, 0)
        wait_out(ya, 2)
        scatter(dstl_ref, 0, yb, 3)
        wait_out(yb, 3)


def _moe_call(block_expert, src0, src_step, dst_step, dst_last, h1, w1, b1, w2, b2, n_out_rows):
    n_steps = src_step.shape[0]
    smem = functools.partial(pl.BlockSpec, memory_space=pltpu.SMEM)
    first = lambda i, be: (0, 0, 0)
    step = lambda i, be: (i, 0, 0)
    ex_a = lambda i, be: (be[2 * i], 0, 0)
    ex_b = lambda i, be: (be[2 * i + 1], 0, 0)

    def expert_specs(ex):
        return [pl.BlockSpec((1, D_MODEL, 2 * D_FF), ex), pl.BlockSpec((1, 1, 2 * D_FF), ex),
                pl.BlockSpec((1, D_FF, D_MODEL), ex), pl.BlockSpec((1, 1, D_MODEL), ex)]

    grid_spec = pltpu.PrefetchScalarGridSpec(
        num_scalar_prefetch=1,
        grid=(n_steps,),
        in_specs=[
            smem((1, 1, MOE_BLOCK), first),
            smem((1, 1, 2 * MOE_BLOCK), step),
            smem((1, 1, 2 * MOE_BLOCK), step),
            smem((1, 1, MOE_BLOCK), first),
            pl.BlockSpec(memory_space=pl.ANY),
        ] + expert_specs(ex_a) + expert_specs(ex_b),
        out_specs=pl.BlockSpec(memory_space=pl.ANY),
        scratch_shapes=[pltpu.VMEM((MOE_BLOCK * TILE_ROWS, LANES), F32)] * 4
        + [pltpu.SemaphoreType.DMA((4,))],
    )
    return pl.pallas_call(
        _moe_kernel,
        grid_spec=grid_spec,
        out_shape=jax.ShapeDtypeStruct((n_out_rows * TILE_ROWS, LANES), F32),
        compiler_params=pltpu.CompilerParams(
            dimension_semantics=("arbitrary",), vmem_limit_bytes=VMEM_LIMIT),
        name="moe_experts",
    )(block_expert, src0, src_step, dst_step, dst_last, h1, w1, b1, w2, b2, w1, b1, w2, b2)


def _combine_kernel(y0_ref, y1_ref, y2_ref, y3_ref, gate_ref, h1_ref, g_ref, b_ref, out_ref):
    gates = gate_ref[...]
    acc = DEEPNORM_ALPHA * h1_ref[...]
    for k, y_ref in enumerate((y0_ref, y1_ref, y2_ref, y3_ref)):
        acc = acc + gates[:, k:k + 1] * _load_token_tiles(y_ref, acc.shape[0])
    out_ref[...] = _layer_norm(acc, g_ref[...], b_ref[...])


def _combine_call(y, gates, h1, g, b, tm):
    t = h1.shape[0]
    nt = t // tm
    const = lambda i: (0, 0)
    y_specs = [pl.BlockSpec((tm * TILE_ROWS, LANES), functools.partial(lambda i, k: (k * nt + i, 0), k=k))
               for k in range(TOP_K)]
    return pl.pallas_call(
        _combine_kernel,
        grid=(nt,),
        in_specs=y_specs + [
            pl.BlockSpec((tm, LANES), lambda i: (i, 0)),
            pl.BlockSpec((tm, D_MODEL), lambda i: (i, 0)),
            pl.BlockSpec((1, D_MODEL), const),
            pl.BlockSpec((1, D_MODEL), const),
        ],
        out_specs=pl.BlockSpec((tm, D_MODEL), lambda i: (i, 0)),
        out_shape=jax.ShapeDtypeStruct((t, D_MODEL), F32),
        compiler_params=pltpu.CompilerParams(
            dimension_semantics=("arbitrary",), vmem_limit_bytes=VMEM_LIMIT),
        name="combine_ln2",
    )(y, y, y, y, gates, h1, g, b)


def _relayout_w_in(w):
    sizes = (256, 256, 512, 512, GATE_RANK, 256, 256, 512, 512)
    offs = np.concatenate([[0], np.cumsum(sizes)])
    gq, gk, gv, gg, glr, rq, rk, rv, rg = [w[:, offs[i]:offs[i + 1]] for i in range(9)]
    perm = np.zeros((256,), np.int32)
    for p in range(2):
        for l in range(LANES):
            part, hh, f = l // 64, (l % 64) // 32, l % 32
            perm[p * LANES + l] = (2 * p + hh) * DK + part * (DK // 2) + f
    glr = jnp.pad(glr, ((0, 0), (0, LANES - GATE_RANK)))
    return jnp.concatenate([gq, gk, gv, gg, glr, rq[:, perm], rk[:, perm], rv, rg], axis=1)


def kernel(x, positions, ln_in_g, ln_in_b, w_in, gla_gate_w, gla_gate_b, gla_norm_g, ret_norm_g,
           ret_norm_b, w_out, ln1_g, ln1_b, router_w, router_b, moe_w1, moe_b1, moe_w2, moe_b2,
           ln2_g, ln2_b):
    bsz, seq, d = x.shape
    assert d == D_MODEL and seq % CHUNK == 0 and w_in.shape[0] == DEPTH == 1
    t = bsz * seq
    ts = min(512, seq)
    tm = min(512, t)
    assert seq % ts == 0 and t % tm == 0
    row = lambda v: v.reshape(1, -1).astype(F32)

    w_p = _relayout_w_in(w_in[0]).astype(MXU_DTYPE)
    h0, proj = _ln_proj_call(x.reshape(t, d), row(ln_in_g), row(ln_in_b), w_p, tm)

    gw = jnp.pad(gla_gate_w[0], ((0, LANES - GATE_RANK), (0, 0))).astype(MXU_DTYPE)
    half = DK // 2
    inv_freq = 1.0 / (ROPE_BASE ** np.linspace(0.0, 1.0, half, dtype=np.float32))
    invf = jnp.asarray(np.tile(inv_freq, LANES // half).reshape(1, LANES), F32)
    sgn = jnp.asarray(np.where(np.arange(LANES) < LANES // 2, -1.0, 1.0).reshape(1, LANES), F32)
    o = _mixer_call(proj, positions.reshape(t, 1), gw, row(gla_gate_b[0]), row(gla_norm_g[0]),
                    row(ret_norm_g[0]), row(ret_norm_b[0]), invf, sgn, bsz, seq, ts)

    rw = jnp.pad(router_w[0], ((0, 0), (0, LANES - N_EXPERTS))).astype(MXU_DTYPE)
    rb = jnp.pad(router_b[0], (0, LANES - N_EXPERTS)).reshape(1, LANES).astype(F32)
    h1, h1t, r_i, r_g, cnt = _post_mix_call(o, h0, w_out[0].astype(MXU_DTYPE), row(ln1_g[0]),
                                       row(ln1_b[0]), rw, rb, tm)

    e = r_i[:, :TOP_K]
    rank = r_i[:, TOP_K:2 * TOP_K]
    counts = cnt[0, :N_EXPERTS].astype(jnp.int32)
    padded = ((counts + MOE_BLOCK - 1) // MOE_BLOCK) * MOE_BLOCK
    pad_end = jnp.cumsum(padded)
    pad_start = pad_end - padded
    tk = t * TOP_K
    n_blocks = (tk + MOE_BLOCK - 1) // MOE_BLOCK + N_EXPERTS
    n_slots = n_blocks * MOE_BLOCK
    assert n_blocks % 2 == 0
    i32 = jnp.int32
    ex = jnp.arange(N_EXPERTS, dtype=i32)
    dest = (jnp.sum(jnp.where(e[..., None] == ex, pad_start, 0), -1) + rank).reshape(-1)
    gap = padded - counts
    gap_end = jnp.cumsum(gap)
    j = jnp.arange(n_slots - tk, dtype=i32)
    je = jnp.sum(j[:, None] >= gap_end[None, :], -1)
    base = jnp.concatenate([pad_start + counts - (gap_end - gap), pad_end[-1:] - gap_end[-1:]])
    pad_slot = j + jnp.sum(jnp.where(je[:, None] == jnp.arange(N_EXPERTS + 1, dtype=i32), base, 0), -1)
    keys = jnp.concatenate([dest, pad_slot])
    vals = jnp.concatenate([jnp.arange(tk, dtype=i32), jnp.full((n_slots - tk,), -1, i32)])
    _, slot_flat = lax.sort((keys, vals), num_keys=1)
    slot_tok = slot_flat >> 2
    slot_src = jnp.where(slot_flat >= 0, slot_tok, 0)
    dump = tk + jnp.arange(MOE_BLOCK, dtype=i32)
    slot_dst = jnp.where(slot_flat >= 0, (slot_flat & 3) * t + slot_tok,
                         jnp.tile(dump, n_blocks))
    block_start = jnp.arange(n_blocks, dtype=i32) * MOE_BLOCK
    block_expert = jnp.minimum(jnp.sum(block_start[:, None] >= pad_end[None, :], -1),
                               N_EXPERTS - 1).astype(i32)
    n_steps = n_blocks // 2
    src_step = jnp.concatenate([slot_src[MOE_BLOCK:], jnp.zeros((MOE_BLOCK,), i32)])
    dst_step = jnp.concatenate([dump, slot_dst[:-MOE_BLOCK]])

    y = _moe_call(block_expert, slot_src[:MOE_BLOCK].reshape(1, 1, MOE_BLOCK),
                  src_step.reshape(n_steps, 1, 2 * MOE_BLOCK), dst_step.reshape(n_steps, 1, 2 * MOE_BLOCK),
                  slot_dst[-MOE_BLOCK:].reshape(1, 1, MOE_BLOCK), h1t,
                  moe_w1[0].astype(MXU_DTYPE), moe_b1[0].reshape(N_EXPERTS, 1, 2 * D_FF),
                  moe_w2[0].astype(MXU_DTYPE), moe_b2[0].reshape(N_EXPERTS, 1, D_MODEL), tk + MOE_BLOCK)

    out = _combine_call(y, r_g, h1, row(ln2_g[0]), row(ln2_b[0]), min(256, t))
    return out.reshape(bsz, seq, d)
```

```python
import functools
import math

import numpy as np
import jax
import jax.numpy as jnp
from jax import lax
from jax.experimental import pallas as pl
from jax.experimental.pallas import tpu as pltpu

F32 = jnp.float32
MXU_DTYPE = jnp.bfloat16

D_MODEL = 1024
CHUNK = 64
GC = 4
GROUP = GC * CHUNK
HEADS = 4
DK = 64
DV = 128
GATE_RANK = 16
GATE_NORM = 16.0
ROPE_BASE = 10000.0
N_EXPERTS = 32
TOP_K = 4
D_FF = 1024
SWIGLU_ALPHA = 1.702
SWIGLU_LIMIT = 7.0
MOE_BLOCK = 256
DMA_UNROLL = 8
LN_EPS = 1e-5
DEPTH = 1
DEEPNORM_ALPHA = (2.0 * DEPTH) ** 0.25

LANES = 128
VMEM_LIMIT = 56 * 1024 * 1024

C_GQ, C_GK, C_GV, C_GG, C_GLR = 0, 256, 512, 1024, 1536
C_RQ, C_RK, C_RV, C_RG = 1664, 1920, 2176, 2688
PROJ_W = 3200

LOG_GAMMA = [math.log1p(-(2.0 ** (-5.0 - h))) for h in range(HEADS)]


def _dot(a, b):
    return jnp.dot(a, b, preferred_element_type=F32)


def _dot_nt(a, b):
    return lax.dot_general(a, b, (((1,), (1,)), ((), ())), preferred_element_type=F32)


def _dot_tn(a, b):
    return lax.dot_general(a, b, (((0,), (0,)), ((), ())), preferred_element_type=F32)


def _mx(a):
    return a.astype(MXU_DTYPE)


def _layer_norm(x, g, b):
    mu = jnp.mean(x, -1, keepdims=True)
    xc = x - mu
    var = jnp.mean(xc * xc, -1, keepdims=True)
    return xc * lax.rsqrt(var + LN_EPS) * g + b


def _silu(x):
    return x * (1.0 / (1.0 + jnp.exp(-x)))


TILE_ROWS = D_MODEL // LANES


def _load_token_tiles(ref, n):
    return jnp.concatenate([ref[pl.ds(j, n, stride=TILE_ROWS), :] for j in range(TILE_ROWS)], axis=1)


def _store_token_tiles(ref, rows):
    n = rows.shape[0]
    for j in range(TILE_ROWS):
        ref[pl.ds(j, n, stride=TILE_ROWS), :] = rows[:, j * LANES:(j + 1) * LANES]


def _ln_proj_kernel(x_ref, g_ref, b_ref, w_ref, h_ref, p_ref):
    h = _layer_norm(x_ref[...], g_ref[...], b_ref[...])
    h_ref[...] = h
    p_ref[...] = _dot(_mx(h), w_ref[...])


def _ln_proj_call(x2, g, b, w, tm):
    t = x2.shape[0]
    return pl.pallas_call(
        _ln_proj_kernel,
        grid=(t // tm,),
        in_specs=[
            pl.BlockSpec((tm, D_MODEL), lambda i: (i, 0)),
            pl.BlockSpec((1, D_MODEL), lambda i: (0, 0)),
            pl.BlockSpec((1, D_MODEL), lambda i: (0, 0)),
            pl.BlockSpec((D_MODEL, PROJ_W), lambda i: (0, 0)),
        ],
        out_specs=[
            pl.BlockSpec((tm, D_MODEL), lambda i: (i, 0)),
            pl.BlockSpec((tm, PROJ_W), lambda i: (i, 0)),
        ],
        out_shape=[
            jax.ShapeDtypeStruct((t, D_MODEL), F32),
            jax.ShapeDtypeStruct((t, PROJ_W), F32),
        ],
        compiler_params=pltpu.CompilerParams(
            dimension_semantics=("arbitrary",), vmem_limit_bytes=VMEM_LIMIT),
        name="ln_proj",
    )(x2, g, b, w)


def _mixer_kernel(p_ref, pos_ref, gw_ref, gb_ref, gn_ref, rng_ref, rnb_ref, invf_ref, sgn_ref,
                  cd_ref, sd_ref, o_ref, st_ref, *, n_groups):
    @pl.when(pl.program_id(1) == 0)
    def _():
        st_ref[...] = jnp.zeros_like(st_ref)

    rr = lax.broadcasted_iota(jnp.int32, (GROUP, GROUP), 0)
    cc = lax.broadcasted_iota(jnp.int32, (GROUP, GROUP), 1)
    same = (rr >> 6) == (cc >> 6)
    lower = same & (rr >= cc)
    upper = same & (rr < cc)
    tri = lower.astype(MXU_DTYPE)
    dist = jnp.abs(rr - cc).astype(F32)
    rowi = lax.broadcasted_iota(jnp.int32, (GROUP, LANES), 0)
    rin = (rowi & (CHUNK - 1)).astype(F32)
    in_chunk = [(rowi >> 6) == c for c in range(GC)]
    lane = lax.broadcasted_iota(jnp.int32, (GROUP, LANES), 1)
    gla_half = [((lane >> 6) & 1) == i for i in range(2)]
    ret_half = [((lane >> 5) & 1) == i for i in range(2)]
    ret_d = [jnp.where(same, jnp.exp(LOG_GAMMA[h] * dist), 0.0) for h in range(HEADS)]
    ret_eb = [jnp.exp(LOG_GAMMA[h] * (rin + 1.0)) for h in range(HEADS)]
    ret_ek = [jnp.exp(LOG_GAMMA[h] * (CHUNK - 1.0 - rin)) for h in range(HEADS)]
    ret_dec = [math.exp(LOG_GAMMA[h] * CHUNK) for h in range(HEADS)]

    def block_diag(x):
        return _mx(jnp.concatenate([jnp.where(in_chunk[c], x, 0.0) for c in range(GC)], axis=1))

    def recur(hs, q_m, kv_t, decay):
        st = st_ref[hs]
        parts = []
        for c in range(GC):
            parts.append(_dot_nt(q_m[c * CHUNK:(c + 1) * CHUNK], _mx(st)))
            st = st * decay(c) + kv_t[:, c * LANES:(c + 1) * LANES]
        st_ref[hs] = st
        return jnp.concatenate(parts, axis=0)

    def group_body(g, carry):
        rows = pl.ds(pl.multiple_of(g * GROUP, GROUP), GROUP)

        z = _dot(_mx(p_ref[rows, C_GLR:C_GLR + LANES]), gw_ref[...]) + gb_ref[...]
        la = (jnp.minimum(z, 0.0) - jnp.log1p(jnp.exp(-jnp.abs(z)))) * (1.0 / GATE_NORM)
        la_hi = _mx(la)
        r1 = la - la_hi.astype(F32)
        la_mid = _mx(r1)
        la_lo = _mx(r1 - la_mid.astype(F32))
        b = _dot(tri, la_hi) + _dot(tri, la_mid) + _dot(tri, la_lo)
        b_ends = [b[(c + 1) * CHUNK - 1:(c + 1) * CHUNK, :] for c in range(GC)]
        b_last = jnp.concatenate([jnp.broadcast_to(e, (CHUNK, 256)) for e in b_ends], axis=0)
        decs = [jnp.exp(e) for e in b_ends]
        eb = jnp.exp(b)
        enb = jnp.exp(-b)
        ekv = jnp.exp(b_last - b)
        q = p_ref[rows, C_GQ:C_GQ + 256] * (DK ** -0.5)
        k = p_ref[rows, C_GK:C_GK + 256]
        qe, qn = q * eb, q * enb
        ke, kn, kk = k * eb, k * enb, k * ekv
        for h in range(HEADS):
            p, half = h // 2, h % 2
            ls = slice(p * LANES, (p + 1) * LANES)
            m = gla_half[half]
            qe_m = _mx(jnp.where(m, qe[:, ls], 0.0))
            qn_m = _mx(jnp.where(m, qn[:, ls], 0.0))
            s_lo = _dot_nt(qe_m, _mx(kn[:, ls]))
            s_up = _dot_nt(qn_m, _mx(ke[:, ls]))
            sc = jnp.where(lower, s_lo, jnp.where(upper, s_up, 0.0))
            v = _mx(p_ref[rows, C_GV + h * DV:C_GV + (h + 1) * DV])
            kv_t = _dot_tn(v, block_diag(kk[:, ls]))
            o = _dot(_mx(sc), v) + recur(h, qe_m, kv_t, lambda c: decs[c][:, ls])
            o = o * lax.rsqrt(jnp.mean(o * o, -1, keepdims=True) + LN_EPS) * gn_ref[...]
            o = o * _silu(p_ref[rows, C_GG + h * DV:C_GG + (h + 1) * DV])
            o_ref[rows, h * DV:(h + 1) * DV] = o.astype(o_ref.dtype)

        base = pos_ref[pl.ds(pl.multiple_of(g * GROUP, GROUP), 1), :].astype(F32) * invf_ref[...]
        c0, s0 = jnp.cos(base), jnp.sin(base)
        cs = c0 * cd_ref[...] - s0 * sd_ref[...]
        sn = (s0 * sgn_ref[...]) * cd_ref[...] + (c0 * sgn_ref[...]) * sd_ref[...]
        for p in range(2):
            tq = p_ref[rows, C_RQ + p * LANES:C_RQ + (p + 1) * LANES]
            tk = p_ref[rows, C_RK + p * LANES:C_RK + (p + 1) * LANES] * (DK ** -0.5)
            rq = tq * cs + pltpu.roll(tq, LANES // 2, 1) * sn
            rk = tk * cs + pltpu.roll(tk, LANES // 2, 1) * sn
            rk_m = _mx(rk)
            for half in range(2):
                h = 2 * p + half
                q_m = _mx(jnp.where(ret_half[half], rq, 0.0))
                s = _dot_nt(q_m, rk_m) * ret_d[h]
                v = _mx(p_ref[rows, C_RV + h * DV:C_RV + (h + 1) * DV])
                kv_t = _dot_tn(v, block_diag(rk * ret_ek[h]))
                o = _dot(_mx(s), v) + ret_eb[h] * recur(HEADS + h, q_m, kv_t, lambda c: ret_dec[h])
                mu = jnp.mean(o, -1, keepdims=True)
                oc = o - mu
                var = jnp.mean(oc * oc, -1, keepdims=True)
                o = oc * lax.rsqrt(var + LN_EPS) * rng_ref[:, h * DV:(h + 1) * DV] \
                    + rnb_ref[:, h * DV:(h + 1) * DV]
                o = o * _silu(p_ref[rows, C_RG + h * DV:C_RG + (h + 1) * DV])
                o_ref[rows, (HEADS + h) * DV:(HEADS + h + 1) * DV] = o.astype(o_ref.dtype)
        return carry

    lax.fori_loop(0, n_groups, group_body, 0)


def _mixer_call(proj, pos, gw, gb, gn, rng, rnb, invf, sgn, cd, sd, bsz, seq, ts):
    t = bsz * seq
    nst = seq // ts
    const = lambda b, s: (0, 0)
    return pl.pallas_call(
        functools.partial(_mixer_kernel, n_groups=ts // GROUP),
        grid=(bsz, nst),
        in_specs=[
            pl.BlockSpec((ts, PROJ_W), lambda b, s: (b * nst + s, 0)),
            pl.BlockSpec((ts, 1), lambda b, s: (b * nst + s, 0)),
            pl.BlockSpec((LANES, 256), const),
            pl.BlockSpec((1, 256), const),
            pl.BlockSpec((1, DV), const),
            pl.BlockSpec((1, HEADS * DV), const),
            pl.BlockSpec((1, HEADS * DV), const),
            pl.BlockSpec((1, LANES), const),
            pl.BlockSpec((1, LANES), const),
            pl.BlockSpec((GROUP, LANES), const),
            pl.BlockSpec((GROUP, LANES), const),
        ],
        out_specs=pl.BlockSpec((ts, D_MODEL), lambda b, s: (b * nst + s, 0)),
        out_shape=jax.ShapeDtypeStruct((t, D_MODEL), MXU_DTYPE),
        scratch_shapes=[pltpu.VMEM((2 * HEADS, DV, LANES), F32)],
        compiler_params=pltpu.CompilerParams(
            dimension_semantics=("arbitrary", "arbitrary"), vmem_limit_bytes=VMEM_LIMIT),
        name="mixers",
    )(proj, pos, gw, gb, gn, rng, rnb, invf, sgn, cd, sd)


def _post_mix_kernel(o_ref, h0_ref, wo_ref, g_ref, b_ref, rw_ref, rb_ref,
                     h1_ref, h1t_ref, ri_ref, rg_ref, cnt_ref, tril_ref, carry_ref, *, tm):
    @pl.when(pl.program_id(0) == 0)
    def _():
        r = lax.broadcasted_iota(jnp.int32, (tm, tm), 0)
        c = lax.broadcasted_iota(jnp.int32, (tm, tm), 1)
        tril_ref[...] = (r > c).astype(MXU_DTYPE)
        carry_ref[...] = jnp.zeros_like(carry_ref)

    mix = _dot(o_ref[...], wo_ref[...])
    h1 = _layer_norm(DEEPNORM_ALPHA * h0_ref[...] + mix, g_ref[...], b_ref[...])
    h1_ref[...] = h1
    _store_token_tiles(h1t_ref, h1)

    lane = lax.broadcasted_iota(jnp.int32, (tm, LANES), 1)
    lane_f = lane.astype(F32)
    logits = _dot(_mx(h1), rw_ref[...]) + rb_ref[...]
    l = jnp.where(lane < N_EXPERTS, logits, -jnp.inf)
    vals, idxs = [], []
    for _ in range(TOP_K):
        m = jnp.max(l, -1, keepdims=True)
        i = jnp.min(jnp.where(l == m, lane_f, float(LANES)), -1, keepdims=True)
        vals.append(m)
        idxs.append(i)
        l = jnp.where(lane_f == i, -jnp.inf, l)
    exps = [jnp.exp(v - vals[0]) for v in vals]
    inv = 1.0 / (exps[0] + exps[1] + exps[2] + exps[3])

    onehot = jnp.zeros((tm, LANES), F32)
    for i in idxs:
        onehot = onehot + (lane_f == i).astype(F32)
    before = _dot(tril_ref[...], _mx(onehot)) + carry_ref[...]
    packed_i = jnp.zeros((tm, LANES), F32)
    packed_g = jnp.zeros((tm, LANES), F32)
    for k in range(TOP_K):
        rank = jnp.sum(jnp.where(lane_f == idxs[k], before, 0.0), -1, keepdims=True)
        packed_i = jnp.where(lane == k, idxs[k], packed_i)
        packed_i = jnp.where(lane == TOP_K + k, rank, packed_i)
        packed_g = jnp.where(lane == k, exps[k] * inv, packed_g)
    ri_ref[...] = packed_i.astype(jnp.int32)
    rg_ref[...] = packed_g
    carry_ref[...] = carry_ref[...] + jnp.sum(onehot, 0, keepdims=True)
    cnt_ref[...] = carry_ref[...]


def _post_mix_call(o, h0, wo, g, b, rw, rb, tm):
    t = o.shape[0]
    const = lambda i: (0, 0)
    return pl.pallas_call(
        functools.partial(_post_mix_kernel, tm=tm),
        grid=(t // tm,),
        in_specs=[
            pl.BlockSpec((tm, D_MODEL), lambda i: (i, 0)),
            pl.BlockSpec((tm, D_MODEL), lambda i: (i, 0)),
            pl.BlockSpec((D_MODEL, D_MODEL), const),
            pl.BlockSpec((1, D_MODEL), const),
            pl.BlockSpec((1, D_MODEL), const),
            pl.BlockSpec((D_MODEL, LANES), const),
            pl.BlockSpec((1, LANES), const),
        ],
        out_specs=[
            pl.BlockSpec((tm, D_MODEL), lambda i: (i, 0)),
            pl.BlockSpec((tm * TILE_ROWS, LANES), lambda i: (i, 0)),
            pl.BlockSpec((tm, LANES), lambda i: (i, 0)),
            pl.BlockSpec((tm, LANES), lambda i: (i, 0)),
            pl.BlockSpec((1, LANES), const),
        ],
        out_shape=[
            jax.ShapeDtypeStruct((t, D_MODEL), F32),
            jax.ShapeDtypeStruct((t * TILE_ROWS, LANES), F32),
            jax.ShapeDtypeStruct((t, LANES), jnp.int32),
            jax.ShapeDtypeStruct((t, LANES), F32),
            jax.ShapeDtypeStruct((1, LANES), F32),
        ],
        scratch_shapes=[pltpu.VMEM((tm, tm), MXU_DTYPE), pltpu.VMEM((1, LANES), F32)],
        compiler_params=pltpu.CompilerParams(
            dimension_semantics=("arbitrary",), vmem_limit_bytes=VMEM_LIMIT),
        name="post_mix_router",
    )(o, h0, wo, g, b, rw, rb)


def _moe_kernel(be_ref, src0_ref, src_ref, dst_ref, dstl_ref, h1_hbm,
                w1a_ref, b1a_ref, w2a_ref, b2a_ref, w1b_ref, b1b_ref, w2b_ref, b2b_ref,
                y_hbm, xa, xb, ya, yb, sem):
    i = pl.program_id(0)
    last = pl.num_programs(0) - 1

    def tile(ref, t):
        return ref.at[pl.ds(pl.multiple_of(t * TILE_ROWS, TILE_ROWS), TILE_ROWS), :]

    def row_in(tok, xbuf, r, s):
        return pltpu.make_async_copy(tile(h1_hbm, tok), tile(xbuf, r), sem.at[s])

    def row_out(ybuf, r, row, s):
        return pltpu.make_async_copy(tile(ybuf, r), tile(y_hbm, row), sem.at[s])

    def gather(tbl, off, xbuf, s):
        def body(it, c):
            for u in range(DMA_UNROLL):
                r = it * DMA_UNROLL + u
                row_in(tbl[0, 0, off + r], xbuf, r, s).start(priority=u % 2)
            return c
        lax.fori_loop(0, MOE_BLOCK // DMA_UNROLL, body, 0)

    def scatter(tbl, off, ybuf, s):
        def body(it, c):
            for u in range(DMA_UNROLL):
                r = it * DMA_UNROLL + u
                row_out(ybuf, r, tbl[0, 0, off + r], s).start(priority=u % 2)
            return c
        lax.fori_loop(0, MOE_BLOCK // DMA_UNROLL, body, 0)

    def wait_in(xbuf, s):
        for _ in range(MOE_BLOCK):
            row_in(0, xbuf, 0, s).wait()

    def wait_out(ybuf, s):
        for _ in range(MOE_BLOCK):
            row_out(ybuf, 0, 0, s).wait()

    def ffn(xbuf, ybuf, w1, b1, w2, b2):
        hh = _dot(_mx(_load_token_tiles(xbuf, MOE_BLOCK)), w1[0]) + b1[0]
        x_glu = jnp.minimum(hh[:, :D_FF], SWIGLU_LIMIT)
        x_lin = jnp.clip(hh[:, D_FF:], -SWIGLU_LIMIT, SWIGLU_LIMIT)
        act = x_glu * (1.0 / (1.0 + jnp.exp(-SWIGLU_ALPHA * x_glu))) * (x_lin + 1.0)
        _store_token_tiles(ybuf, _dot(_mx(act), w2[0]) + b2[0])

    @pl.when(i == 0)
    def _():
        yb[...] = jnp.zeros_like(yb)
        gather(src0_ref, 0, xa, 0)

    wait_in(xa, 0)

    @pl.when(i > 0)
    def _():
        wait_out(ya, 2)

    gather(src_ref, 0, xb, 1)
    scatter(dst_ref, 0, yb, 3)
    ffn(xa, ya, w1a_ref, b1a_ref, w2a_ref, b2a_ref)

    wait_in(xb, 1)
    wait_out(yb, 3)
    gather(src_ref, MOE_BLOCK, xa, 0)
    scatter(dst_ref, MOE_BLOCK, ya, 2)
    ffn(xb, yb, w1b_ref, b1b_ref, w2b_ref, b2b_ref)

    @pl.when(i == last)
    def _():
        wait_in(xa, 0)
        wait_out(ya, 2)
        scatter(dstl_ref, 0, yb, 3)
        wait_out(yb, 3)


def _moe_call(block_expert, src0, src_step, dst_step, dst_last, h1, w1, b1, w2, b2, n_out_rows):
    n_steps = src_step.shape[0]
    smem = functools.partial(pl.BlockSpec, memory_space=pltpu.SMEM)
    first = lambda i, be: (0, 0, 0)
    step = lambda i, be: (i, 0, 0)
    ex_a = lambda i, be: (be[2 * i], 0, 0)
    ex_b = lambda i, be: (be[2 * i + 1], 0, 0)

    def expert_specs(ex):
        return [pl.BlockSpec((1, D_MODEL, 2 * D_FF), ex), pl.BlockSpec((1, 1, 2 * D_FF), ex),
                pl.BlockSpec((1, D_FF, D_MODEL), ex), pl.BlockSpec((1, 1, D_MODEL), ex)]

    grid_spec = pltpu.PrefetchScalarGridSpec(
        num_scalar_prefetch=1,
        grid=(n_steps,),
        in_specs=[
            smem((1, 1, MOE_BLOCK), first),
            smem((1, 1, 2 * MOE_BLOCK), step),
            smem((1, 1, 2 * MOE_BLOCK), step),
            smem((1, 1, MOE_BLOCK), first),
            pl.BlockSpec(memory_space=pl.ANY),
        ] + expert_specs(ex_a) + expert_specs(ex_b),
        out_specs=pl.BlockSpec(memory_space=pl.ANY),
        scratch_shapes=[pltpu.VMEM((MOE_BLOCK * TILE_ROWS, LANES), F32)] * 4
        + [pltpu.SemaphoreType.DMA((4,))],
    )
    return pl.pallas_call(
        _moe_kernel,
        grid_spec=grid_spec,
        out_shape=jax.ShapeDtypeStruct((n_out_rows * TILE_ROWS, LANES), F32),
        compiler_params=pltpu.CompilerParams(
            dimension_semantics=("arbitrary",), vmem_limit_bytes=VMEM_LIMIT),
        name="moe_experts",
    )(block_expert, src0, src_step, dst_step, dst_last, h1, w1, b1, w2, b2, w1, b1, w2, b2)


def _combine_kernel(y0_ref, y1_ref, y2_ref, y3_ref, gate_ref, h1_ref, g_ref, b_ref, out_ref):
    gates = gate_ref[...]
    acc = DEEPNORM_ALPHA * h1_ref[...]
    for k, y_ref in enumerate((y0_ref, y1_ref, y2_ref, y3_ref)):
        acc = acc + gates[:, k:k + 1] * _load_token_tiles(y_ref, acc.shape[0])
    out_ref[...] = _layer_norm(acc, g_ref[...], b_ref[...])


def _combine_call(y, gates, h1, g, b, tm):
    t = h1.shape[0]
    nt = t // tm
    const = lambda i: (0, 0)
    y_specs = [pl.BlockSpec((tm * TILE_ROWS, LANES), functools.partial(lambda i, k: (k * nt + i, 0), k=k))
               for k in range(TOP_K)]
    return pl.pallas_call(
        _combine_kernel,
        grid=(nt,),
        in_specs=y_specs + [
            pl.BlockSpec((tm, LANES), lambda i: (i, 0)),
            pl.BlockSpec((tm, D_MODEL), lambda i: (i, 0)),
            pl.BlockSpec((1, D_MODEL), const),
            pl.BlockSpec((1, D_MODEL), const),
        ],
        out_specs=pl.BlockSpec((tm, D_MODEL), lambda i: (i, 0)),
        out_shape=jax.ShapeDtypeStruct((t, D_MODEL), F32),
        compiler_params=pltpu.CompilerParams(
            dimension_semantics=("arbitrary",), vmem_limit_bytes=VMEM_LIMIT),
        name="combine_ln2",
    )(y, y, y, y, gates, h1, g, b)


def _relayout_w_in(w):
    sizes = (256, 256, 512, 512, GATE_RANK, 256, 256, 512, 512)
    offs = np.concatenate([[0], np.cumsum(sizes)])
    gq, gk, gv, gg, glr, rq, rk, rv, rg = [w[:, offs[i]:offs[i + 1]] for i in range(9)]
    perm = np.zeros((256,), np.int32)
    for p in range(2):
        for l in range(LANES):
            part, hh, f = l // 64, (l % 64) // 32, l % 32
            perm[p * LANES + l] = (2 * p + hh) * DK + part * (DK // 2) + f
    glr = jnp.pad(glr, ((0, 0), (0, LANES - GATE_RANK)))
    return jnp.concatenate([gq, gk, gv, gg, glr, rq[:, perm], rk[:, perm], rv, rg], axis=1)


def kernel(x, positions, ln_in_g, ln_in_b, w_in, gla_gate_w, gla_gate_b, gla_norm_g, ret_norm_g,
           ret_norm_b, w_out, ln1_g, ln1_b, router_w, router_b, moe_w1, moe_b1, moe_w2, moe_b2,
           ln2_g, ln2_b):
    bsz, seq, d = x.shape
    assert d == D_MODEL and seq % GROUP == 0 and w_in.shape[0] == DEPTH == 1
    t = bsz * seq
    ts = min(512, seq)
    tm = min(512, t)
    assert seq % ts == 0 and t % tm == 0
    row = lambda v: v.reshape(1, -1).astype(F32)

    w_p = _relayout_w_in(w_in[0]).astype(MXU_DTYPE)
    h0, proj = _ln_proj_call(x.reshape(t, d), row(ln_in_g), row(ln_in_b), w_p, tm)

    gw = jnp.pad(gla_gate_w[0], ((0, LANES - GATE_RANK), (0, 0))).astype(MXU_DTYPE)
    half = DK // 2
    inv_freq = 1.0 / (ROPE_BASE ** np.linspace(0.0, 1.0, half, dtype=np.float32))
    invf = jnp.asarray(np.tile(inv_freq, LANES // half).reshape(1, LANES), F32)
    sgn = jnp.asarray(np.where(np.arange(LANES) < LANES // 2, -1.0, 1.0).reshape(1, LANES), F32)
    step_ang = np.arange(GROUP, dtype=np.float64)[:, None] * np.tile(inv_freq, LANES // half)[None, :]
    cd = jnp.asarray(np.cos(step_ang), F32)
    sd = jnp.asarray(np.sin(step_ang), F32)
    o = _mixer_call(proj, positions.reshape(t, 1), gw, row(gla_gate_b[0]), row(gla_norm_g[0]),
                    row(ret_norm_g[0]), row(ret_norm_b[0]), invf, sgn, cd, sd, bsz, seq, ts)

    rw = jnp.pad(router_w[0], ((0, 0), (0, LANES - N_EXPERTS))).astype(MXU_DTYPE)
    rb = jnp.pad(router_b[0], (0, LANES - N_EXPERTS)).reshape(1, LANES).astype(F32)
    h1, h1t, r_i, r_g, cnt = _post_mix_call(o, h0, w_out[0].astype(MXU_DTYPE), row(ln1_g[0]),
                                       row(ln1_b[0]), rw, rb, tm)

    e = r_i[:, :TOP_K]
    rank = r_i[:, TOP_K:2 * TOP_K]
    counts = cnt[0, :N_EXPERTS].astype(jnp.int32)
    padded = ((counts + MOE_BLOCK - 1) // MOE_BLOCK) * MOE_BLOCK
    pad_end = jnp.cumsum(padded)
    pad_start = pad_end - padded
    tk = t * TOP_K
    n_blocks = (tk + MOE_BLOCK - 1) // MOE_BLOCK + N_EXPERTS
    n_slots = n_blocks * MOE_BLOCK
    assert n_blocks % 2 == 0
    i32 = jnp.int32
    ex = jnp.arange(N_EXPERTS, dtype=i32)
    dest = (jnp.sum(jnp.where(e[..., None] == ex, pad_start, 0), -1) + rank).reshape(-1)
    gap = padded - counts
    gap_end = jnp.cumsum(gap)
    j = jnp.arange(n_slots - tk, dtype=i32)
    je = jnp.sum(j[:, None] >= gap_end[None, :], -1)
    base = jnp.concatenate([pad_start + counts - (gap_end - gap), pad_end[-1:] - gap_end[-1:]])
    pad_slot = j + jnp.sum(jnp.where(je[:, None] == jnp.arange(N_EXPERTS + 1, dtype=i32), base, 0), -1)
    keys = jnp.concatenate([dest, pad_slot])
    vals = jnp.concatenate([jnp.arange(tk, dtype=i32), jnp.full((n_slots - tk,), -1, i32)])
    _, slot_flat = lax.sort((keys, vals), num_keys=1)
    slot_tok = slot_flat >> 2
    slot_src = jnp.where(slot_flat >= 0, slot_tok, 0)
    dump = tk + jnp.arange(MOE_BLOCK, dtype=i32)
    slot_dst = jnp.where(slot_flat >= 0, (slot_flat & 3) * t + slot_tok,
                         jnp.tile(dump, n_blocks))
    block_start = jnp.arange(n_blocks, dtype=i32) * MOE_BLOCK
    block_expert = jnp.minimum(jnp.sum(block_start[:, None] >= pad_end[None, :], -1),
                               N_EXPERTS - 1).astype(i32)
    n_steps = n_blocks // 2
    src_step = jnp.concatenate([slot_src[MOE_BLOCK:], jnp.zeros((MOE_BLOCK,), i32)])
    dst_step = jnp.concatenate([dump, slot_dst[:-MOE_BLOCK]])

    y = _moe_call(block_expert, slot_src[:MOE_BLOCK].reshape(1, 1, MOE_BLOCK),
                  src_step.reshape(n_steps, 1, 2 * MOE_BLOCK), dst_step.reshape(n_steps, 1, 2 * MOE_BLOCK),
                  slot_dst[-MOE_BLOCK:].reshape(1, 1, MOE_BLOCK), h1t,
                  moe_w1[0].astype(MXU_DTYPE), moe_b1[0].reshape(N_EXPERTS, 1, 2 * D_FF),
                  moe_w2[0].astype(MXU_DTYPE), moe_b2[0].reshape(N_EXPERTS, 1, D_MODEL), tk + MOE_BLOCK)

    out = _combine_call(y, r_g, h1, row(ln2_g[0]), row(ln2_b[0]), min(256, t))
    return out.reshape(bsz, seq, d)
```

```python
import functools
import math

import numpy as np
import jax
import jax.numpy as jnp
from jax import lax
from jax.experimental import pallas as pl
from jax.experimental.pallas import tpu as pltpu

F32 = jnp.float32
MXU_DTYPE = jnp.bfloat16

D_MODEL = 1024
CHUNK = 64
GC = 4
GROUP = GC * CHUNK
HEADS = 4
DK = 64
DV = 128
GATE_RANK = 16
GATE_NORM = 16.0
ROPE_BASE = 10000.0
N_EXPERTS = 32
TOP_K = 4
D_FF = 1024
SWIGLU_ALPHA = 1.702
SWIGLU_LIMIT = 7.0
MOE_BLOCK = 256
DMA_UNROLL = 8
LN_EPS = 1e-5
DEPTH = 1
DEEPNORM_ALPHA = (2.0 * DEPTH) ** 0.25

LANES = 128
VMEM_LIMIT = 56 * 1024 * 1024

C_GQ, C_GK, C_GV, C_GG, C_GLR = 0, 256, 512, 1024, 1536
C_RQ, C_RK, C_RV, C_RG = 1664, 1920, 2176, 2688
PROJ_W = 3200

LOG_GAMMA = [math.log1p(-(2.0 ** (-5.0 - h))) for h in range(HEADS)]


def _dot(a, b):
    return jnp.dot(a, b, preferred_element_type=F32)


def _dot_nt(a, b):
    return lax.dot_general(a, b, (((1,), (1,)), ((), ())), preferred_element_type=F32)


def _dot_tn(a, b):
    return lax.dot_general(a, b, (((0,), (0,)), ((), ())), preferred_element_type=F32)


def _mx(a):
    return a.astype(MXU_DTYPE)


def _layer_norm(x, g, b):
    mu = jnp.mean(x, -1, keepdims=True)
    xc = x - mu
    var = jnp.mean(xc * xc, -1, keepdims=True)
    return xc * lax.rsqrt(var + LN_EPS) * g + b


def _silu(x):
    return x * (1.0 / (1.0 + jnp.exp(-x)))


F32_ROWS = D_MODEL // LANES
PK_ROWS = D_MODEL // (2 * LANES)


def _load_token_tiles(ref, n, r):
    return jnp.concatenate([ref[pl.ds(j, n, stride=r), :] for j in range(r)], axis=1)


def _store_token_tiles(ref, rows):
    n, r = rows.shape[0], rows.shape[1] // LANES
    for j in range(r):
        ref[pl.ds(j, n, stride=r), :] = rows[:, j * LANES:(j + 1) * LANES]


def _pack_pairs(x):
    h = x.shape[1] // 2
    r = x.astype(jnp.bfloat16).astype(F32)
    lo = lax.bitcast_convert_type(r[:, :h], jnp.uint32) >> 16
    hi = lax.bitcast_convert_type(r[:, h:], jnp.uint32) & jnp.uint32(0xFFFF0000)
    return lo | hi


def _unpack_pairs(u):
    lo = lax.bitcast_convert_type(u << 16, F32)
    hi = lax.bitcast_convert_type(u & jnp.uint32(0xFFFF0000), F32)
    return jnp.concatenate([lo, hi], axis=1)


def _ln_proj_kernel(x_ref, g_ref, b_ref, w_ref, h_ref, p_ref):
    h = _layer_norm(x_ref[...], g_ref[...], b_ref[...])
    h_ref[...] = h
    p_ref[...] = _dot(_mx(h), w_ref[...])


def _ln_proj_call(x2, g, b, w, tm):
    t = x2.shape[0]
    return pl.pallas_call(
        _ln_proj_kernel,
        grid=(t // tm,),
        in_specs=[
            pl.BlockSpec((tm, D_MODEL), lambda i: (i, 0)),
            pl.BlockSpec((1, D_MODEL), lambda i: (0, 0)),
            pl.BlockSpec((1, D_MODEL), lambda i: (0, 0)),
            pl.BlockSpec((D_MODEL, PROJ_W), lambda i: (0, 0)),
        ],
        out_specs=[
            pl.BlockSpec((tm, D_MODEL), lambda i: (i, 0)),
            pl.BlockSpec((tm, PROJ_W), lambda i: (i, 0)),
        ],
        out_shape=[
            jax.ShapeDtypeStruct((t, D_MODEL), F32),
            jax.ShapeDtypeStruct((t, PROJ_W), F32),
        ],
        compiler_params=pltpu.CompilerParams(
            dimension_semantics=("arbitrary",), vmem_limit_bytes=VMEM_LIMIT),
        name="ln_proj",
    )(x2, g, b, w)


def _mixer_kernel(p_ref, pos_ref, gw_ref, gb_ref, gn_ref, rng_ref, rnb_ref, invf_ref, sgn_ref,
                  cd_ref, sd_ref, o_ref, st_ref, *, n_groups):
    @pl.when(pl.program_id(1) == 0)
    def _():
        st_ref[...] = jnp.zeros_like(st_ref)

    rr = lax.broadcasted_iota(jnp.int32, (GROUP, GROUP), 0)
    cc = lax.broadcasted_iota(jnp.int32, (GROUP, GROUP), 1)
    same = (rr >> 6) == (cc >> 6)
    lower = same & (rr >= cc)
    upper = same & (rr < cc)
    tri = lower.astype(MXU_DTYPE)
    dist = jnp.abs(rr - cc).astype(F32)
    rowi = lax.broadcasted_iota(jnp.int32, (GROUP, LANES), 0)
    rin = (rowi & (CHUNK - 1)).astype(F32)
    in_chunk = [(rowi >> 6) == c for c in range(GC)]
    lane = lax.broadcasted_iota(jnp.int32, (GROUP, LANES), 1)
    gla_half = [((lane >> 6) & 1) == i for i in range(2)]
    ret_half = [((lane >> 5) & 1) == i for i in range(2)]
    ret_d = [jnp.where(same, jnp.exp(LOG_GAMMA[h] * dist), 0.0) for h in range(HEADS)]
    ret_eb = [jnp.exp(LOG_GAMMA[h] * (rin + 1.0)) for h in range(HEADS)]
    ret_ek = [jnp.exp(LOG_GAMMA[h] * (CHUNK - 1.0 - rin)) for h in range(HEADS)]
    ret_dec = [math.exp(LOG_GAMMA[h] * CHUNK) for h in range(HEADS)]

    def block_diag(x):
        return _mx(jnp.concatenate([jnp.where(in_chunk[c], x, 0.0) for c in range(GC)], axis=1))

    def recur(hs, q_m, kv_t, decay):
        st = st_ref[hs]
        parts = []
        for c in range(GC):
            parts.append(_dot_nt(q_m[c * CHUNK:(c + 1) * CHUNK], _mx(st)))
            st = st * decay(c) + kv_t[:, c * LANES:(c + 1) * LANES]
        st_ref[hs] = st
        return jnp.concatenate(parts, axis=0)

    def group_body(g, carry):
        rows = pl.ds(pl.multiple_of(g * GROUP, GROUP), GROUP)

        z = _dot(_mx(p_ref[rows, C_GLR:C_GLR + LANES]), gw_ref[...]) + gb_ref[...]
        la = (jnp.minimum(z, 0.0) - jnp.log1p(jnp.exp(-jnp.abs(z)))) * (1.0 / GATE_NORM)
        la_hi = _mx(la)
        r1 = la - la_hi.astype(F32)
        la_mid = _mx(r1)
        la_lo = _mx(r1 - la_mid.astype(F32))
        b = _dot(tri, la_hi) + _dot(tri, la_mid) + _dot(tri, la_lo)
        b_ends = [b[(c + 1) * CHUNK - 1:(c + 1) * CHUNK, :] for c in range(GC)]
        b_last = jnp.concatenate([jnp.broadcast_to(e, (CHUNK, 256)) for e in b_ends], axis=0)
        decs = [jnp.exp(e) for e in b_ends]
        eb = jnp.exp(b)
        enb = jnp.exp(-b)
        ekv = jnp.exp(b_last - b)
        q = p_ref[rows, C_GQ:C_GQ + 256] * (DK ** -0.5)
        k = p_ref[rows, C_GK:C_GK + 256]
        qe, qn = q * eb, q * enb
        ke, kn, kk = k * eb, k * enb, k * ekv
        for h in range(HEADS):
            p, half = h // 2, h % 2
            ls = slice(p * LANES, (p + 1) * LANES)
            m = gla_half[half]
            qe_m = _mx(jnp.where(m, qe[:, ls], 0.0))
            qn_m = _mx(jnp.where(m, qn[:, ls], 0.0))
            s_lo = _dot_nt(qe_m, _mx(kn[:, ls]))
            s_up = _dot_nt(qn_m, _mx(ke[:, ls]))
            sc = jnp.where(lower, s_lo, jnp.where(upper, s_up, 0.0))
            v = _mx(p_ref[rows, C_GV + h * DV:C_GV + (h + 1) * DV])
            kv_t = _dot_tn(v, block_diag(kk[:, ls]))
            o = _dot(_mx(sc), v) + recur(h, qe_m, kv_t, lambda c: decs[c][:, ls])
            o = o * lax.rsqrt(jnp.mean(o * o, -1, keepdims=True) + LN_EPS) * gn_ref[...]
            o = o * _silu(p_ref[rows, C_GG + h * DV:C_GG + (h + 1) * DV])
            o_ref[rows, h * DV:(h + 1) * DV] = o.astype(o_ref.dtype)

        base = pos_ref[pl.ds(pl.multiple_of(g * GROUP, GROUP), 1), :].astype(F32) * invf_ref[...]
        c0, s0 = jnp.cos(base), jnp.sin(base)
        cs = c0 * cd_ref[...] - s0 * sd_ref[...]
        sn = (s0 * sgn_ref[...]) * cd_ref[...] + (c0 * sgn_ref[...]) * sd_ref[...]
        for p in range(2):
            tq = p_ref[rows, C_RQ + p * LANES:C_RQ + (p + 1) * LANES]
            tk = p_ref[rows, C_RK + p * LANES:C_RK + (p + 1) * LANES] * (DK ** -0.5)
            rq = tq * cs + pltpu.roll(tq, LANES // 2, 1) * sn
            rk = tk * cs + pltpu.roll(tk, LANES // 2, 1) * sn
            rk_m = _mx(rk)
            for half in range(2):
                h = 2 * p + half
                q_m = _mx(jnp.where(ret_half[half], rq, 0.0))
                s = _dot_nt(q_m, rk_m) * ret_d[h]
                v = _mx(p_ref[rows, C_RV + h * DV:C_RV + (h + 1) * DV])
                kv_t = _dot_tn(v, block_diag(rk * ret_ek[h]))
                o = _dot(_mx(s), v) + ret_eb[h] * recur(HEADS + h, q_m, kv_t, lambda c: ret_dec[h])
                mu = jnp.mean(o, -1, keepdims=True)
                oc = o - mu
                var = jnp.mean(oc * oc, -1, keepdims=True)
                o = oc * lax.rsqrt(var + LN_EPS) * rng_ref[:, h * DV:(h + 1) * DV] \
                    + rnb_ref[:, h * DV:(h + 1) * DV]
                o = o * _silu(p_ref[rows, C_RG + h * DV:C_RG + (h + 1) * DV])
                o_ref[rows, (HEADS + h) * DV:(HEADS + h + 1) * DV] = o.astype(o_ref.dtype)
        return carry

    lax.fori_loop(0, n_groups, group_body, 0)


def _mixer_call(proj, pos, gw, gb, gn, rng, rnb, invf, sgn, cd, sd, bsz, seq, ts):
    t = bsz * seq
    nst = seq // ts
    const = lambda b, s: (0, 0)
    return pl.pallas_call(
        functools.partial(_mixer_kernel, n_groups=ts // GROUP),
        grid=(bsz, nst),
        in_specs=[
            pl.BlockSpec((ts, PROJ_W), lambda b, s: (b * nst + s, 0)),
            pl.BlockSpec((ts, 1), lambda b, s: (b * nst + s, 0)),
            pl.BlockSpec((LANES, 256), const),
            pl.BlockSpec((1, 256), const),
            pl.BlockSpec((1, DV), const),
            pl.BlockSpec((1, HEADS * DV), const),
            pl.BlockSpec((1, HEADS * DV), const),
            pl.BlockSpec((1, LANES), const),
            pl.BlockSpec((1, LANES), const),
            pl.BlockSpec((GROUP, LANES), const),
            pl.BlockSpec((GROUP, LANES), const),
        ],
        out_specs=pl.BlockSpec((ts, D_MODEL), lambda b, s: (b * nst + s, 0)),
        out_shape=jax.ShapeDtypeStruct((t, D_MODEL), MXU_DTYPE),
        scratch_shapes=[pltpu.VMEM((2 * HEADS, DV, LANES), F32)],
        compiler_params=pltpu.CompilerParams(
            dimension_semantics=("arbitrary", "arbitrary"), vmem_limit_bytes=VMEM_LIMIT),
        name="mixers",
    )(proj, pos, gw, gb, gn, rng, rnb, invf, sgn, cd, sd)


def _post_mix_kernel(o_ref, h0_ref, wo_ref, g_ref, b_ref, rw_ref, rb_ref,
                     h1_ref, h1t_ref, ri_ref, rg_ref, cnt_ref, tril_ref, carry_ref, *, tm):
    @pl.when(pl.program_id(0) == 0)
    def _():
        r = lax.broadcasted_iota(jnp.int32, (tm, tm), 0)
        c = lax.broadcasted_iota(jnp.int32, (tm, tm), 1)
        tril_ref[...] = (r > c).astype(MXU_DTYPE)
        carry_ref[...] = jnp.zeros_like(carry_ref)

    mix = _dot(o_ref[...], wo_ref[...])
    h1 = _layer_norm(DEEPNORM_ALPHA * h0_ref[...] + mix, g_ref[...], b_ref[...])
    h1_ref[...] = h1
    _store_token_tiles(h1t_ref, _pack_pairs(h1))

    lane = lax.broadcasted_iota(jnp.int32, (tm, LANES), 1)
    lane_f = lane.astype(F32)
    logits = _dot(_mx(h1), rw_ref[...]) + rb_ref[...]
    l = jnp.where(lane < N_EXPERTS, logits, -jnp.inf)
    vals, idxs = [], []
    for _ in range(TOP_K):
        m = jnp.max(l, -1, keepdims=True)
        i = jnp.min(jnp.where(l == m, lane_f, float(LANES)), -1, keepdims=True)
        vals.append(m)
        idxs.append(i)
        l = jnp.where(lane_f == i, -jnp.inf, l)
    exps = [jnp.exp(v - vals[0]) for v in vals]
    inv = 1.0 / (exps[0] + exps[1] + exps[2] + exps[3])

    onehot = jnp.zeros((tm, LANES), F32)
    for i in idxs:
        onehot = onehot + (lane_f == i).astype(F32)
    before = _dot(tril_ref[...], _mx(onehot)) + carry_ref[...]
    packed_i = jnp.zeros((tm, LANES), F32)
    packed_g = jnp.zeros((tm, LANES), F32)
    for k in range(TOP_K):
        rank = jnp.sum(jnp.where(lane_f == idxs[k], before, 0.0), -1, keepdims=True)
        packed_i = jnp.where(lane == k, idxs[k], packed_i)
        packed_i = jnp.where(lane == TOP_K + k, rank, packed_i)
        packed_g = jnp.where(lane == k, exps[k] * inv, packed_g)
    ri_ref[...] = packed_i.astype(jnp.int32)
    rg_ref[...] = packed_g
    carry_ref[...] = carry_ref[...] + jnp.sum(onehot, 0, keepdims=True)
    cnt_ref[...] = carry_ref[...]


def _post_mix_call(o, h0, wo, g, b, rw, rb, tm):
    t = o.shape[0]
    const = lambda i: (0, 0)
    return pl.pallas_call(
        functools.partial(_post_mix_kernel, tm=tm),
        grid=(t // tm,),
        in_specs=[
            pl.BlockSpec((tm, D_MODEL), lambda i: (i, 0)),
            pl.BlockSpec((tm, D_MODEL), lambda i: (i, 0)),
            pl.BlockSpec((D_MODEL, D_MODEL), const),
            pl.BlockSpec((1, D_MODEL), const),
            pl.BlockSpec((1, D_MODEL), const),
            pl.BlockSpec((D_MODEL, LANES), const),
            pl.BlockSpec((1, LANES), const),
        ],
        out_specs=[
            pl.BlockSpec((tm, D_MODEL), lambda i: (i, 0)),
            pl.BlockSpec((tm * PK_ROWS, LANES), lambda i: (i, 0)),
            pl.BlockSpec((tm, LANES), lambda i: (i, 0)),
            pl.BlockSpec((tm, LANES), lambda i: (i, 0)),
            pl.BlockSpec((1, LANES), const),
        ],
        out_shape=[
            jax.ShapeDtypeStruct((t, D_MODEL), F32),
            jax.ShapeDtypeStruct((t * PK_ROWS, LANES), jnp.uint32),
            jax.ShapeDtypeStruct((t, LANES), jnp.int32),
            jax.ShapeDtypeStruct((t, LANES), F32),
            jax.ShapeDtypeStruct((1, LANES), F32),
        ],
        scratch_shapes=[pltpu.VMEM((tm, tm), MXU_DTYPE), pltpu.VMEM((1, LANES), F32)],
        compiler_params=pltpu.CompilerParams(
            dimension_semantics=("arbitrary",), vmem_limit_bytes=VMEM_LIMIT),
        name="post_mix_router",
    )(o, h0, wo, g, b, rw, rb)


def _dispatch_kernel(pad_row_ref, pad_n_ref, dest_ref, h1p_ref, xs_hbm, zero_ref, sem, *, tm):
    i = pl.program_id(0)

    def tile(ref, t):
        return ref.at[pl.ds(pl.multiple_of(t * PK_ROWS, PK_ROWS), PK_ROWS), :]

    def body(it, c):
        for u in range(DMA_UNROLL):
            r = it * DMA_UNROLL + u
            for k in range(TOP_K):
                pltpu.make_async_copy(tile(h1p_ref, r), tile(xs_hbm, dest_ref[0, 0, r * TOP_K + k]),
                                      sem.at[0]).start(priority=k % 2)
        return c
    lax.fori_loop(0, tm // DMA_UNROLL, body, 0)
    for _ in range(TOP_K):
        pltpu.make_async_copy(h1p_ref, xs_hbm.at[pl.ds(0, tm * PK_ROWS), :], sem.at[0]).wait()

    @pl.when(i == pl.num_programs(0) - 1)
    def _():
        zero_ref[...] = jnp.zeros_like(zero_ref)
        zero = zero_ref.at[pl.ds(0, PK_ROWS), :]

        def per_expert(e, c):
            def start(r, c2):
                pltpu.make_async_copy(zero, tile(xs_hbm, pad_row_ref[e] + r), sem.at[1]).start()
                return c2
            lax.fori_loop(0, pad_n_ref[e], start, 0)

            def wait(r, c2):
                pltpu.make_async_copy(zero, tile(xs_hbm, 0), sem.at[1]).wait()
                return c2
            lax.fori_loop(0, pad_n_ref[e], wait, 0)
            return c
        lax.fori_loop(0, pad_row_ref.shape[0], per_expert, 0)


def _dispatch_call(pad_row, pad_n, dest, h1p, n_slots, tm):
    t = h1p.shape[0] // PK_ROWS
    grid_spec = pltpu.PrefetchScalarGridSpec(
        num_scalar_prefetch=2,
        grid=(t // tm,),
        in_specs=[
            pl.BlockSpec((1, 1, tm * TOP_K), lambda i, pr, pn: (i, 0, 0), memory_space=pltpu.SMEM),
            pl.BlockSpec((tm * PK_ROWS, LANES), lambda i, pr, pn: (i, 0)),
        ],
        out_specs=pl.BlockSpec(memory_space=pl.ANY),
        scratch_shapes=[pltpu.VMEM((F32_ROWS, LANES), jnp.uint32), pltpu.SemaphoreType.DMA((2,))],
    )
    return pl.pallas_call(
        functools.partial(_dispatch_kernel, tm=tm),
        grid_spec=grid_spec,
        out_shape=jax.ShapeDtypeStruct((n_slots * PK_ROWS, LANES), jnp.uint32),
        compiler_params=pltpu.CompilerParams(
            dimension_semantics=("arbitrary",), vmem_limit_bytes=VMEM_LIMIT),
        name="dispatch",
    )(pad_row, pad_n, dest.reshape(t // tm, 1, tm * TOP_K), h1p)


def _expert_kernel(be_ref, nr_ref, x_ref, w1_ref, b1_ref, w2_ref, b2_ref, y_ref):
    @pl.when(pl.program_id(0) >= nr_ref[0])
    def _():
        y_ref[...] = jnp.zeros_like(y_ref)

    @pl.when(pl.program_id(0) < nr_ref[0])
    def _():
        x = _mx(_unpack_pairs(_load_token_tiles(x_ref, MOE_BLOCK, PK_ROWS)))
        hh = _dot(x, w1_ref[0]) + b1_ref[0]
        x_glu = jnp.minimum(hh[:, :D_FF], SWIGLU_LIMIT)
        x_lin = jnp.clip(hh[:, D_FF:], -SWIGLU_LIMIT, SWIGLU_LIMIT)
        act = x_glu * (1.0 / (1.0 + jnp.exp(-SWIGLU_ALPHA * x_glu))) * (x_lin + 1.0)
        _store_token_tiles(y_ref, _dot(_mx(act), w2_ref[0]) + b2_ref[0])


def _expert_call(block_expert, n_real, xs, w1, b1, w2, b2):
    n_blocks = block_expert.shape[0]
    blk = lambda g, be, nr: (jnp.minimum(g, nr[0] - 1), 0)
    ex = lambda g, be, nr: (be[g], 0, 0)
    grid_spec = pltpu.PrefetchScalarGridSpec(
        num_scalar_prefetch=2,
        grid=(n_blocks,),
        in_specs=[
            pl.BlockSpec((MOE_BLOCK * PK_ROWS, LANES), blk),
            pl.BlockSpec((1, D_MODEL, 2 * D_FF), ex),
            pl.BlockSpec((1, 1, 2 * D_FF), ex),
            pl.BlockSpec((1, D_FF, D_MODEL), ex),
            pl.BlockSpec((1, 1, D_MODEL), ex),
        ],
        out_specs=pl.BlockSpec((MOE_BLOCK * F32_ROWS, LANES), lambda g, be, nr: (g, 0)),
    )
    return pl.pallas_call(
        _expert_kernel,
        grid_spec=grid_spec,
        out_shape=jax.ShapeDtypeStruct((n_blocks * MOE_BLOCK * F32_ROWS, LANES), F32),
        compiler_params=pltpu.CompilerParams(
            dimension_semantics=("arbitrary",), vmem_limit_bytes=VMEM_LIMIT),
        name="moe_experts",
    )(block_expert, n_real, xs, w1, b1, w2, b2)


def _gather_combine_kernel(dest0_ref, destn_ref, gate_ref, h1_ref, g_ref, b_ref, ys_hbm,
                           out_ref, ybuf, sem, *, tm):
    i = pl.program_id(0)
    slot = i % 2
    rows = tm * F32_ROWS

    def tile(ref, t):
        return ref.at[pl.ds(pl.multiple_of(t * F32_ROWS, F32_ROWS), F32_ROWS), :]

    def gather(tbl, s):
        def body(it, c):
            for u in range(DMA_UNROLL):
                r = it * DMA_UNROLL + u
                for k in range(TOP_K):
                    pltpu.make_async_copy(tile(ys_hbm, tbl[0, 0, r * TOP_K + k]),
                                          tile(ybuf, (s * TOP_K + k) * tm + r), sem.at[s]).start(priority=k % 2)
            return c
        lax.fori_loop(0, tm // DMA_UNROLL, body, 0)

    @pl.when(i == 0)
    def _():
        gather(dest0_ref, 0)

    @pl.when(i + 1 < pl.num_programs(0))
    def _():
        gather(destn_ref, 1 - slot)

    for k in range(TOP_K):
        plane = ybuf.at[pl.ds(pl.multiple_of((slot * TOP_K + k) * rows, rows), rows), :]
        pltpu.make_async_copy(ys_hbm.at[pl.ds(0, rows), :], plane, sem.at[slot]).wait()

    gates = gate_ref[...]
    acc = DEEPNORM_ALPHA * h1_ref[...]
    for k in range(TOP_K):
        plane = ybuf.at[pl.ds(pl.multiple_of((slot * TOP_K + k) * rows, rows), rows), :]
        acc = acc + gates[:, k:k + 1] * _load_token_tiles(plane, tm, F32_ROWS)
    out_ref[...] = _layer_norm(acc, g_ref[...], b_ref[...])


def _gather_combine_call(dest, gates, h1, g, b, ys, tm):
    t = h1.shape[0]
    nt = t // tm
    const = lambda i: (0, 0)
    smem = functools.partial(pl.BlockSpec, memory_space=pltpu.SMEM)
    dest = dest.reshape(nt, 1, tm * TOP_K)
    dest_next = jnp.concatenate([dest[1:], dest[:1]], axis=0)
    return pl.pallas_call(
        functools.partial(_gather_combine_kernel, tm=tm),
        grid=(nt,),
        in_specs=[
            smem((1, 1, tm * TOP_K), lambda i: (0, 0, 0)),
            smem((1, 1, tm * TOP_K), lambda i: (i, 0, 0)),
            pl.BlockSpec((tm, LANES), lambda i: (i, 0)),
            pl.BlockSpec((tm, D_MODEL), lambda i: (i, 0)),
            pl.BlockSpec((1, D_MODEL), const),
            pl.BlockSpec((1, D_MODEL), const),
            pl.BlockSpec(memory_space=pl.ANY),
        ],
        out_specs=pl.BlockSpec((tm, D_MODEL), lambda i: (i, 0)),
        out_shape=jax.ShapeDtypeStruct((t, D_MODEL), F32),
        scratch_shapes=[pltpu.VMEM((2 * TOP_K * tm * F32_ROWS, LANES), F32), pltpu.SemaphoreType.DMA((2,))],
        compiler_params=pltpu.CompilerParams(
            dimension_semantics=("arbitrary",), vmem_limit_bytes=VMEM_LIMIT),
        name="gather_combine_ln2",
    )(dest, dest_next, gates, h1, g, b, ys)


def _relayout_w_in(w):
    sizes = (256, 256, 512, 512, GATE_RANK, 256, 256, 512, 512)
    offs = np.concatenate([[0], np.cumsum(sizes)])
    gq, gk, gv, gg, glr, rq, rk, rv, rg = [w[:, offs[i]:offs[i + 1]] for i in range(9)]
    perm = np.zeros((256,), np.int32)
    for p in range(2):
        for l in range(LANES):
            part, hh, f = l // 64, (l % 64) // 32, l % 32
            perm[p * LANES + l] = (2 * p + hh) * DK + part * (DK // 2) + f
    glr = jnp.pad(glr, ((0, 0), (0, LANES - GATE_RANK)))
    return jnp.concatenate([gq, gk, gv, gg, glr, rq[:, perm], rk[:, perm], rv, rg], axis=1)


def kernel(x, positions, ln_in_g, ln_in_b, w_in, gla_gate_w, gla_gate_b, gla_norm_g, ret_norm_g,
           ret_norm_b, w_out, ln1_g, ln1_b, router_w, router_b, moe_w1, moe_b1, moe_w2, moe_b2,
           ln2_g, ln2_b):
    bsz, seq, d = x.shape
    assert d == D_MODEL and seq % GROUP == 0 and w_in.shape[0] == DEPTH == 1
    t = bsz * seq
    ts = min(512, seq)
    tm = min(512, t)
    assert seq % ts == 0 and t % tm == 0
    row = lambda v: v.reshape(1, -1).astype(F32)

    w_p = _relayout_w_in(w_in[0]).astype(MXU_DTYPE)
    h0, proj = _ln_proj_call(x.reshape(t, d), row(ln_in_g), row(ln_in_b), w_p, tm)

    gw = jnp.pad(gla_gate_w[0], ((0, LANES - GATE_RANK), (0, 0))).astype(MXU_DTYPE)
    half = DK // 2
    inv_freq = 1.0 / (ROPE_BASE ** np.linspace(0.0, 1.0, half, dtype=np.float32))
    invf = jnp.asarray(np.tile(inv_freq, LANES // half).reshape(1, LANES), F32)
    sgn = jnp.asarray(np.where(np.arange(LANES) < LANES // 2, -1.0, 1.0).reshape(1, LANES), F32)
    step_ang = np.arange(GROUP, dtype=np.float64)[:, None] * np.tile(inv_freq, LANES // half)[None, :]
    cd = jnp.asarray(np.cos(step_ang), F32)
    sd = jnp.asarray(np.sin(step_ang), F32)
    o = _mixer_call(proj, positions.reshape(t, 1), gw, row(gla_gate_b[0]), row(gla_norm_g[0]),
                    row(ret_norm_g[0]), row(ret_norm_b[0]), invf, sgn, cd, sd, bsz, seq, ts)

    rw = jnp.pad(router_w[0], ((0, 0), (0, LANES - N_EXPERTS))).astype(MXU_DTYPE)
    rb = jnp.pad(router_b[0], (0, LANES - N_EXPERTS)).reshape(1, LANES).astype(F32)
    h1, h1t, r_i, r_g, cnt = _post_mix_call(o, h0, w_out[0].astype(MXU_DTYPE), row(ln1_g[0]),
                                       row(ln1_b[0]), rw, rb, tm)

    e = r_i[:, :TOP_K]
    rank = r_i[:, TOP_K:2 * TOP_K]
    counts = cnt[0, :N_EXPERTS].astype(jnp.int32)
    padded = ((counts + MOE_BLOCK - 1) // MOE_BLOCK) * MOE_BLOCK
    pad_end = jnp.cumsum(padded)
    pad_start = pad_end - padded
    tk = t * TOP_K
    n_blocks = (tk + MOE_BLOCK - 1) // MOE_BLOCK + N_EXPERTS
    i32 = jnp.int32
    ex = jnp.arange(N_EXPERTS, dtype=i32)
    dest = jnp.sum(jnp.where(e[..., None] == ex, pad_start, 0), -1) + rank
    block_start = jnp.arange(n_blocks, dtype=i32) * MOE_BLOCK
    block_expert = jnp.minimum(jnp.sum(block_start[:, None] >= pad_end[None, :], -1),
                               N_EXPERTS - 1).astype(i32)
    n_real = (pad_end[-1:] // MOE_BLOCK).astype(i32)

    n_slots = n_blocks * MOE_BLOCK
    hole_row = jnp.concatenate([pad_start + counts, pad_end[-1:]]).astype(i32)
    hole_n = jnp.concatenate([padded - counts, n_slots - pad_end[-1:]]).astype(i32)
    xs = _dispatch_call(hole_row, hole_n, dest, h1t, n_slots, tm)
    ys = _expert_call(block_expert, n_real, xs,
                      moe_w1[0].astype(MXU_DTYPE), moe_b1[0].reshape(N_EXPERTS, 1, 2 * D_FF),
                      moe_w2[0].astype(MXU_DTYPE), moe_b2[0].reshape(N_EXPERTS, 1, D_MODEL))

    out = _gather_combine_call(dest, r_g, h1, row(ln2_g[0]), row(ln2_b[0]), ys, min(256, t))
    return out.reshape(bsz, seq, d)
```

```python
import functools
import math

import numpy as np
import jax
import jax.numpy as jnp
from jax import lax
from jax.experimental import pallas as pl
from jax.experimental.pallas import tpu as pltpu

F32 = jnp.float32
MXU_DTYPE = jnp.bfloat16

D_MODEL = 1024
CHUNK = 64
GC = 4
GROUP = GC * CHUNK
HEADS = 4
DK = 64
DV = 128
GATE_RANK = 16
GATE_NORM = 16.0
ROPE_BASE = 10000.0
N_EXPERTS = 32
TOP_K = 4
D_FF = 1024
SWIGLU_ALPHA = 1.702
SWIGLU_LIMIT = 7.0
MOE_BLOCK = 256
DMA_UNROLL = 8
DISPATCH_STEPS = 64
LN_EPS = 1e-5
DEPTH = 1
DEEPNORM_ALPHA = (2.0 * DEPTH) ** 0.25

LANES = 128
VMEM_LIMIT = 56 * 1024 * 1024

C_GQ, C_GK, C_GV, C_GG, C_GLR = 0, 256, 512, 1024, 1536
C_RQ, C_RK, C_RV, C_RG = 1664, 1920, 2176, 2688
PROJ_W = 3200

LOG_GAMMA = [math.log1p(-(2.0 ** (-5.0 - h))) for h in range(HEADS)]


def _dot(a, b):
    return jnp.dot(a, b, preferred_element_type=F32)


def _dot_nt(a, b):
    return lax.dot_general(a, b, (((1,), (1,)), ((), ())), preferred_element_type=F32)


def _dot_tn(a, b):
    return lax.dot_general(a, b, (((0,), (0,)), ((), ())), preferred_element_type=F32)


def _mx(a):
    return a.astype(MXU_DTYPE)


def _layer_norm(x, g, b):
    mu = jnp.mean(x, -1, keepdims=True)
    xc = x - mu
    var = jnp.mean(xc * xc, -1, keepdims=True)
    return xc * lax.rsqrt(var + LN_EPS) * g + b


def _silu(x):
    return x * (1.0 / (1.0 + jnp.exp(-x)))


F32_ROWS = D_MODEL // LANES
PK_ROWS = D_MODEL // (2 * LANES)


def _load_token_tiles(ref, n, r):
    return jnp.concatenate([ref[pl.ds(j, n, stride=r), :] for j in range(r)], axis=1)


def _store_token_tiles(ref, rows):
    n, r = rows.shape[0], rows.shape[1] // LANES
    for j in range(r):
        ref[pl.ds(j, n, stride=r), :] = rows[:, j * LANES:(j + 1) * LANES]


def _pack_pairs(x):
    h = x.shape[1] // 2
    r = x.astype(jnp.bfloat16).astype(F32)
    lo = lax.bitcast_convert_type(r[:, :h], jnp.uint32) >> 16
    hi = lax.bitcast_convert_type(r[:, h:], jnp.uint32) & jnp.uint32(0xFFFF0000)
    return lo | hi


def _unpack_pairs(u):
    lo = lax.bitcast_convert_type(u << 16, F32)
    hi = lax.bitcast_convert_type(u & jnp.uint32(0xFFFF0000), F32)
    return jnp.concatenate([lo, hi], axis=1)


def _ln_proj_kernel(x_ref, g_ref, b_ref, w_ref, h_ref, p_ref):
    h = _layer_norm(x_ref[...], g_ref[...], b_ref[...])
    h_ref[...] = h
    p_ref[...] = _dot(_mx(h), w_ref[...])


def _ln_proj_call(x2, g, b, w, tm):
    t = x2.shape[0]
    return pl.pallas_call(
        _ln_proj_kernel,
        grid=(t // tm,),
        in_specs=[
            pl.BlockSpec((tm, D_MODEL), lambda i: (i, 0)),
            pl.BlockSpec((1, D_MODEL), lambda i: (0, 0)),
            pl.BlockSpec((1, D_MODEL), lambda i: (0, 0)),
            pl.BlockSpec((D_MODEL, PROJ_W), lambda i: (0, 0)),
        ],
        out_specs=[
            pl.BlockSpec((tm, D_MODEL), lambda i: (i, 0)),
            pl.BlockSpec((tm, PROJ_W), lambda i: (i, 0)),
        ],
        out_shape=[
            jax.ShapeDtypeStruct((t, D_MODEL), F32),
            jax.ShapeDtypeStruct((t, PROJ_W), F32),
        ],
        compiler_params=pltpu.CompilerParams(
            dimension_semantics=("arbitrary",), vmem_limit_bytes=VMEM_LIMIT),
        name="ln_proj",
    )(x2, g, b, w)


def _mixer_kernel(p_ref, pos_ref, gw_ref, gb_ref, gn_ref, rng_ref, rnb_ref, invf_ref, sgn_ref,
                  cd_ref, sd_ref, o_ref, st_ref, *, n_groups):
    @pl.when(pl.program_id(1) == 0)
    def _():
        st_ref[...] = jnp.zeros_like(st_ref)

    rr = lax.broadcasted_iota(jnp.int32, (GROUP, GROUP), 0)
    cc = lax.broadcasted_iota(jnp.int32, (GROUP, GROUP), 1)
    same = (rr >> 6) == (cc >> 6)
    lower = same & (rr >= cc)
    upper = same & (rr < cc)
    tri = lower.astype(MXU_DTYPE)
    dist = jnp.abs(rr - cc).astype(F32)
    rowi = lax.broadcasted_iota(jnp.int32, (GROUP, LANES), 0)
    rin = (rowi & (CHUNK - 1)).astype(F32)
    in_chunk = [(rowi >> 6) == c for c in range(GC)]
    lane = lax.broadcasted_iota(jnp.int32, (GROUP, LANES), 1)
    gla_half = [((lane >> 6) & 1) == i for i in range(2)]
    ret_half = [((lane >> 5) & 1) == i for i in range(2)]
    ret_d = [jnp.where(same, jnp.exp(LOG_GAMMA[h] * dist), 0.0) for h in range(HEADS)]
    ret_eb = [jnp.exp(LOG_GAMMA[h] * (rin + 1.0)) for h in range(HEADS)]
    ret_ek = [jnp.exp(LOG_GAMMA[h] * (CHUNK - 1.0 - rin)) for h in range(HEADS)]
    ret_dec = [math.exp(LOG_GAMMA[h] * CHUNK) for h in range(HEADS)]

    def block_diag(x):
        return _mx(jnp.concatenate([jnp.where(in_chunk[c], x, 0.0) for c in range(GC)], axis=1))

    def recur(hs, q_m, kv_t, decay):
        st = st_ref[hs]
        parts = []
        for c in range(GC):
            parts.append(_dot_nt(q_m[c * CHUNK:(c + 1) * CHUNK], _mx(st)))
            st = st * decay(c) + kv_t[:, c * LANES:(c + 1) * LANES]
        st_ref[hs] = st
        return jnp.concatenate(parts, axis=0)

    def group_body(g, carry):
        rows = pl.ds(pl.multiple_of(g * GROUP, GROUP), GROUP)

        z = _dot(_mx(p_ref[rows, C_GLR:C_GLR + LANES]), gw_ref[...]) + gb_ref[...]
        la = (jnp.minimum(z, 0.0) - jnp.log1p(jnp.exp(-jnp.abs(z)))) * (1.0 / GATE_NORM)
        la_hi = _mx(la)
        r1 = la - la_hi.astype(F32)
        la_mid = _mx(r1)
        la_lo = _mx(r1 - la_mid.astype(F32))
        b = _dot(tri, la_hi) + _dot(tri, la_mid) + _dot(tri, la_lo)
        b_ends = [b[(c + 1) * CHUNK - 1:(c + 1) * CHUNK, :] for c in range(GC)]
        b_last = jnp.concatenate([jnp.broadcast_to(e, (CHUNK, 256)) for e in b_ends], axis=0)
        decs = [jnp.exp(e) for e in b_ends]
        eb = jnp.exp(b)
        enb = jnp.exp(-b)
        ekv = jnp.exp(b_last - b)
        q = p_ref[rows, C_GQ:C_GQ + 256] * (DK ** -0.5)
        k = p_ref[rows, C_GK:C_GK + 256]
        qe, qn = q * eb, q * enb
        ke, kn, kk = k * eb, k * enb, k * ekv
        for h in range(HEADS):
            p, half = h // 2, h % 2
            ls = slice(p * LANES, (p + 1) * LANES)
            m = gla_half[half]
            qe_m = _mx(jnp.where(m, qe[:, ls], 0.0))
            qn_m = _mx(jnp.where(m, qn[:, ls], 0.0))
            s_lo = _dot_nt(qe_m, _mx(kn[:, ls]))
            s_up = _dot_nt(qn_m, _mx(ke[:, ls]))
            sc = jnp.where(lower, s_lo, jnp.where(upper, s_up, 0.0))
            v = _mx(p_ref[rows, C_GV + h * DV:C_GV + (h + 1) * DV])
            kv_t = _dot_tn(v, block_diag(kk[:, ls]))
            o = _dot(_mx(sc), v) + recur(h, qe_m, kv_t, lambda c: decs[c][:, ls])
            o = o * lax.rsqrt(jnp.mean(o * o, -1, keepdims=True) + LN_EPS) * gn_ref[...]
            o = o * _silu(p_ref[rows, C_GG + h * DV:C_GG + (h + 1) * DV])
            o_ref[rows, h * DV:(h + 1) * DV] = o.astype(o_ref.dtype)

        base = pos_ref[pl.ds(pl.multiple_of(g * GROUP, GROUP), 1), :].astype(F32) * invf_ref[...]
        c0, s0 = jnp.cos(base), jnp.sin(base)
        cs = c0 * cd_ref[...] - s0 * sd_ref[...]
        sn = (s0 * sgn_ref[...]) * cd_ref[...] + (c0 * sgn_ref[...]) * sd_ref[...]
        for p in range(2):
            tq = p_ref[rows, C_RQ + p * LANES:C_RQ + (p + 1) * LANES]
            tk = p_ref[rows, C_RK + p * LANES:C_RK + (p + 1) * LANES] * (DK ** -0.5)
            rq = tq * cs + pltpu.roll(tq, LANES // 2, 1) * sn
            rk = tk * cs + pltpu.roll(tk, LANES // 2, 1) * sn
            rk_m = _mx(rk)
            for half in range(2):
                h = 2 * p + half
                q_m = _mx(jnp.where(ret_half[half], rq, 0.0))
                s = _dot_nt(q_m, rk_m) * ret_d[h]
                v = _mx(p_ref[rows, C_RV + h * DV:C_RV + (h + 1) * DV])
                kv_t = _dot_tn(v, block_diag(rk * ret_ek[h]))
                o = _dot(_mx(s), v) + ret_eb[h] * recur(HEADS + h, q_m, kv_t, lambda c: ret_dec[h])
                mu = jnp.mean(o, -1, keepdims=True)
                oc = o - mu
                var = jnp.mean(oc * oc, -1, keepdims=True)
                o = oc * lax.rsqrt(var + LN_EPS) * rng_ref[:, h * DV:(h + 1) * DV] \
                    + rnb_ref[:, h * DV:(h + 1) * DV]
                o = o * _silu(p_ref[rows, C_RG + h * DV:C_RG + (h + 1) * DV])
                o_ref[rows, (HEADS + h) * DV:(HEADS + h + 1) * DV] = o.astype(o_ref.dtype)
        return carry

    lax.fori_loop(0, n_groups, group_body, 0)


def _mixer_call(proj, pos, gw, gb, gn, rng, rnb, invf, sgn, cd, sd, bsz, seq, ts):
    t = bsz * seq
    nst = seq // ts
    const = lambda b, s: (0, 0)
    return pl.pallas_call(
        functools.partial(_mixer_kernel, n_groups=ts // GROUP),
        grid=(bsz, nst),
        in_specs=[
            pl.BlockSpec((ts, PROJ_W), lambda b, s: (b * nst + s, 0)),
            pl.BlockSpec((ts, 1), lambda b, s: (b * nst + s, 0)),
            pl.BlockSpec((LANES, 256), const),
            pl.BlockSpec((1, 256), const),
            pl.BlockSpec((1, DV), const),
            pl.BlockSpec((1, HEADS * DV), const),
            pl.BlockSpec((1, HEADS * DV), const),
            pl.BlockSpec((1, LANES), const),
            pl.BlockSpec((1, LANES), const),
            pl.BlockSpec((GROUP, LANES), const),
            pl.BlockSpec((GROUP, LANES), const),
        ],
        out_specs=pl.BlockSpec((ts, D_MODEL), lambda b, s: (b * nst + s, 0)),
        out_shape=jax.ShapeDtypeStruct((t, D_MODEL), MXU_DTYPE),
        scratch_shapes=[pltpu.VMEM((2 * HEADS, DV, LANES), F32)],
        compiler_params=pltpu.CompilerParams(
            dimension_semantics=("arbitrary", "arbitrary"), vmem_limit_bytes=VMEM_LIMIT),
        name="mixers",
    )(proj, pos, gw, gb, gn, rng, rnb, invf, sgn, cd, sd)


def _post_mix_kernel(o_ref, h0_ref, wo_ref, g_ref, b_ref, rw_ref, rb_ref,
                     h1_ref, h1t_ref, ri_ref, rg_ref, cnt_ref, tril_ref, carry_ref, *, tm):
    @pl.when(pl.program_id(0) == 0)
    def _():
        r = lax.broadcasted_iota(jnp.int32, (tm, tm), 0)
        c = lax.broadcasted_iota(jnp.int32, (tm, tm), 1)
        tril_ref[...] = (r > c).astype(MXU_DTYPE)
        carry_ref[...] = jnp.zeros_like(carry_ref)

    mix = _dot(o_ref[...], wo_ref[...])
    h1 = _layer_norm(DEEPNORM_ALPHA * h0_ref[...] + mix, g_ref[...], b_ref[...])
    h1_ref[...] = h1
    _store_token_tiles(h1t_ref, _pack_pairs(h1))

    lane = lax.broadcasted_iota(jnp.int32, (tm, LANES), 1)
    lane_f = lane.astype(F32)
    logits = _dot(_mx(h1), rw_ref[...]) + rb_ref[...]
    l = jnp.where(lane < N_EXPERTS, logits, -jnp.inf)
    vals, idxs = [], []
    for _ in range(TOP_K):
        m = jnp.max(l, -1, keepdims=True)
        i = jnp.min(jnp.where(l == m, lane_f, float(LANES)), -1, keepdims=True)
        vals.append(m)
        idxs.append(i)
        l = jnp.where(lane_f == i, -jnp.inf, l)
    exps = [jnp.exp(v - vals[0]) for v in vals]
    inv = 1.0 / (exps[0] + exps[1] + exps[2] + exps[3])

    onehot = jnp.zeros((tm, LANES), F32)
    for i in idxs:
        onehot = onehot + (lane_f == i).astype(F32)
    before = _dot(tril_ref[...], _mx(onehot)) + carry_ref[...]
    packed_i = jnp.zeros((tm, LANES), F32)
    packed_g = jnp.zeros((tm, LANES), F32)
    for k in range(TOP_K):
        rank = jnp.sum(jnp.where(lane_f == idxs[k], before, 0.0), -1, keepdims=True)
        packed_i = jnp.where(lane == k, idxs[k], packed_i)
        packed_i = jnp.where(lane == TOP_K + k, rank, packed_i)
        packed_g = jnp.where(lane == k, exps[k] * inv, packed_g)
    ri_ref[...] = packed_i.astype(jnp.int32)
    rg_ref[...] = packed_g
    carry_ref[...] = carry_ref[...] + jnp.sum(onehot, 0, keepdims=True)
    cnt_ref[...] = carry_ref[...]


def _post_mix_call(o, h0, wo, g, b, rw, rb, tm):
    t = o.shape[0]
    const = lambda i: (0, 0)
    return pl.pallas_call(
        functools.partial(_post_mix_kernel, tm=tm),
        grid=(t // tm,),
        in_specs=[
            pl.BlockSpec((tm, D_MODEL), lambda i: (i, 0)),
            pl.BlockSpec((tm, D_MODEL), lambda i: (i, 0)),
            pl.BlockSpec((D_MODEL, D_MODEL), const),
            pl.BlockSpec((1, D_MODEL), const),
            pl.BlockSpec((1, D_MODEL), const),
            pl.BlockSpec((D_MODEL, LANES), const),
            pl.BlockSpec((1, LANES), const),
        ],
        out_specs=[
            pl.BlockSpec((tm, D_MODEL), lambda i: (i, 0)),
            pl.BlockSpec((tm * PK_ROWS, LANES), lambda i: (i, 0)),
            pl.BlockSpec((tm, LANES), lambda i: (i, 0)),
            pl.BlockSpec((tm, LANES), lambda i: (i, 0)),
            pl.BlockSpec((1, LANES), const),
        ],
        out_shape=[
            jax.ShapeDtypeStruct((t, D_MODEL), F32),
            jax.ShapeDtypeStruct((t * PK_ROWS, LANES), jnp.uint32),
            jax.ShapeDtypeStruct((t, LANES), jnp.int32),
            jax.ShapeDtypeStruct((t, LANES), F32),
            jax.ShapeDtypeStruct((1, LANES), F32),
        ],
        scratch_shapes=[pltpu.VMEM((tm, tm), MXU_DTYPE), pltpu.VMEM((1, LANES), F32)],
        compiler_params=pltpu.CompilerParams(
            dimension_semantics=("arbitrary",), vmem_limit_bytes=VMEM_LIMIT),
        name="post_mix_router",
    )(o, h0, wo, g, b, rw, rb)


def _dispatch_kernel(pad_row_ref, pad_n_ref, dest_ref, h1p_ref, w1_ref, w2_ref,
                     xs_hbm, w1c_ref, w2c_ref, zero_ref, sem, *, tm):
    i = pl.program_id(0)

    def tile(ref, t):
        return ref.at[pl.ds(pl.multiple_of(t * PK_ROWS, PK_ROWS), PK_ROWS), :]

    def body(it, c):
        for u in range(DMA_UNROLL):
            r = it * DMA_UNROLL + u
            for k in range(TOP_K):
                pltpu.make_async_copy(tile(h1p_ref, r), tile(xs_hbm, dest_ref[0, 0, r * TOP_K + k]),
                                      sem.at[0]).start(priority=k % 2)
        return c
    lax.fori_loop(0, tm // DMA_UNROLL, body, 0)
    w1c_ref[...] = _mx(w1_ref[...])
    w2c_ref[...] = _mx(w2_ref[...])
    for _ in range(TOP_K):
        pltpu.make_async_copy(h1p_ref, xs_hbm.at[pl.ds(0, tm * PK_ROWS), :], sem.at[0]).wait()

    @pl.when(i == pl.num_programs(0) - 1)
    def _():
        zero_ref[...] = jnp.zeros_like(zero_ref)
        zero = zero_ref.at[pl.ds(0, PK_ROWS), :]

        def per_expert(e, c):
            def start(r, c2):
                pltpu.make_async_copy(zero, tile(xs_hbm, pad_row_ref[e] + r), sem.at[1]).start()
                return c2
            lax.fori_loop(0, pad_n_ref[e], start, 0)

            def wait(r, c2):
                pltpu.make_async_copy(zero, tile(xs_hbm, 0), sem.at[1]).wait()
                return c2
            lax.fori_loop(0, pad_n_ref[e], wait, 0)
            return c
        lax.fori_loop(0, pad_row_ref.shape[0], per_expert, 0)


def _dispatch_call(pad_row, pad_n, dest, h1p, w1, w2, n_slots, n_steps):
    t = h1p.shape[0] // PK_ROWS
    tm = t // n_steps
    r1, r2 = w1.shape[0] // n_steps, w2.shape[0] // n_steps
    step = lambda i, pr, pn: (i, 0)
    grid_spec = pltpu.PrefetchScalarGridSpec(
        num_scalar_prefetch=2,
        grid=(n_steps,),
        in_specs=[
            pl.BlockSpec((1, 1, tm * TOP_K), lambda i, pr, pn: (i, 0, 0), memory_space=pltpu.SMEM),
            pl.BlockSpec((tm * PK_ROWS, LANES), step),
            pl.BlockSpec((r1, w1.shape[1]), step),
            pl.BlockSpec((r2, w2.shape[1]), step),
        ],
        out_specs=[
            pl.BlockSpec(memory_space=pl.ANY),
            pl.BlockSpec((r1, w1.shape[1]), step),
            pl.BlockSpec((r2, w2.shape[1]), step),
        ],
        scratch_shapes=[pltpu.VMEM((F32_ROWS, LANES), jnp.uint32), pltpu.SemaphoreType.DMA((2,))],
    )
    return pl.pallas_call(
        functools.partial(_dispatch_kernel, tm=tm),
        grid_spec=grid_spec,
        out_shape=[
            jax.ShapeDtypeStruct((n_slots * PK_ROWS, LANES), jnp.uint32),
            jax.ShapeDtypeStruct(w1.shape, MXU_DTYPE),
            jax.ShapeDtypeStruct(w2.shape, MXU_DTYPE),
        ],
        compiler_params=pltpu.CompilerParams(
            dimension_semantics=("arbitrary",), vmem_limit_bytes=VMEM_LIMIT),
        name="dispatch",
    )(pad_row, pad_n, dest.reshape(n_steps, 1, tm * TOP_K), h1p, w1, w2)


def _expert_kernel(be_ref, nr_ref, x_ref, w1_ref, b1_ref, w2_ref, b2_ref, y_ref):
    @pl.when(pl.program_id(0) >= nr_ref[0])
    def _():
        y_ref[...] = jnp.zeros_like(y_ref)

    @pl.when(pl.program_id(0) < nr_ref[0])
    def _():
        x = _mx(_unpack_pairs(_load_token_tiles(x_ref, MOE_BLOCK, PK_ROWS)))
        hh = _dot(x, w1_ref[0]) + b1_ref[0]
        x_glu = jnp.minimum(hh[:, :D_FF], SWIGLU_LIMIT)
        x_lin = jnp.clip(hh[:, D_FF:], -SWIGLU_LIMIT, SWIGLU_LIMIT)
        act = x_glu * (1.0 / (1.0 + jnp.exp(-SWIGLU_ALPHA * x_glu))) * (x_lin + 1.0)
        _store_token_tiles(y_ref, _dot(_mx(act), w2_ref[0]) + b2_ref[0])


def _expert_call(block_expert, n_real, xs, w1, b1, w2, b2):
    n_blocks = block_expert.shape[0]
    blk = lambda g, be, nr: (jnp.minimum(g, nr[0] - 1), 0)
    ex = lambda g, be, nr: (be[g], 0, 0)
    grid_spec = pltpu.PrefetchScalarGridSpec(
        num_scalar_prefetch=2,
        grid=(n_blocks,),
        in_specs=[
            pl.BlockSpec((MOE_BLOCK * PK_ROWS, LANES), blk),
            pl.BlockSpec((1, D_MODEL, 2 * D_FF), ex),
            pl.BlockSpec((1, 1, 2 * D_FF), ex),
            pl.BlockSpec((1, D_FF, D_MODEL), ex),
            pl.BlockSpec((1, 1, D_MODEL), ex),
        ],
        out_specs=pl.BlockSpec((MOE_BLOCK * F32_ROWS, LANES), lambda g, be, nr: (g, 0)),
    )
    return pl.pallas_call(
        _expert_kernel,
        grid_spec=grid_spec,
        out_shape=jax.ShapeDtypeStruct((n_blocks * MOE_BLOCK * F32_ROWS, LANES), F32),
        compiler_params=pltpu.CompilerParams(
            dimension_semantics=("arbitrary",), vmem_limit_bytes=VMEM_LIMIT),
        name="moe_experts",
    )(block_expert, n_real, xs, w1, b1, w2, b2)


def _gather_combine_kernel(dest0_ref, destn_ref, gate_ref, h1_ref, g_ref, b_ref, ys_hbm,
                           out_ref, ybuf, sem, *, tm):
    i = pl.program_id(0)
    slot = i % 2
    rows = tm * F32_ROWS

    def tile(ref, t):
        return ref.at[pl.ds(pl.multiple_of(t * F32_ROWS, F32_ROWS), F32_ROWS), :]

    def gather(tbl, s):
        def body(it, c):
            for u in range(DMA_UNROLL):
                r = it * DMA_UNROLL + u
                for k in range(TOP_K):
                    pltpu.make_async_copy(tile(ys_hbm, tbl[0, 0, r * TOP_K + k]),
                                          tile(ybuf, (s * TOP_K + k) * tm + r), sem.at[s]).start(priority=k % 2)
            return c
        lax.fori_loop(0, tm // DMA_UNROLL, body, 0)

    @pl.when(i == 0)
    def _():
        gather(dest0_ref, 0)

    @pl.when(i + 1 < pl.num_programs(0))
    def _():
        gather(destn_ref, 1 - slot)

    for k in range(TOP_K):
        plane = ybuf.at[pl.ds(pl.multiple_of((slot * TOP_K + k) * rows, rows), rows), :]
        pltpu.make_async_copy(ys_hbm.at[pl.ds(0, rows), :], plane, sem.at[slot]).wait()

    gates = gate_ref[...]
    acc = DEEPNORM_ALPHA * h1_ref[...]
    for k in range(TOP_K):
        plane = ybuf.at[pl.ds(pl.multiple_of((slot * TOP_K + k) * rows, rows), rows), :]
        acc = acc + gates[:, k:k + 1] * _load_token_tiles(plane, tm, F32_ROWS)
    out_ref[...] = _layer_norm(acc, g_ref[...], b_ref[...])


def _gather_combine_call(dest, gates, h1, g, b, ys, tm):
    t = h1.shape[0]
    nt = t // tm
    const = lambda i: (0, 0)
    smem = functools.partial(pl.BlockSpec, memory_space=pltpu.SMEM)
    dest = dest.reshape(nt, 1, tm * TOP_K)
    dest_next = jnp.concatenate([dest[1:], dest[:1]], axis=0)
    return pl.pallas_call(
        functools.partial(_gather_combine_kernel, tm=tm),
        grid=(nt,),
        in_specs=[
            smem((1, 1, tm * TOP_K), lambda i: (0, 0, 0)),
            smem((1, 1, tm * TOP_K), lambda i: (i, 0, 0)),
            pl.BlockSpec((tm, LANES), lambda i: (i, 0)),
            pl.BlockSpec((tm, D_MODEL), lambda i: (i, 0)),
            pl.BlockSpec((1, D_MODEL), const),
            pl.BlockSpec((1, D_MODEL), const),
            pl.BlockSpec(memory_space=pl.ANY),
        ],
        out_specs=pl.BlockSpec((tm, D_MODEL), lambda i: (i, 0)),
        out_shape=jax.ShapeDtypeStruct((t, D_MODEL), F32),
        scratch_shapes=[pltpu.VMEM((2 * TOP_K * tm * F32_ROWS, LANES), F32), pltpu.SemaphoreType.DMA((2,))],
        compiler_params=pltpu.CompilerParams(
            dimension_semantics=("arbitrary",), vmem_limit_bytes=VMEM_LIMIT),
        name="gather_combine_ln2",
    )(dest, dest_next, gates, h1, g, b, ys)


def _relayout_w_in(w):
    sizes = (256, 256, 512, 512, GATE_RANK, 256, 256, 512, 512)
    offs = np.concatenate([[0], np.cumsum(sizes)])
    gq, gk, gv, gg, glr, rq, rk, rv, rg = [w[:, offs[i]:offs[i + 1]] for i in range(9)]
    perm = np.zeros((256,), np.int32)
    for p in range(2):
        for l in range(LANES):
            part, hh, f = l // 64, (l % 64) // 32, l % 32
            perm[p * LANES + l] = (2 * p + hh) * DK + part * (DK // 2) + f
    glr = jnp.pad(glr, ((0, 0), (0, LANES - GATE_RANK)))
    return jnp.concatenate([gq, gk, gv, gg, glr, rq[:, perm], rk[:, perm], rv, rg], axis=1)


def kernel(x, positions, ln_in_g, ln_in_b, w_in, gla_gate_w, gla_gate_b, gla_norm_g, ret_norm_g,
           ret_norm_b, w_out, ln1_g, ln1_b, router_w, router_b, moe_w1, moe_b1, moe_w2, moe_b2,
           ln2_g, ln2_b):
    bsz, seq, d = x.shape
    assert d == D_MODEL and seq % GROUP == 0 and w_in.shape[0] == DEPTH == 1
    t = bsz * seq
    ts = 1024 if seq % 1024 == 0 else GROUP
    tm = min(512, t)
    assert seq % ts == 0 and t % tm == 0
    row = lambda v: v.reshape(1, -1).astype(F32)

    w_p = _relayout_w_in(w_in[0]).astype(MXU_DTYPE)
    h0, proj = _ln_proj_call(x.reshape(t, d), row(ln_in_g), row(ln_in_b), w_p, tm)

    gw = jnp.pad(gla_gate_w[0], ((0, LANES - GATE_RANK), (0, 0))).astype(MXU_DTYPE)
    half = DK // 2
    inv_freq = 1.0 / (ROPE_BASE ** np.linspace(0.0, 1.0, half, dtype=np.float32))
    invf = jnp.asarray(np.tile(inv_freq, LANES // half).reshape(1, LANES), F32)
    sgn = jnp.asarray(np.where(np.arange(LANES) < LANES // 2, -1.0, 1.0).reshape(1, LANES), F32)
    step_ang = np.arange(GROUP, dtype=np.float64)[:, None] * np.tile(inv_freq, LANES // half)[None, :]
    cd = jnp.asarray(np.cos(step_ang), F32)
    sd = jnp.asarray(np.sin(step_ang), F32)
    o = _mixer_call(proj, positions.reshape(t, 1), gw, row(gla_gate_b[0]), row(gla_norm_g[0]),
                    row(ret_norm_g[0]), row(ret_norm_b[0]), invf, sgn, cd, sd, bsz, seq, ts)

    rw = jnp.pad(router_w[0], ((0, 0), (0, LANES - N_EXPERTS))).astype(MXU_DTYPE)
    rb = jnp.pad(router_b[0], (0, LANES - N_EXPERTS)).reshape(1, LANES).astype(F32)
    h1, h1t, r_i, r_g, cnt = _post_mix_call(o, h0, w_out[0].astype(MXU_DTYPE), row(ln1_g[0]),
                                       row(ln1_b[0]), rw, rb, tm)

    e = r_i[:, :TOP_K]
    rank = r_i[:, TOP_K:2 * TOP_K]
    counts = cnt[0, :N_EXPERTS].astype(jnp.int32)
    padded = ((counts + MOE_BLOCK - 1) // MOE_BLOCK) * MOE_BLOCK
    pad_end = jnp.cumsum(padded)
    pad_start = pad_end - padded
    tk = t * TOP_K
    n_blocks = (tk + MOE_BLOCK - 1) // MOE_BLOCK + N_EXPERTS
    i32 = jnp.int32
    ex = jnp.arange(N_EXPERTS, dtype=i32)
    dest = jnp.sum(jnp.where(e[..., None] == ex, pad_start, 0), -1) + rank
    block_start = jnp.arange(n_blocks, dtype=i32) * MOE_BLOCK
    block_expert = jnp.minimum(jnp.sum(block_start[:, None] >= pad_end[None, :], -1),
                               N_EXPERTS - 1).astype(i32)
    n_real = (pad_end[-1:] // MOE_BLOCK).astype(i32)

    n_slots = n_blocks * MOE_BLOCK
    hole_row = jnp.concatenate([pad_start + counts, pad_end[-1:]]).astype(i32)
    hole_n = jnp.concatenate([padded - counts, n_slots - pad_end[-1:]]).astype(i32)
    xs, w1c, w2c = _dispatch_call(hole_row, hole_n, dest, h1t,
                                  moe_w1[0].reshape(N_EXPERTS * D_MODEL, 2 * D_FF),
                                  moe_w2[0].reshape(N_EXPERTS * D_FF, D_MODEL), n_slots, DISPATCH_STEPS)
    ys = _expert_call(block_expert, n_real, xs,
                      w1c.reshape(N_EXPERTS, D_MODEL, 2 * D_FF), moe_b1[0].reshape(N_EXPERTS, 1, 2 * D_FF),
                      w2c.reshape(N_EXPERTS, D_FF, D_MODEL), moe_b2[0].reshape(N_EXPERTS, 1, D_MODEL))

    out = _gather_combine_call(dest, r_g, h1, row(ln2_g[0]), row(ln2_b[0]), ys, min(256, t))
    return out.reshape(bsz, seq, d)
```

```python
import functools
import math

import numpy as np
import jax
import jax.numpy as jnp
from jax import lax
from jax.experimental import pallas as pl
from jax.experimental.pallas import tpu as pltpu

F32 = jnp.float32
MXU_DTYPE = jnp.bfloat16

D_MODEL = 1024
CHUNK = 64
GC = 4
GROUP = GC * CHUNK
HEADS = 4
DK = 64
DV = 128
GATE_RANK = 16
GATE_NORM = 16.0
ROPE_BASE = 10000.0
N_EXPERTS = 32
TOP_K = 4
D_FF = 1024
SWIGLU_ALPHA = 1.702
SWIGLU_LIMIT = 7.0
MOE_BLOCK = 256
DMA_UNROLL = 8
DISPATCH_STEPS = 64
LN_EPS = 1e-5
DEPTH = 1
DEEPNORM_ALPHA = (2.0 * DEPTH) ** 0.25

LANES = 128
VMEM_LIMIT = 56 * 1024 * 1024

C_GQ, C_GK, C_GV, C_GG, C_GLR = 0, 256, 512, 1024, 1536
C_RQ, C_RK, C_RV, C_RG = 1664, 1920, 2176, 2688
PROJ_W = 3200

LOG_GAMMA = [math.log1p(-(2.0 ** (-5.0 - h))) for h in range(HEADS)]


def _dot(a, b):
    return jnp.dot(a, b, preferred_element_type=F32)


def _dot_nt(a, b):
    return lax.dot_general(a, b, (((1,), (1,)), ((), ())), preferred_element_type=F32)


def _dot_tn(a, b):
    return lax.dot_general(a, b, (((0,), (0,)), ((), ())), preferred_element_type=F32)


def _mx(a):
    return a.astype(MXU_DTYPE)


def _layer_norm(x, g, b):
    mu = jnp.mean(x, -1, keepdims=True)
    xc = x - mu
    var = jnp.mean(xc * xc, -1, keepdims=True)
    return xc * lax.rsqrt(var + LN_EPS) * g + b


def _silu(x):
    return x * (1.0 / (1.0 + jnp.exp(-x)))


F32_ROWS = D_MODEL // LANES
PK_ROWS = D_MODEL // (2 * LANES)


def _load_token_tiles(ref, n, r):
    return jnp.concatenate([ref[pl.ds(j, n, stride=r), :] for j in range(r)], axis=1)


def _store_token_tiles(ref, rows):
    n, r = rows.shape[0], rows.shape[1] // LANES
    for j in range(r):
        ref[pl.ds(j, n, stride=r), :] = rows[:, j * LANES:(j + 1) * LANES]


def _pack_pairs(x):
    h = x.shape[1] // 2
    r = x.astype(jnp.bfloat16).astype(F32)
    lo = lax.bitcast_convert_type(r[:, :h], jnp.uint32) >> 16
    hi = lax.bitcast_convert_type(r[:, h:], jnp.uint32) & jnp.uint32(0xFFFF0000)
    return lo | hi


def _unpack_pairs(u):
    lo = lax.bitcast_convert_type(u << 16, F32)
    hi = lax.bitcast_convert_type(u & jnp.uint32(0xFFFF0000), F32)
    return jnp.concatenate([lo, hi], axis=1)


def _ln_proj_kernel(x_ref, g_ref, b_ref, w_ref, h_ref, p_ref):
    h = _layer_norm(x_ref[...], g_ref[...], b_ref[...])
    h_ref[...] = h
    p_ref[...] = _dot(_mx(h), w_ref[...])


def _ln_proj_call(x2, g, b, w, tm):
    t = x2.shape[0]
    return pl.pallas_call(
        _ln_proj_kernel,
        grid=(t // tm,),
        in_specs=[
            pl.BlockSpec((tm, D_MODEL), lambda i: (i, 0)),
            pl.BlockSpec((1, D_MODEL), lambda i: (0, 0)),
            pl.BlockSpec((1, D_MODEL), lambda i: (0, 0)),
            pl.BlockSpec((D_MODEL, PROJ_W), lambda i: (0, 0)),
        ],
        out_specs=[
            pl.BlockSpec((tm, D_MODEL), lambda i: (i, 0)),
            pl.BlockSpec((tm, PROJ_W), lambda i: (i, 0)),
        ],
        out_shape=[
            jax.ShapeDtypeStruct((t, D_MODEL), F32),
            jax.ShapeDtypeStruct((t, PROJ_W), F32),
        ],
        compiler_params=pltpu.CompilerParams(
            dimension_semantics=("arbitrary",), vmem_limit_bytes=VMEM_LIMIT),
        name="ln_proj",
    )(x2, g, b, w)


def _mixer_kernel(p_ref, pos_ref, gw_ref, gb_ref, gn_ref, rng_ref, rnb_ref, invf_ref, sgn_ref,
                  cd_ref, sd_ref, o_ref, st_ref, *, n_groups):
    @pl.when(pl.program_id(1) == 0)
    def _():
        st_ref[...] = jnp.zeros_like(st_ref)

    rr = lax.broadcasted_iota(jnp.int32, (GROUP, GROUP), 0)
    cc = lax.broadcasted_iota(jnp.int32, (GROUP, GROUP), 1)
    same = (rr >> 6) == (cc >> 6)
    lower = same & (rr >= cc)
    upper = same & (rr < cc)
    tri = lower.astype(MXU_DTYPE)
    dist = jnp.abs(rr - cc).astype(F32)
    rowi = lax.broadcasted_iota(jnp.int32, (GROUP, LANES), 0)
    rin = (rowi & (CHUNK - 1)).astype(F32)
    in_chunk = [(rowi >> 6) == c for c in range(GC)]
    lane = lax.broadcasted_iota(jnp.int32, (GROUP, LANES), 1)
    gla_half = [((lane >> 6) & 1) == i for i in range(2)]
    ret_half = [((lane >> 5) & 1) == i for i in range(2)]
    ret_d = [jnp.where(same, jnp.exp(LOG_GAMMA[h] * dist), 0.0) for h in range(HEADS)]
    ret_eb = [jnp.exp(LOG_GAMMA[h] * (rin + 1.0)) for h in range(HEADS)]
    ret_ek = [jnp.exp(LOG_GAMMA[h] * (CHUNK - 1.0 - rin)) for h in range(HEADS)]
    ret_dec = [math.exp(LOG_GAMMA[h] * CHUNK) for h in range(HEADS)]

    def block_diag(x):
        return _mx(jnp.concatenate([jnp.where(in_chunk[c], x, 0.0) for c in range(GC)], axis=1))

    def recur(hs, q_m, kv_t, decay):
        st = st_ref[hs]
        parts = []
        for c in range(GC):
            parts.append(_dot_nt(q_m[c * CHUNK:(c + 1) * CHUNK], _mx(st)))
            st = st * decay(c) + kv_t[:, c * LANES:(c + 1) * LANES]
        st_ref[hs] = st
        return jnp.concatenate(parts, axis=0)

    def group_body(g, carry):
        rows = pl.ds(pl.multiple_of(g * GROUP, GROUP), GROUP)

        z = _dot(_mx(p_ref[rows, C_GLR:C_GLR + LANES]), gw_ref[...]) + gb_ref[...]
        la = (jnp.minimum(z, 0.0) - jnp.log1p(jnp.exp(-jnp.abs(z)))) * (1.0 / GATE_NORM)
        la_hi = _mx(la)
        r1 = la - la_hi.astype(F32)
        la_mid = _mx(r1)
        la_lo = _mx(r1 - la_mid.astype(F32))
        b = _dot(tri, la_hi) + _dot(tri, la_mid) + _dot(tri, la_lo)
        b_ends = [b[(c + 1) * CHUNK - 1:(c + 1) * CHUNK, :] for c in range(GC)]
        b_last = jnp.concatenate([jnp.broadcast_to(e, (CHUNK, 256)) for e in b_ends], axis=0)
        decs = [jnp.exp(e) for e in b_ends]
        eb = jnp.exp(b)
        enb = jnp.exp(-b)
        ekv = jnp.exp(b_last - b)
        q = p_ref[rows, C_GQ:C_GQ + 256] * (DK ** -0.5)
        k = p_ref[rows, C_GK:C_GK + 256]
        qe, qn = q * eb, q * enb
        ke, kn, kk = k * eb, k * enb, k * ekv
        for h in range(HEADS):
            p, half = h // 2, h % 2
            ls = slice(p * LANES, (p + 1) * LANES)
            m = gla_half[half]
            qe_m = _mx(jnp.where(m, qe[:, ls], 0.0))
            qn_m = _mx(jnp.where(m, qn[:, ls], 0.0))
            s_lo = _dot_nt(qe_m, _mx(kn[:, ls]))
            s_up = _dot_nt(qn_m, _mx(ke[:, ls]))
            sc = jnp.where(lower, s_lo, jnp.where(upper, s_up, 0.0))
            v = _mx(p_ref[rows, C_GV + h * DV:C_GV + (h + 1) * DV])
            kv_t = _dot_tn(v, block_diag(kk[:, ls]))
            o = _dot(_mx(sc), v) + recur(h, qe_m, kv_t, lambda c: decs[c][:, ls])
            o = o * lax.rsqrt(jnp.mean(o * o, -1, keepdims=True) + LN_EPS) * gn_ref[...]
            o = o * _silu(p_ref[rows, C_GG + h * DV:C_GG + (h + 1) * DV])
            o_ref[rows, h * DV:(h + 1) * DV] = o.astype(o_ref.dtype)

        base = pos_ref[pl.ds(pl.multiple_of(g * GROUP, GROUP), 1), :].astype(F32) * invf_ref[...]
        c0, s0 = jnp.cos(base), jnp.sin(base)
        cs = c0 * cd_ref[...] - s0 * sd_ref[...]
        sn = (s0 * sgn_ref[...]) * cd_ref[...] + (c0 * sgn_ref[...]) * sd_ref[...]
        for p in range(2):
            tq = p_ref[rows, C_RQ + p * LANES:C_RQ + (p + 1) * LANES]
            tk = p_ref[rows, C_RK + p * LANES:C_RK + (p + 1) * LANES] * (DK ** -0.5)
            rq = tq * cs + pltpu.roll(tq, LANES // 2, 1) * sn
            rk = tk * cs + pltpu.roll(tk, LANES // 2, 1) * sn
            rk_m = _mx(rk)
            for half in range(2):
                h = 2 * p + half
                q_m = _mx(jnp.where(ret_half[half], rq, 0.0))
                s = _dot_nt(q_m, rk_m) * ret_d[h]
                v = _mx(p_ref[rows, C_RV + h * DV:C_RV + (h + 1) * DV])
                kv_t = _dot_tn(v, block_diag(rk * ret_ek[h]))
                o = _dot(_mx(s), v) + ret_eb[h] * recur(HEADS + h, q_m, kv_t, lambda c: ret_dec[h])
                mu = jnp.mean(o, -1, keepdims=True)
                oc = o - mu
                var = jnp.mean(oc * oc, -1, keepdims=True)
                o = oc * lax.rsqrt(var + LN_EPS) * rng_ref[:, h * DV:(h + 1) * DV] \
                    + rnb_ref[:, h * DV:(h + 1) * DV]
                o = o * _silu(p_ref[rows, C_RG + h * DV:C_RG + (h + 1) * DV])
                o_ref[rows, (HEADS + h) * DV:(HEADS + h + 1) * DV] = o.astype(o_ref.dtype)
        return carry

    lax.fori_loop(0, n_groups, group_body, 0)


def _mixer_call(proj, pos, gw, gb, gn, rng, rnb, invf, sgn, cd, sd, bsz, seq, ts):
    t = bsz * seq
    nst = seq // ts
    const = lambda b, s: (0, 0)
    return pl.pallas_call(
        functools.partial(_mixer_kernel, n_groups=ts // GROUP),
        grid=(bsz, nst),
        in_specs=[
            pl.BlockSpec((ts, PROJ_W), lambda b, s: (b * nst + s, 0)),
            pl.BlockSpec((ts, 1), lambda b, s: (b * nst + s, 0)),
            pl.BlockSpec((LANES, 256), const),
            pl.BlockSpec((1, 256), const),
            pl.BlockSpec((1, DV), const),
            pl.BlockSpec((1, HEADS * DV), const),
            pl.BlockSpec((1, HEADS * DV), const),
            pl.BlockSpec((1, LANES), const),
            pl.BlockSpec((1, LANES), const),
            pl.BlockSpec((GROUP, LANES), const),
            pl.BlockSpec((GROUP, LANES), const),
        ],
        out_specs=pl.BlockSpec((ts, D_MODEL), lambda b, s: (b * nst + s, 0)),
        out_shape=jax.ShapeDtypeStruct((t, D_MODEL), MXU_DTYPE),
        scratch_shapes=[pltpu.VMEM((2 * HEADS, DV, LANES), F32)],
        compiler_params=pltpu.CompilerParams(
            dimension_semantics=("arbitrary", "arbitrary"), vmem_limit_bytes=VMEM_LIMIT),
        name="mixers",
    )(proj, pos, gw, gb, gn, rng, rnb, invf, sgn, cd, sd)


def _post_mix_kernel(o_ref, h0_ref, wo_ref, g_ref, b_ref, rw_ref, rb_ref,
                     h1_ref, h1t_ref, ri_ref, rg_ref, cnt_ref, tril_ref, carry_ref, *, tm):
    @pl.when(pl.program_id(0) == 0)
    def _():
        r = lax.broadcasted_iota(jnp.int32, (tm, tm), 0)
        c = lax.broadcasted_iota(jnp.int32, (tm, tm), 1)
        tril_ref[...] = (r > c).astype(MXU_DTYPE)
        carry_ref[...] = jnp.zeros_like(carry_ref)

    mix = _dot(o_ref[...], wo_ref[...])
    h1 = _layer_norm(DEEPNORM_ALPHA * h0_ref[...] + mix, g_ref[...], b_ref[...])
    h1_ref[...] = h1
    _store_token_tiles(h1t_ref, _pack_pairs(h1))

    lane = lax.broadcasted_iota(jnp.int32, (tm, LANES), 1)
    lane_f = lane.astype(F32)
    logits = _dot(_mx(h1), rw_ref[...]) + rb_ref[...]
    l = jnp.where(lane < N_EXPERTS, logits, -jnp.inf)
    vals, idxs = [], []
    for _ in range(TOP_K):
        m = jnp.max(l, -1, keepdims=True)
        i = jnp.min(jnp.where(l == m, lane_f, float(LANES)), -1, keepdims=True)
        vals.append(m)
        idxs.append(i)
        l = jnp.where(lane_f == i, -jnp.inf, l)
    exps = [jnp.exp(v - vals[0]) for v in vals]
    inv = 1.0 / (exps[0] + exps[1] + exps[2] + exps[3])

    onehot = jnp.zeros((tm, LANES), F32)
    for i in idxs:
        onehot = onehot + (lane_f == i).astype(F32)
    before = _dot(tril_ref[...], _mx(onehot)) + carry_ref[...]
    packed_i = jnp.zeros((tm, LANES), F32)
    packed_g = jnp.zeros((tm, LANES), F32)
    for k in range(TOP_K):
        rank = jnp.sum(jnp.where(lane_f == idxs[k], before, 0.0), -1, keepdims=True)
        packed_i = jnp.where(lane == k, idxs[k], packed_i)
        packed_i = jnp.where(lane == TOP_K + k, rank, packed_i)
        packed_g = jnp.where(lane == k, exps[k] * inv, packed_g)
    ri_ref[...] = packed_i.astype(jnp.int32)
    rg_ref[...] = packed_g
    carry_ref[...] = carry_ref[...] + jnp.sum(onehot, 0, keepdims=True)
    cnt_ref[...] = carry_ref[...]


def _post_mix_call(o, h0, wo, g, b, rw, rb, tm):
    t = o.shape[0]
    const = lambda i: (0, 0)
    return pl.pallas_call(
        functools.partial(_post_mix_kernel, tm=tm),
        grid=(t // tm,),
        in_specs=[
            pl.BlockSpec((tm, D_MODEL), lambda i: (i, 0)),
            pl.BlockSpec((tm, D_MODEL), lambda i: (i, 0)),
            pl.BlockSpec((D_MODEL, D_MODEL), const),
            pl.BlockSpec((1, D_MODEL), const),
            pl.BlockSpec((1, D_MODEL), const),
            pl.BlockSpec((D_MODEL, LANES), const),
            pl.BlockSpec((1, LANES), const),
        ],
        out_specs=[
            pl.BlockSpec((tm, D_MODEL), lambda i: (i, 0)),
            pl.BlockSpec((tm * PK_ROWS, LANES), lambda i: (i, 0)),
            pl.BlockSpec((tm, LANES), lambda i: (i, 0)),
            pl.BlockSpec((tm, LANES), lambda i: (i, 0)),
            pl.BlockSpec((1, LANES), const),
        ],
        out_shape=[
            jax.ShapeDtypeStruct((t, D_MODEL), F32),
            jax.ShapeDtypeStruct((t * PK_ROWS, LANES), jnp.uint32),
            jax.ShapeDtypeStruct((t, LANES), jnp.int32),
            jax.ShapeDtypeStruct((t, LANES), F32),
            jax.ShapeDtypeStruct((1, LANES), F32),
        ],
        scratch_shapes=[pltpu.VMEM((tm, tm), MXU_DTYPE), pltpu.VMEM((1, LANES), F32)],
        compiler_params=pltpu.CompilerParams(
            dimension_semantics=("arbitrary",), vmem_limit_bytes=VMEM_LIMIT),
        name="post_mix_router",
    )(o, h0, wo, g, b, rw, rb)


def _dispatch_kernel(pad_row_ref, pad_n_ref, dest_ref, h1p_ref, w1_ref, w2_ref,
                     xs_hbm, w1c_ref, w2c_ref, zero_ref, sem, *, tm):
    i = pl.program_id(0)

    def tile(ref, t):
        return ref.at[pl.ds(pl.multiple_of(t * PK_ROWS, PK_ROWS), PK_ROWS), :]

    def body(it, c):
        for u in range(DMA_UNROLL):
            r = it * DMA_UNROLL + u
            for k in range(TOP_K):
                pltpu.make_async_copy(tile(h1p_ref, r), tile(xs_hbm, dest_ref[0, 0, r * TOP_K + k]),
                                      sem.at[0]).start(priority=k % 2)
        return c
    lax.fori_loop(0, tm // DMA_UNROLL, body, 0)
    w1c_ref[...] = _mx(w1_ref[...])
    w2c_ref[...] = _mx(w2_ref[...])
    for _ in range(TOP_K):
        pltpu.make_async_copy(h1p_ref, xs_hbm.at[pl.ds(0, tm * PK_ROWS), :], sem.at[0]).wait()

    @pl.when(i == pl.num_programs(0) - 1)
    def _():
        zero_ref[...] = jnp.zeros_like(zero_ref)
        zero = zero_ref.at[pl.ds(0, PK_ROWS), :]

        def per_expert(e, c):
            def start(r, c2):
                pltpu.make_async_copy(zero, tile(xs_hbm, pad_row_ref[e] + r), sem.at[1]).start()
                return c2
            lax.fori_loop(0, pad_n_ref[e], start, 0)

            def wait(r, c2):
                pltpu.make_async_copy(zero, tile(xs_hbm, 0), sem.at[1]).wait()
                return c2
            lax.fori_loop(0, pad_n_ref[e], wait, 0)
            return c
        lax.fori_loop(0, pad_row_ref.shape[0], per_expert, 0)


def _dispatch_call(pad_row, pad_n, dest, h1p, w1, w2, n_slots, n_steps):
    t = h1p.shape[0] // PK_ROWS
    tm = t // n_steps
    r1, r2 = w1.shape[0] // n_steps, w2.shape[0] // n_steps
    step = lambda i, pr, pn: (i, 0)
    grid_spec = pltpu.PrefetchScalarGridSpec(
        num_scalar_prefetch=2,
        grid=(n_steps,),
        in_specs=[
            pl.BlockSpec((1, 1, tm * TOP_K), lambda i, pr, pn: (i, 0, 0), memory_space=pltpu.SMEM),
            pl.BlockSpec((tm * PK_ROWS, LANES), step),
            pl.BlockSpec((r1, w1.shape[1]), step),
            pl.BlockSpec((r2, w2.shape[1]), step),
        ],
        out_specs=[
            pl.BlockSpec(memory_space=pl.ANY),
            pl.BlockSpec((r1, w1.shape[1]), step),
            pl.BlockSpec((r2, w2.shape[1]), step),
        ],
        scratch_shapes=[pltpu.VMEM((F32_ROWS, LANES), jnp.uint32), pltpu.SemaphoreType.DMA((2,))],
    )
    return pl.pallas_call(
        functools.partial(_dispatch_kernel, tm=tm),
        grid_spec=grid_spec,
        out_shape=[
            jax.ShapeDtypeStruct((n_slots * PK_ROWS, LANES), jnp.uint32),
            jax.ShapeDtypeStruct(w1.shape, MXU_DTYPE),
            jax.ShapeDtypeStruct(w2.shape, MXU_DTYPE),
        ],
        compiler_params=pltpu.CompilerParams(
            dimension_semantics=("arbitrary",), vmem_limit_bytes=VMEM_LIMIT),
        name="dispatch",
    )(pad_row, pad_n, dest.reshape(n_steps, 1, tm * TOP_K), h1p, w1, w2)


def _expert_kernel(be_ref, nr_ref, x_ref, w1_ref, b1_ref, w2_ref, b2_ref, y_ref):
    @pl.when(pl.program_id(0) >= nr_ref[0])
    def _():
        y_ref[...] = jnp.zeros_like(y_ref)

    @pl.when(pl.program_id(0) < nr_ref[0])
    def _():
        x = _mx(_unpack_pairs(_load_token_tiles(x_ref, MOE_BLOCK, PK_ROWS)))
        hh = _dot(x, w1_ref[0]) + b1_ref[0]
        x_glu = jnp.minimum(hh[:, :D_FF], SWIGLU_LIMIT)
        x_lin = jnp.clip(hh[:, D_FF:], -SWIGLU_LIMIT, SWIGLU_LIMIT)
        act = x_glu * (1.0 / (1.0 + jnp.exp(-SWIGLU_ALPHA * x_glu))) * (x_lin + 1.0)
        _store_token_tiles(y_ref, _pack_pairs(_dot(_mx(act), w2_ref[0]) + b2_ref[0]))


def _expert_call(block_expert, n_real, xs, w1, b1, w2, b2):
    n_blocks = block_expert.shape[0]
    blk = lambda g, be, nr: (jnp.minimum(g, nr[0] - 1), 0)
    ex = lambda g, be, nr: (be[g], 0, 0)
    grid_spec = pltpu.PrefetchScalarGridSpec(
        num_scalar_prefetch=2,
        grid=(n_blocks,),
        in_specs=[
            pl.BlockSpec((MOE_BLOCK * PK_ROWS, LANES), blk),
            pl.BlockSpec((1, D_MODEL, 2 * D_FF), ex),
            pl.BlockSpec((1, 1, 2 * D_FF), ex),
            pl.BlockSpec((1, D_FF, D_MODEL), ex),
            pl.BlockSpec((1, 1, D_MODEL), ex),
        ],
        out_specs=pl.BlockSpec((MOE_BLOCK * PK_ROWS, LANES), lambda g, be, nr: (g, 0)),
    )
    return pl.pallas_call(
        _expert_kernel,
        grid_spec=grid_spec,
        out_shape=jax.ShapeDtypeStruct((n_blocks * MOE_BLOCK * PK_ROWS, LANES), jnp.uint32),
        compiler_params=pltpu.CompilerParams(
            dimension_semantics=("arbitrary",), vmem_limit_bytes=VMEM_LIMIT),
        name="moe_experts",
    )(block_expert, n_real, xs, w1, b1, w2, b2)


def _gather_combine_kernel(dest0_ref, destn_ref, gate_ref, h1_ref, g_ref, b_ref, ys_hbm,
                           out_ref, ybuf, sem, *, tm):
    i = pl.program_id(0)
    slot = i % 2
    rows = tm * PK_ROWS

    def tile(ref, t):
        return ref.at[pl.ds(pl.multiple_of(t * PK_ROWS, PK_ROWS), PK_ROWS), :]

    def gather(tbl, s):
        def body(it, c):
            for u in range(DMA_UNROLL):
                r = it * DMA_UNROLL + u
                for k in range(TOP_K):
                    pltpu.make_async_copy(tile(ys_hbm, tbl[0, 0, r * TOP_K + k]),
                                          tile(ybuf, (s * TOP_K + k) * tm + r), sem.at[s]).start(priority=k % 2)
            return c
        lax.fori_loop(0, tm // DMA_UNROLL, body, 0)

    @pl.when(i == 0)
    def _():
        gather(dest0_ref, 0)

    @pl.when(i + 1 < pl.num_programs(0))
    def _():
        gather(destn_ref, 1 - slot)

    for k in range(TOP_K):
        plane = ybuf.at[pl.ds(pl.multiple_of((slot * TOP_K + k) * rows, rows), rows), :]
        pltpu.make_async_copy(ys_hbm.at[pl.ds(0, rows), :], plane, sem.at[slot]).wait()

    gates = gate_ref[...]
    acc = DEEPNORM_ALPHA * h1_ref[...]
    for k in range(TOP_K):
        plane = ybuf.at[pl.ds(pl.multiple_of((slot * TOP_K + k) * rows, rows), rows), :]
        acc = acc + gates[:, k:k + 1] * _unpack_pairs(_load_token_tiles(plane, tm, PK_ROWS))
    out_ref[...] = _layer_norm(acc, g_ref[...], b_ref[...])


def _gather_combine_call(dest, gates, h1, g, b, ys, tm):
    t = h1.shape[0]
    nt = t // tm
    const = lambda i: (0, 0)
    smem = functools.partial(pl.BlockSpec, memory_space=pltpu.SMEM)
    dest = dest.reshape(nt, 1, tm * TOP_K)
    dest_next = jnp.concatenate([dest[1:], dest[:1]], axis=0)
    return pl.pallas_call(
        functools.partial(_gather_combine_kernel, tm=tm),
        grid=(nt,),
        in_specs=[
            smem((1, 1, tm * TOP_K), lambda i: (0, 0, 0)),
            smem((1, 1, tm * TOP_K), lambda i: (i, 0, 0)),
            pl.BlockSpec((tm, LANES), lambda i: (i, 0)),
            pl.BlockSpec((tm, D_MODEL), lambda i: (i, 0)),
            pl.BlockSpec((1, D_MODEL), const),
            pl.BlockSpec((1, D_MODEL), const),
            pl.BlockSpec(memory_space=pl.ANY),
        ],
        out_specs=pl.BlockSpec((tm, D_MODEL), lambda i: (i, 0)),
        out_shape=jax.ShapeDtypeStruct((t, D_MODEL), F32),
        scratch_shapes=[pltpu.VMEM((2 * TOP_K * tm * PK_ROWS, LANES), jnp.uint32),
                        pltpu.SemaphoreType.DMA((2,))],
        compiler_params=pltpu.CompilerParams(
            dimension_semantics=("arbitrary",), vmem_limit_bytes=VMEM_LIMIT),
        name="gather_combine_ln2",
    )(dest, dest_next, gates, h1, g, b, ys)


def _relayout_w_in(w):
    sizes = (256, 256, 512, 512, GATE_RANK, 256, 256, 512, 512)
    offs = np.concatenate([[0], np.cumsum(sizes)])
    gq, gk, gv, gg, glr, rq, rk, rv, rg = [w[:, offs[i]:offs[i + 1]] for i in range(9)]
    perm = np.zeros((256,), np.int32)
    for p in range(2):
        for l in range(LANES):
            part, hh, f = l // 64, (l % 64) // 32, l % 32
            perm[p * LANES + l] = (2 * p + hh) * DK + part * (DK // 2) + f
    glr = jnp.pad(glr, ((0, 0), (0, LANES - GATE_RANK)))
    return jnp.concatenate([gq, gk, gv, gg, glr, rq[:, perm], rk[:, perm], rv, rg], axis=1)


def kernel(x, positions, ln_in_g, ln_in_b, w_in, gla_gate_w, gla_gate_b, gla_norm_g, ret_norm_g,
           ret_norm_b, w_out, ln1_g, ln1_b, router_w, router_b, moe_w1, moe_b1, moe_w2, moe_b2,
           ln2_g, ln2_b):
    bsz, seq, d = x.shape
    assert d == D_MODEL and seq % GROUP == 0 and w_in.shape[0] == DEPTH == 1
    t = bsz * seq
    ts = 1024 if seq % 1024 == 0 else GROUP
    tm = min(512, t)
    assert seq % ts == 0 and t % tm == 0
    row = lambda v: v.reshape(1, -1).astype(F32)

    w_p = _relayout_w_in(w_in[0]).astype(MXU_DTYPE)
    h0, proj = _ln_proj_call(x.reshape(t, d), row(ln_in_g), row(ln_in_b), w_p, tm)

    gw = jnp.pad(gla_gate_w[0], ((0, LANES - GATE_RANK), (0, 0))).astype(MXU_DTYPE)
    half = DK // 2
    inv_freq = 1.0 / (ROPE_BASE ** np.linspace(0.0, 1.0, half, dtype=np.float32))
    invf = jnp.asarray(np.tile(inv_freq, LANES // half).reshape(1, LANES), F32)
    sgn = jnp.asarray(np.where(np.arange(LANES) < LANES // 2, -1.0, 1.0).reshape(1, LANES), F32)
    step_ang = np.arange(GROUP, dtype=np.float64)[:, None] * np.tile(inv_freq, LANES // half)[None, :]
    cd = jnp.asarray(np.cos(step_ang), F32)
    sd = jnp.asarray(np.sin(step_ang), F32)
    o = _mixer_call(proj, positions.reshape(t, 1), gw, row(gla_gate_b[0]), row(gla_norm_g[0]),
                    row(ret_norm_g[0]), row(ret_norm_b[0]), invf, sgn, cd, sd, bsz, seq, ts)

    rw = jnp.pad(router_w[0], ((0, 0), (0, LANES - N_EXPERTS))).astype(MXU_DTYPE)
    rb = jnp.pad(router_b[0], (0, LANES - N_EXPERTS)).reshape(1, LANES).astype(F32)
    h1, h1t, r_i, r_g, cnt = _post_mix_call(o, h0, w_out[0].astype(MXU_DTYPE), row(ln1_g[0]),
                                       row(ln1_b[0]), rw, rb, tm)

    e = r_i[:, :TOP_K]
    rank = r_i[:, TOP_K:2 * TOP_K]
    counts = cnt[0, :N_EXPERTS].astype(jnp.int32)
    padded = ((counts + MOE_BLOCK - 1) // MOE_BLOCK) * MOE_BLOCK
    pad_end = jnp.cumsum(padded)
    pad_start = pad_end - padded
    tk = t * TOP_K
    n_blocks = (tk + MOE_BLOCK - 1) // MOE_BLOCK + N_EXPERTS
    i32 = jnp.int32
    ex = jnp.arange(N_EXPERTS, dtype=i32)
    dest = jnp.sum(jnp.where(e[..., None] == ex, pad_start, 0), -1) + rank
    block_start = jnp.arange(n_blocks, dtype=i32) * MOE_BLOCK
    block_expert = jnp.minimum(jnp.sum(block_start[:, None] >= pad_end[None, :], -1),
                               N_EXPERTS - 1).astype(i32)
    n_real = (pad_end[-1:] // MOE_BLOCK).astype(i32)

    n_slots = n_blocks * MOE_BLOCK
    hole_row = jnp.concatenate([pad_start + counts, pad_end[-1:]]).astype(i32)
    hole_n = jnp.concatenate([padded - counts, n_slots - pad_end[-1:]]).astype(i32)
    xs, w1c, w2c = _dispatch_call(hole_row, hole_n, dest, h1t,
                                  moe_w1[0].reshape(N_EXPERTS * D_MODEL, 2 * D_FF),
                                  moe_w2[0].reshape(N_EXPERTS * D_FF, D_MODEL), n_slots, DISPATCH_STEPS)
    ys = _expert_call(block_expert, n_real, xs,
                      w1c.reshape(N_EXPERTS, D_MODEL, 2 * D_FF), moe_b1[0].reshape(N_EXPERTS, 1, 2 * D_FF),
                      w2c.reshape(N_EXPERTS, D_FF, D_MODEL), moe_b2[0].reshape(N_EXPERTS, 1, D_MODEL))

    out = _gather_combine_call(dest, r_g, h1, row(ln2_g[0]), row(ln2_b[0]), ys, min(256, t))
    return out.reshape(bsz, seq, d)
```

```python
import functools
import math

import numpy as np
import jax
import jax.numpy as jnp
from jax import lax
from jax.experimental import pallas as pl
from jax.experimental.pallas import tpu as pltpu

F32 = jnp.float32
MXU_DTYPE = jnp.bfloat16

D_MODEL = 1024
CHUNK = 64
GC = 4
GROUP = GC * CHUNK
HEADS = 4
DK = 64
DV = 128
GATE_RANK = 16
GATE_NORM = 16.0
ROPE_BASE = 10000.0
N_EXPERTS = 32
TOP_K = 4
D_FF = 1024
SWIGLU_ALPHA = 1.702
SWIGLU_LIMIT = 7.0
MOE_BLOCK = 256
DMA_UNROLL = 8
DISPATCH_STEPS = 64
LN_EPS = 1e-5
DEPTH = 1
DEEPNORM_ALPHA = (2.0 * DEPTH) ** 0.25

LANES = 128
VMEM_LIMIT = 56 * 1024 * 1024

C_GQ, C_GK, C_GV, C_GG, C_GLR = 0, 256, 512, 1024, 1536
C_RQ, C_RK, C_RV, C_RG = 1664, 1920, 2176, 2688
PROJ_W = 3200

LOG_GAMMA = [math.log1p(-(2.0 ** (-5.0 - h))) for h in range(HEADS)]


def _dot(a, b):
    return jnp.dot(a, b, preferred_element_type=F32)


def _dot_nt(a, b):
    return lax.dot_general(a, b, (((1,), (1,)), ((), ())), preferred_element_type=F32)


def _dot_tn(a, b):
    return lax.dot_general(a, b, (((0,), (0,)), ((), ())), preferred_element_type=F32)


def _mx(a):
    return a.astype(MXU_DTYPE)


def _layer_norm(x, g, b):
    mu = jnp.mean(x, -1, keepdims=True)
    xc = x - mu
    var = jnp.mean(xc * xc, -1, keepdims=True)
    return xc * lax.rsqrt(var + LN_EPS) * g + b


def _silu(x):
    return x * (1.0 / (1.0 + jnp.exp(-x)))


F32_ROWS = D_MODEL // LANES
PK_ROWS = D_MODEL // (2 * LANES)


def _load_token_tiles(ref, n, r):
    return jnp.concatenate([ref[pl.ds(j, n, stride=r), :] for j in range(r)], axis=1)


def _store_token_tiles(ref, rows):
    n, r = rows.shape[0], rows.shape[1] // LANES
    for j in range(r):
        ref[pl.ds(j, n, stride=r), :] = rows[:, j * LANES:(j + 1) * LANES]


def _pack_pairs(x):
    h = x.shape[1] // 2
    r = x.astype(jnp.bfloat16).astype(F32)
    lo = lax.bitcast_convert_type(r[:, :h], jnp.uint32) >> 16
    hi = lax.bitcast_convert_type(r[:, h:], jnp.uint32) & jnp.uint32(0xFFFF0000)
    return lo | hi


def _unpack_pairs(u):
    lo = lax.bitcast_convert_type(u << 16, F32)
    hi = lax.bitcast_convert_type(u & jnp.uint32(0xFFFF0000), F32)
    return jnp.concatenate([lo, hi], axis=1)


def _ln_proj_kernel(x_ref, g_ref, b_ref, w_ref, h_ref, p_ref):
    h = _layer_norm(x_ref[...], g_ref[...], b_ref[...])
    h_ref[...] = h
    p_ref[...] = _dot(_mx(h), w_ref[...])


def _ln_proj_call(x2, g, b, w, tm):
    t = x2.shape[0]
    return pl.pallas_call(
        _ln_proj_kernel,
        grid=(t // tm,),
        in_specs=[
            pl.BlockSpec((tm, D_MODEL), lambda i: (i, 0)),
            pl.BlockSpec((1, D_MODEL), lambda i: (0, 0)),
            pl.BlockSpec((1, D_MODEL), lambda i: (0, 0)),
            pl.BlockSpec((D_MODEL, PROJ_W), lambda i: (0, 0)),
        ],
        out_specs=[
            pl.BlockSpec((tm, D_MODEL), lambda i: (i, 0)),
            pl.BlockSpec((tm, PROJ_W), lambda i: (i, 0)),
        ],
        out_shape=[
            jax.ShapeDtypeStruct((t, D_MODEL), F32),
            jax.ShapeDtypeStruct((t, PROJ_W), F32),
        ],
        compiler_params=pltpu.CompilerParams(
            dimension_semantics=("arbitrary",), vmem_limit_bytes=VMEM_LIMIT),
        name="ln_proj",
    )(x2, g, b, w)


def _mixer_kernel(p_ref, pos_ref, gw_ref, gb_ref, gn_ref, rng_ref, rnb_ref, invf_ref, sgn_ref,
                  cd_ref, sd_ref, o_ref, st_ref, *, n_groups):
    @pl.when(pl.program_id(1) == 0)
    def _():
        st_ref[...] = jnp.zeros_like(st_ref)

    rr = lax.broadcasted_iota(jnp.int32, (GROUP, GROUP), 0)
    cc = lax.broadcasted_iota(jnp.int32, (GROUP, GROUP), 1)
    same = (rr >> 6) == (cc >> 6)
    lower = same & (rr >= cc)
    upper = same & (rr < cc)
    tri = lower.astype(MXU_DTYPE)
    dist = jnp.abs(rr - cc).astype(F32)
    rowi = lax.broadcasted_iota(jnp.int32, (GROUP, LANES), 0)
    rin = (rowi & (CHUNK - 1)).astype(F32)
    in_chunk = [(rowi >> 6) == c for c in range(GC)]
    lane = lax.broadcasted_iota(jnp.int32, (GROUP, LANES), 1)
    gla_half = [((lane >> 6) & 1) == i for i in range(2)]
    ret_half = [((lane >> 5) & 1) == i for i in range(2)]
    ret_d = [jnp.where(same, jnp.exp(LOG_GAMMA[h] * dist), 0.0) for h in range(HEADS)]
    ret_eb = [jnp.exp(LOG_GAMMA[h] * (rin + 1.0)) for h in range(HEADS)]
    ret_ek = [jnp.exp(LOG_GAMMA[h] * (CHUNK - 1.0 - rin)) for h in range(HEADS)]
    ret_dec = [math.exp(LOG_GAMMA[h] * CHUNK) for h in range(HEADS)]

    def block_diag(x):
        return _mx(jnp.concatenate([jnp.where(in_chunk[c], x, 0.0) for c in range(GC)], axis=1))

    def recur(hs, q_m, kv_t, decay):
        st = st_ref[hs]
        parts = []
        for c in range(GC):
            parts.append(_dot_nt(q_m[c * CHUNK:(c + 1) * CHUNK], _mx(st)))
            st = st * decay(c) + kv_t[:, c * LANES:(c + 1) * LANES]
        st_ref[hs] = st
        return jnp.concatenate(parts, axis=0)

    def group_body(g, carry):
        rows = pl.ds(pl.multiple_of(g * GROUP, GROUP), GROUP)

        z = _dot(_mx(p_ref[rows, C_GLR:C_GLR + LANES]), gw_ref[...]) + gb_ref[...]
        la = (jnp.minimum(z, 0.0) - jnp.log1p(jnp.exp(-jnp.abs(z)))) * (1.0 / GATE_NORM)
        la_hi = _mx(la)
        r1 = la - la_hi.astype(F32)
        la_mid = _mx(r1)
        la_lo = _mx(r1 - la_mid.astype(F32))
        b = _dot(tri, la_hi) + _dot(tri, la_mid) + _dot(tri, la_lo)
        b_ends = [b[(c + 1) * CHUNK - 1:(c + 1) * CHUNK, :] for c in range(GC)]
        b_last = jnp.concatenate([jnp.broadcast_to(e, (CHUNK, 256)) for e in b_ends], axis=0)
        decs = [jnp.exp(e) for e in b_ends]
        eb = jnp.exp(b)
        enb = jnp.exp(-b)
        ekv = jnp.exp(b_last - b)
        q = p_ref[rows, C_GQ:C_GQ + 256] * (DK ** -0.5)
        k = p_ref[rows, C_GK:C_GK + 256]
        qe, qn = q * eb, q * enb
        ke, kn, kk = k * eb, k * enb, k * ekv
        for h in range(HEADS):
            p, half = h // 2, h % 2
            ls = slice(p * LANES, (p + 1) * LANES)
            m = gla_half[half]
            qe_m = _mx(jnp.where(m, qe[:, ls], 0.0))
            qn_m = _mx(jnp.where(m, qn[:, ls], 0.0))
            s_lo = _dot_nt(qe_m, _mx(kn[:, ls]))
            s_up = _dot_nt(qn_m, _mx(ke[:, ls]))
            sc = jnp.where(lower, s_lo, jnp.where(upper, s_up, 0.0))
            v = _mx(p_ref[rows, C_GV + h * DV:C_GV + (h + 1) * DV])
            kv_t = _dot_tn(v, block_diag(kk[:, ls]))
            o = _dot(_mx(sc), v) + recur(h, qe_m, kv_t, lambda c: decs[c][:, ls])
            o = o * lax.rsqrt(jnp.mean(o * o, -1, keepdims=True) + LN_EPS) * gn_ref[...]
            o = o * _silu(p_ref[rows, C_GG + h * DV:C_GG + (h + 1) * DV])
            o_ref[rows, h * DV:(h + 1) * DV] = o.astype(o_ref.dtype)

        base = pos_ref[pl.ds(pl.multiple_of(g * GROUP, GROUP), 1), :].astype(F32) * invf_ref[...]
        c0, s0 = jnp.cos(base), jnp.sin(base)
        cs = c0 * cd_ref[...] - s0 * sd_ref[...]
        sn = (s0 * sgn_ref[...]) * cd_ref[...] + (c0 * sgn_ref[...]) * sd_ref[...]
        for p in range(2):
            tq = p_ref[rows, C_RQ + p * LANES:C_RQ + (p + 1) * LANES]
            tk = p_ref[rows, C_RK + p * LANES:C_RK + (p + 1) * LANES] * (DK ** -0.5)
            rq = tq * cs + pltpu.roll(tq, LANES // 2, 1) * sn
            rk = tk * cs + pltpu.roll(tk, LANES // 2, 1) * sn
            rk_m = _mx(rk)
            for half in range(2):
                h = 2 * p + half
                q_m = _mx(jnp.where(ret_half[half], rq, 0.0))
                s = _dot_nt(q_m, rk_m) * ret_d[h]
                v = _mx(p_ref[rows, C_RV + h * DV:C_RV + (h + 1) * DV])
                kv_t = _dot_tn(v, block_diag(rk * ret_ek[h]))
                o = _dot(_mx(s), v) + ret_eb[h] * recur(HEADS + h, q_m, kv_t, lambda c: ret_dec[h])
                mu = jnp.mean(o, -1, keepdims=True)
                oc = o - mu
                var = jnp.mean(oc * oc, -1, keepdims=True)
                o = oc * lax.rsqrt(var + LN_EPS) * rng_ref[:, h * DV:(h + 1) * DV] \
                    + rnb_ref[:, h * DV:(h + 1) * DV]
                o = o * _silu(p_ref[rows, C_RG + h * DV:C_RG + (h + 1) * DV])
                o_ref[rows, (HEADS + h) * DV:(HEADS + h + 1) * DV] = o.astype(o_ref.dtype)
        return carry

    lax.fori_loop(0, n_groups, group_body, 0)


def _mixer_call(proj, pos, gw, gb, gn, rng, rnb, invf, sgn, cd, sd, bsz, seq, ts):
    t = bsz * seq
    nst = seq // ts
    const = lambda b, s: (0, 0)
    return pl.pallas_call(
        functools.partial(_mixer_kernel, n_groups=ts // GROUP),
        grid=(bsz, nst),
        in_specs=[
            pl.BlockSpec((ts, PROJ_W), lambda b, s: (b * nst + s, 0)),
            pl.BlockSpec((ts, 1), lambda b, s: (b * nst + s, 0)),
            pl.BlockSpec((LANES, 256), const),
            pl.BlockSpec((1, 256), const),
            pl.BlockSpec((1, DV), const),
            pl.BlockSpec((1, HEADS * DV), const),
            pl.BlockSpec((1, HEADS * DV), const),
            pl.BlockSpec((1, LANES), const),
            pl.BlockSpec((1, LANES), const),
            pl.BlockSpec((GROUP, LANES), const),
            pl.BlockSpec((GROUP, LANES), const),
        ],
        out_specs=pl.BlockSpec((ts, D_MODEL), lambda b, s: (b * nst + s, 0)),
        out_shape=jax.ShapeDtypeStruct((t, D_MODEL), MXU_DTYPE),
        scratch_shapes=[pltpu.VMEM((2 * HEADS, DV, LANES), F32)],
        compiler_params=pltpu.CompilerParams(
            dimension_semantics=("arbitrary", "arbitrary"), vmem_limit_bytes=VMEM_LIMIT),
        name="mixers",
    )(proj, pos, gw, gb, gn, rng, rnb, invf, sgn, cd, sd)


def _post_mix_kernel(o_ref, h0_ref, wo_ref, g_ref, b_ref, rw_ref, rb_ref,
                     h1_ref, h1t_ref, ri_ref, rg_ref, cnt_ref, tril_ref, carry_ref, *, tm):
    @pl.when(pl.program_id(0) == 0)
    def _():
        r = lax.broadcasted_iota(jnp.int32, (tm, tm), 0)
        c = lax.broadcasted_iota(jnp.int32, (tm, tm), 1)
        tril_ref[...] = (r > c).astype(MXU_DTYPE)
        carry_ref[...] = jnp.zeros_like(carry_ref)

    mix = _dot(o_ref[...], wo_ref[...])
    h1 = _layer_norm(DEEPNORM_ALPHA * h0_ref[...] + mix, g_ref[...], b_ref[...])
    h1_ref[...] = h1
    _store_token_tiles(h1t_ref, _pack_pairs(h1))

    lane = lax.broadcasted_iota(jnp.int32, (tm, LANES), 1)
    lane_f = lane.astype(F32)
    logits = _dot(_mx(h1), rw_ref[...]) + rb_ref[...]
    l = jnp.where(lane < N_EXPERTS, logits, -jnp.inf)
    vals, idxs = [], []
    for _ in range(TOP_K):
        m = jnp.max(l, -1, keepdims=True)
        i = jnp.min(jnp.where(l == m, lane_f, float(LANES)), -1, keepdims=True)
        vals.append(m)
        idxs.append(i)
        l = jnp.where(lane_f == i, -jnp.inf, l)
    exps = [jnp.exp(v - vals[0]) for v in vals]
    inv = 1.0 / (exps[0] + exps[1] + exps[2] + exps[3])

    onehot = jnp.zeros((tm, LANES), F32)
    for i in idxs:
        onehot = onehot + (lane_f == i).astype(F32)
    before = _dot(tril_ref[...], _mx(onehot)) + carry_ref[...]
    packed_i = jnp.zeros((tm, LANES), F32)
    packed_g = jnp.zeros((tm, LANES), F32)
    for k in range(TOP_K):
        rank = jnp.sum(jnp.where(lane_f == idxs[k], before, 0.0), -1, keepdims=True)
        packed_i = jnp.where(lane == k, idxs[k], packed_i)
        packed_i = jnp.where(lane == TOP_K + k, rank, packed_i)
        packed_g = jnp.where(lane == k, exps[k] * inv, packed_g)
    ri_ref[...] = jnp.transpose(packed_i)[:2 * TOP_K].astype(jnp.int32)
    rg_ref[...] = packed_g
    carry_ref[...] = carry_ref[...] + jnp.sum(onehot, 0, keepdims=True)
    cnt_ref[...] = carry_ref[...]


def _post_mix_call(o, h0, wo, g, b, rw, rb, tm):
    t = o.shape[0]
    const = lambda i: (0, 0)
    return pl.pallas_call(
        functools.partial(_post_mix_kernel, tm=tm),
        grid=(t // tm,),
        in_specs=[
            pl.BlockSpec((tm, D_MODEL), lambda i: (i, 0)),
            pl.BlockSpec((tm, D_MODEL), lambda i: (i, 0)),
            pl.BlockSpec((D_MODEL, D_MODEL), const),
            pl.BlockSpec((1, D_MODEL), const),
            pl.BlockSpec((1, D_MODEL), const),
            pl.BlockSpec((D_MODEL, LANES), const),
            pl.BlockSpec((1, LANES), const),
        ],
        out_specs=[
            pl.BlockSpec((tm, D_MODEL), lambda i: (i, 0)),
            pl.BlockSpec((tm * PK_ROWS, LANES), lambda i: (i, 0)),
            pl.BlockSpec((2 * TOP_K, tm), lambda i: (0, i)),
            pl.BlockSpec((tm, LANES), lambda i: (i, 0)),
            pl.BlockSpec((1, LANES), const),
        ],
        out_shape=[
            jax.ShapeDtypeStruct((t, D_MODEL), F32),
            jax.ShapeDtypeStruct((t * PK_ROWS, LANES), jnp.uint32),
            jax.ShapeDtypeStruct((2 * TOP_K, t), jnp.int32),
            jax.ShapeDtypeStruct((t, LANES), F32),
            jax.ShapeDtypeStruct((1, LANES), F32),
        ],
        scratch_shapes=[pltpu.VMEM((tm, tm), MXU_DTYPE), pltpu.VMEM((1, LANES), F32)],
        compiler_params=pltpu.CompilerParams(
            dimension_semantics=("arbitrary",), vmem_limit_bytes=VMEM_LIMIT),
        name="post_mix_router",
    )(o, h0, wo, g, b, rw, rb)


def _step_tables(dest, tm):
    k, t = dest.shape
    return dest.reshape(k, t // tm, tm).transpose(1, 0, 2)


def _dispatch_kernel(pad_row_ref, pad_n_ref, dest_ref, h1p_ref, w1_ref, w2_ref,
                     xs_hbm, w1c_ref, w2c_ref, stage_ref, zero_ref, sem, *, tm):
    i = pl.program_id(0)

    def tile(ref, t):
        return ref.at[pl.ds(pl.multiple_of(t * PK_ROWS, PK_ROWS), PK_ROWS), :]

    slot = i % 2
    rows = tm * PK_ROWS
    stage = stage_ref.at[pl.ds(pl.multiple_of(slot * rows, rows), rows), :]
    stage[...] = h1p_ref[...]

    def body(it, c):
        for u in range(DMA_UNROLL):
            r = it * DMA_UNROLL + u
            for k in range(TOP_K):
                pltpu.make_async_copy(tile(stage, r), tile(xs_hbm, dest_ref[0, k, r]),
                                      sem.at[slot]).start(priority=k % 2)
        return c
    lax.fori_loop(0, tm // DMA_UNROLL, body, 0)
    w1c_ref[...] = _mx(w1_ref[...])
    w2c_ref[...] = _mx(w2_ref[...])

    def drain(s):
        for _ in range(TOP_K):
            pltpu.make_async_copy(h1p_ref, xs_hbm.at[pl.ds(0, rows), :], sem.at[s]).wait()

    @pl.when(i > 0)
    def _():
        drain(1 - slot)

    @pl.when(i == pl.num_programs(0) - 1)
    def _():
        drain(slot)
        zero_ref[...] = jnp.zeros_like(zero_ref)
        zero = zero_ref.at[pl.ds(0, PK_ROWS), :]

        def per_expert(e, c):
            def start(r, c2):
                pltpu.make_async_copy(zero, tile(xs_hbm, pad_row_ref[e] + r), sem.at[2]).start()
                return c2
            lax.fori_loop(0, pad_n_ref[e], start, 0)

            def wait(r, c2):
                pltpu.make_async_copy(zero, tile(xs_hbm, 0), sem.at[2]).wait()
                return c2
            lax.fori_loop(0, pad_n_ref[e], wait, 0)
            return c
        lax.fori_loop(0, pad_row_ref.shape[0], per_expert, 0)


def _dispatch_call(pad_row, pad_n, dest, h1p, w1, w2, n_slots, n_steps):
    t = h1p.shape[0] // PK_ROWS
    tm = t // n_steps
    r1, r2 = w1.shape[0] // n_steps, w2.shape[0] // n_steps
    step = lambda i, pr, pn: (i, 0)
    grid_spec = pltpu.PrefetchScalarGridSpec(
        num_scalar_prefetch=2,
        grid=(n_steps,),
        in_specs=[
            pl.BlockSpec((1, TOP_K, tm), lambda i, pr, pn: (i, 0, 0), memory_space=pltpu.SMEM),
            pl.BlockSpec((tm * PK_ROWS, LANES), step),
            pl.BlockSpec((r1, w1.shape[1]), step),
            pl.BlockSpec((r2, w2.shape[1]), step),
        ],
        out_specs=[
            pl.BlockSpec(memory_space=pl.ANY),
            pl.BlockSpec((r1, w1.shape[1]), step),
            pl.BlockSpec((r2, w2.shape[1]), step),
        ],
        scratch_shapes=[pltpu.VMEM((2 * tm * PK_ROWS, LANES), jnp.uint32),
                        pltpu.VMEM((F32_ROWS, LANES), jnp.uint32), pltpu.SemaphoreType.DMA((3,))],
    )
    return pl.pallas_call(
        functools.partial(_dispatch_kernel, tm=tm),
        grid_spec=grid_spec,
        out_shape=[
            jax.ShapeDtypeStruct((n_slots * PK_ROWS, LANES), jnp.uint32),
            jax.ShapeDtypeStruct(w1.shape, MXU_DTYPE),
            jax.ShapeDtypeStruct(w2.shape, MXU_DTYPE),
        ],
        compiler_params=pltpu.CompilerParams(
            dimension_semantics=("arbitrary",), vmem_limit_bytes=VMEM_LIMIT),
        name="dispatch",
    )(pad_row, pad_n, _step_tables(dest, tm), h1p, w1, w2)


def _expert_kernel(be_ref, nr_ref, x_ref, w1_ref, b1_ref, w2_ref, b2_ref, y_ref):
    @pl.when(pl.program_id(0) >= nr_ref[0])
    def _():
        y_ref[...] = jnp.zeros_like(y_ref)

    @pl.when(pl.program_id(0) < nr_ref[0])
    def _():
        x = _mx(_unpack_pairs(_load_token_tiles(x_ref, MOE_BLOCK, PK_ROWS)))
        hh = _dot(x, w1_ref[0]) + b1_ref[0]
        x_glu = jnp.minimum(hh[:, :D_FF], SWIGLU_LIMIT)
        x_lin = jnp.clip(hh[:, D_FF:], -SWIGLU_LIMIT, SWIGLU_LIMIT)
        act = x_glu * (1.0 / (1.0 + jnp.exp(-SWIGLU_ALPHA * x_glu))) * (x_lin + 1.0)
        _store_token_tiles(y_ref, _pack_pairs(_dot(_mx(act), w2_ref[0]) + b2_ref[0]))


def _expert_call(block_expert, n_real, xs, w1, b1, w2, b2):
    n_blocks = block_expert.shape[0]
    blk = lambda g, be, nr: (jnp.minimum(g, nr[0] - 1), 0)
    ex = lambda g, be, nr: (be[g], 0, 0)
    grid_spec = pltpu.PrefetchScalarGridSpec(
        num_scalar_prefetch=2,
        grid=(n_blocks,),
        in_specs=[
            pl.BlockSpec((MOE_BLOCK * PK_ROWS, LANES), blk),
            pl.BlockSpec((1, D_MODEL, 2 * D_FF), ex),
            pl.BlockSpec((1, 1, 2 * D_FF), ex),
            pl.BlockSpec((1, D_FF, D_MODEL), ex),
            pl.BlockSpec((1, 1, D_MODEL), ex),
        ],
        out_specs=pl.BlockSpec((MOE_BLOCK * PK_ROWS, LANES), lambda g, be, nr: (g, 0)),
    )
    return pl.pallas_call(
        _expert_kernel,
        grid_spec=grid_spec,
        out_shape=jax.ShapeDtypeStruct((n_blocks * MOE_BLOCK * PK_ROWS, LANES), jnp.uint32),
        compiler_params=pltpu.CompilerParams(
            dimension_semantics=("arbitrary",), vmem_limit_bytes=VMEM_LIMIT),
        name="moe_experts",
    )(block_expert, n_real, xs, w1, b1, w2, b2)


def _gather_combine_kernel(dest0_ref, destn_ref, gate_ref, h1_ref, g_ref, b_ref, ys_hbm,
                           out_ref, ybuf, sem, *, tm):
    i = pl.program_id(0)
    slot = i % 2
    rows = tm * PK_ROWS

    def tile(ref, t):
        return ref.at[pl.ds(pl.multiple_of(t * PK_ROWS, PK_ROWS), PK_ROWS), :]

    def gather(tbl, s):
        def body(it, c):
            for u in range(DMA_UNROLL):
                r = it * DMA_UNROLL + u
                for k in range(TOP_K):
                    pltpu.make_async_copy(tile(ys_hbm, tbl[0, k, r]),
                                          tile(ybuf, (s * TOP_K + k) * tm + r), sem.at[s]).start(priority=k % 2)
            return c
        lax.fori_loop(0, tm // DMA_UNROLL, body, 0)

    @pl.when(i == 0)
    def _():
        gather(dest0_ref, 0)

    @pl.when(i + 1 < pl.num_programs(0))
    def _():
        gather(destn_ref, 1 - slot)

    for k in range(TOP_K):
        plane = ybuf.at[pl.ds(pl.multiple_of((slot * TOP_K + k) * rows, rows), rows), :]
        pltpu.make_async_copy(ys_hbm.at[pl.ds(0, rows), :], plane, sem.at[slot]).wait()

    gates = gate_ref[...]
    acc = DEEPNORM_ALPHA * h1_ref[...]
    for k in range(TOP_K):
        plane = ybuf.at[pl.ds(pl.multiple_of((slot * TOP_K + k) * rows, rows), rows), :]
        acc = acc + gates[:, k:k + 1] * _unpack_pairs(_load_token_tiles(plane, tm, PK_ROWS))
    out_ref[...] = _layer_norm(acc, g_ref[...], b_ref[...])


def _gather_combine_call(dest, gates, h1, g, b, ys, tm):
    t = h1.shape[0]
    nt = t // tm
    const = lambda i: (0, 0)
    smem = functools.partial(pl.BlockSpec, memory_space=pltpu.SMEM)
    dest = _step_tables(dest, tm)
    dest_next = jnp.concatenate([dest[1:], dest[:1]], axis=0)
    return pl.pallas_call(
        functools.partial(_gather_combine_kernel, tm=tm),
        grid=(nt,),
        in_specs=[
            smem((1, TOP_K, tm), lambda i: (0, 0, 0)),
            smem((1, TOP_K, tm), lambda i: (i, 0, 0)),
            pl.BlockSpec((tm, LANES), lambda i: (i, 0)),
            pl.BlockSpec((tm, D_MODEL), lambda i: (i, 0)),
            pl.BlockSpec((1, D_MODEL), const),
            pl.BlockSpec((1, D_MODEL), const),
            pl.BlockSpec(memory_space=pl.ANY),
        ],
        out_specs=pl.BlockSpec((tm, D_MODEL), lambda i: (i, 0)),
        out_shape=jax.ShapeDtypeStruct((t, D_MODEL), F32),
        scratch_shapes=[pltpu.VMEM((2 * TOP_K * tm * PK_ROWS, LANES), jnp.uint32),
                        pltpu.SemaphoreType.DMA((2,))],
        compiler_params=pltpu.CompilerParams(
            dimension_semantics=("arbitrary",), vmem_limit_bytes=VMEM_LIMIT),
        name="gather_combine_ln2",
    )(dest, dest_next, gates, h1, g, b, ys)


def _relayout_w_in(w):
    sizes = (256, 256, 512, 512, GATE_RANK, 256, 256, 512, 512)
    offs = np.concatenate([[0], np.cumsum(sizes)])
    gq, gk, gv, gg, glr, rq, rk, rv, rg = [w[:, offs[i]:offs[i + 1]] for i in range(9)]
    perm = np.zeros((256,), np.int32)
    for p in range(2):
        for l in range(LANES):
            part, hh, f = l // 64, (l % 64) // 32, l % 32
            perm[p * LANES + l] = (2 * p + hh) * DK + part * (DK // 2) + f
    glr = jnp.pad(glr, ((0, 0), (0, LANES - GATE_RANK)))
    return jnp.concatenate([gq, gk, gv, gg, glr, rq[:, perm], rk[:, perm], rv, rg], axis=1)


def kernel(x, positions, ln_in_g, ln_in_b, w_in, gla_gate_w, gla_gate_b, gla_norm_g, ret_norm_g,
           ret_norm_b, w_out, ln1_g, ln1_b, router_w, router_b, moe_w1, moe_b1, moe_w2, moe_b2,
           ln2_g, ln2_b):
    bsz, seq, d = x.shape
    assert d == D_MODEL and seq % GROUP == 0 and w_in.shape[0] == DEPTH == 1
    t = bsz * seq
    ts = 1024 if seq % 1024 == 0 else GROUP
    tm = min(512, t)
    assert seq % ts == 0 and t % tm == 0
    row = lambda v: v.reshape(1, -1).astype(F32)

    w_p = _relayout_w_in(w_in[0]).astype(MXU_DTYPE)
    h0, proj = _ln_proj_call(x.reshape(t, d), row(ln_in_g), row(ln_in_b), w_p, tm)

    gw = jnp.pad(gla_gate_w[0], ((0, LANES - GATE_RANK), (0, 0))).astype(MXU_DTYPE)
    half = DK // 2
    inv_freq = 1.0 / (ROPE_BASE ** np.linspace(0.0, 1.0, half, dtype=np.float32))
    invf = jnp.asarray(np.tile(inv_freq, LANES // half).reshape(1, LANES), F32)
    sgn = jnp.asarray(np.where(np.arange(LANES) < LANES // 2, -1.0, 1.0).reshape(1, LANES), F32)
    step_ang = np.arange(GROUP, dtype=np.float64)[:, None] * np.tile(inv_freq, LANES // half)[None, :]
    cd = jnp.asarray(np.cos(step_ang), F32)
    sd = jnp.asarray(np.sin(step_ang), F32)
    o = _mixer_call(proj, positions.reshape(t, 1), gw, row(gla_gate_b[0]), row(gla_norm_g[0]),
                    row(ret_norm_g[0]), row(ret_norm_b[0]), invf, sgn, cd, sd, bsz, seq, ts)

    rw = jnp.pad(router_w[0], ((0, 0), (0, LANES - N_EXPERTS))).astype(MXU_DTYPE)
    rb = jnp.pad(router_b[0], (0, LANES - N_EXPERTS)).reshape(1, LANES).astype(F32)
    h1, h1t, r_i, r_g, cnt = _post_mix_call(o, h0, w_out[0].astype(MXU_DTYPE), row(ln1_g[0]),
                                       row(ln1_b[0]), rw, rb, tm)

    e = r_i[:TOP_K]
    rank = r_i[TOP_K:]
    counts = cnt[0, :N_EXPERTS].astype(jnp.int32)
    padded = ((counts + MOE_BLOCK - 1) // MOE_BLOCK) * MOE_BLOCK
    pad_end = jnp.cumsum(padded)
    pad_start = pad_end - padded
    tk = t * TOP_K
    n_blocks = (tk + MOE_BLOCK - 1) // MOE_BLOCK + N_EXPERTS
    i32 = jnp.int32
    dest = rank
    for j in range(N_EXPERTS):
        dest = dest + jnp.where(e == j, pad_start[j], 0)
    block_start = jnp.arange(n_blocks, dtype=i32) * MOE_BLOCK
    block_expert = jnp.minimum(jnp.sum(block_start[:, None] >= pad_end[None, :], -1),
                               N_EXPERTS - 1).astype(i32)
    n_real = (pad_end[-1:] // MOE_BLOCK).astype(i32)

    n_slots = n_blocks * MOE_BLOCK
    hole_row = jnp.concatenate([pad_start + counts, pad_end[-1:]]).astype(i32)
    hole_n = jnp.concatenate([padded - counts, n_slots - pad_end[-1:]]).astype(i32)
    xs, w1c, w2c = _dispatch_call(hole_row, hole_n, dest, h1t,
                                  moe_w1[0].reshape(N_EXPERTS * D_MODEL, 2 * D_FF),
                                  moe_w2[0].reshape(N_EXPERTS * D_FF, D_MODEL), n_slots, DISPATCH_STEPS)
    ys = _expert_call(block_expert, n_real, xs,
                      w1c.reshape(N_EXPERTS, D_MODEL, 2 * D_FF), moe_b1[0].reshape(N_EXPERTS, 1, 2 * D_FF),
                      w2c.reshape(N_EXPERTS, D_FF, D_MODEL), moe_b2[0].reshape(N_EXPERTS, 1, D_MODEL))

    out = _gather_combine_call(dest, r_g, h1, row(ln2_g[0]), row(ln2_b[0]), ys, min(256, t))
    return out.reshape(bsz, seq, d)
```

```python
import functools
import math

import numpy as np
import jax
import jax.numpy as jnp
from jax import lax
from jax.experimental import pallas as pl
from jax.experimental.pallas import tpu as pltpu

F32 = jnp.float32
MXU_DTYPE = jnp.bfloat16

D_MODEL = 1024
CHUNK = 64
GC = 4
GROUP = GC * CHUNK
HEADS = 4
DK = 64
DV = 128
GATE_RANK = 16
GATE_NORM = 16.0
ROPE_BASE = 10000.0
N_EXPERTS = 32
TOP_K = 4
D_FF = 1024
SWIGLU_ALPHA = 1.702
SWIGLU_LIMIT = 7.0
MOE_BLOCK = 256
BLOCKS_PER_STEP = 2
DMA_UNROLL = 8
DISPATCH_STEPS = 64
LN_EPS = 1e-5
DEPTH = 1
DEEPNORM_ALPHA = (2.0 * DEPTH) ** 0.25

LANES = 128
VMEM_LIMIT = 56 * 1024 * 1024

C_GQ, C_GK, C_GV, C_GG, C_GLR = 0, 256, 512, 1024, 1536
C_RQ, C_RK, C_RV, C_RG = 1664, 1920, 2176, 2688
PROJ_W = 3200

LOG_GAMMA = [math.log1p(-(2.0 ** (-5.0 - h))) for h in range(HEADS)]


def _dot(a, b):
    return jnp.dot(a, b, preferred_element_type=F32)


def _dot_nt(a, b):
    return lax.dot_general(a, b, (((1,), (1,)), ((), ())), preferred_element_type=F32)


def _dot_tn(a, b):
    return lax.dot_general(a, b, (((0,), (0,)), ((), ())), preferred_element_type=F32)


def _mx(a):
    return a.astype(MXU_DTYPE)


def _layer_norm(x, g, b):
    mu = jnp.mean(x, -1, keepdims=True)
    xc = x - mu
    var = jnp.mean(xc * xc, -1, keepdims=True)
    return xc * lax.rsqrt(var + LN_EPS) * g + b


def _silu(x):
    return x * (1.0 / (1.0 + jnp.exp(-x)))


F32_ROWS = D_MODEL // LANES
PK_ROWS = D_MODEL // (2 * LANES)


def _load_token_tiles(ref, n, r):
    return jnp.concatenate([ref[pl.ds(j, n, stride=r), :] for j in range(r)], axis=1)


def _store_token_tiles(ref, rows):
    n, r = rows.shape[0], rows.shape[1] // LANES
    for j in range(r):
        ref[pl.ds(j, n, stride=r), :] = rows[:, j * LANES:(j + 1) * LANES]


def _pack_pairs(x):
    h = x.shape[1] // 2
    r = x.astype(jnp.bfloat16).astype(F32)
    lo = lax.bitcast_convert_type(r[:, :h], jnp.uint32) >> 16
    hi = lax.bitcast_convert_type(r[:, h:], jnp.uint32) & jnp.uint32(0xFFFF0000)
    return lo | hi


def _unpack_pairs(u):
    lo = lax.bitcast_convert_type(u << 16, F32)
    hi = lax.bitcast_convert_type(u & jnp.uint32(0xFFFF0000), F32)
    return jnp.concatenate([lo, hi], axis=1)


def _ln_proj_kernel(x_ref, g_ref, b_ref, w_ref, h_ref, p_ref):
    h = _layer_norm(x_ref[...], g_ref[...], b_ref[...])
    h_ref[...] = h
    p_ref[...] = _dot(_mx(h), w_ref[...])


def _ln_proj_call(x2, g, b, w, tm):
    t = x2.shape[0]
    return pl.pallas_call(
        _ln_proj_kernel,
        grid=(t // tm,),
        in_specs=[
            pl.BlockSpec((tm, D_MODEL), lambda i: (i, 0)),
            pl.BlockSpec((1, D_MODEL), lambda i: (0, 0)),
            pl.BlockSpec((1, D_MODEL), lambda i: (0, 0)),
            pl.BlockSpec((D_MODEL, PROJ_W), lambda i: (0, 0)),
        ],
        out_specs=[
            pl.BlockSpec((tm, D_MODEL), lambda i: (i, 0)),
            pl.BlockSpec((tm, PROJ_W), lambda i: (i, 0)),
        ],
        out_shape=[
            jax.ShapeDtypeStruct((t, D_MODEL), F32),
            jax.ShapeDtypeStruct((t, PROJ_W), F32),
        ],
        compiler_params=pltpu.CompilerParams(
            dimension_semantics=("arbitrary",), vmem_limit_bytes=VMEM_LIMIT),
        name="ln_proj",
    )(x2, g, b, w)


def _mixer_kernel(p_ref, pos_ref, gw_ref, gb_ref, gn_ref, rng_ref, rnb_ref, invf_ref, sgn_ref,
                  cd_ref, sd_ref, o_ref, st_ref, *, n_groups):
    @pl.when(pl.program_id(1) == 0)
    def _():
        st_ref[...] = jnp.zeros_like(st_ref)

    rr = lax.broadcasted_iota(jnp.int32, (GROUP, GROUP), 0)
    cc = lax.broadcasted_iota(jnp.int32, (GROUP, GROUP), 1)
    same = (rr >> 6) == (cc >> 6)
    lower = same & (rr >= cc)
    upper = same & (rr < cc)
    tri = lower.astype(MXU_DTYPE)
    dist = jnp.abs(rr - cc).astype(F32)
    rowi = lax.broadcasted_iota(jnp.int32, (GROUP, LANES), 0)
    rin = (rowi & (CHUNK - 1)).astype(F32)
    in_chunk = [(rowi >> 6) == c for c in range(GC)]
    lane = lax.broadcasted_iota(jnp.int32, (GROUP, LANES), 1)
    gla_half = [((lane >> 6) & 1) == i for i in range(2)]
    ret_half = [((lane >> 5) & 1) == i for i in range(2)]
    ret_d = [jnp.where(same, jnp.exp(LOG_GAMMA[h] * dist), 0.0) for h in range(HEADS)]
    ret_eb = [jnp.exp(LOG_GAMMA[h] * (rin + 1.0)) for h in range(HEADS)]
    ret_ek = [jnp.exp(LOG_GAMMA[h] * (CHUNK - 1.0 - rin)) for h in range(HEADS)]
    ret_dec = [math.exp(LOG_GAMMA[h] * CHUNK) for h in range(HEADS)]

    def block_diag(x):
        return _mx(jnp.concatenate([jnp.where(in_chunk[c], x, 0.0) for c in range(GC)], axis=1))

    def recur(hs, q_m, kv_t, decay):
        st = st_ref[hs]
        parts = []
        for c in range(GC):
            parts.append(_dot_nt(q_m[c * CHUNK:(c + 1) * CHUNK], _mx(st)))
            st = st * decay(c) + kv_t[:, c * LANES:(c + 1) * LANES]
        st_ref[hs] = st
        return jnp.concatenate(parts, axis=0)

    def group_body(g, carry):
        rows = pl.ds(pl.multiple_of(g * GROUP, GROUP), GROUP)

        z = _dot(_mx(p_ref[rows, C_GLR:C_GLR + LANES]), gw_ref[...]) + gb_ref[...]
        la = (jnp.minimum(z, 0.0) - jnp.log1p(jnp.exp(-jnp.abs(z)))) * (1.0 / GATE_NORM)
        la_hi = _mx(la)
        r1 = la - la_hi.astype(F32)
        la_mid = _mx(r1)
        la_lo = _mx(r1 - la_mid.astype(F32))
        b = _dot(tri, la_hi) + _dot(tri, la_mid) + _dot(tri, la_lo)
        b_ends = [b[(c + 1) * CHUNK - 1:(c + 1) * CHUNK, :] for c in range(GC)]
        b_last = jnp.concatenate([jnp.broadcast_to(e, (CHUNK, 256)) for e in b_ends], axis=0)
        decs = [jnp.exp(e) for e in b_ends]
        eb = jnp.exp(b)
        enb = jnp.exp(-b)
        ekv = jnp.exp(b_last - b)
        q = p_ref[rows, C_GQ:C_GQ + 256] * (DK ** -0.5)
        k = p_ref[rows, C_GK:C_GK + 256]
        qe, qn = q * eb, q * enb
        ke, kn, kk = k * eb, k * enb, k * ekv
        for h in range(HEADS):
            p, half = h // 2, h % 2
            ls = slice(p * LANES, (p + 1) * LANES)
            m = gla_half[half]
            qe_m = _mx(jnp.where(m, qe[:, ls], 0.0))
            qn_m = _mx(jnp.where(m, qn[:, ls], 0.0))
            s_lo = _dot_nt(qe_m, _mx(kn[:, ls]))
            s_up = _dot_nt(qn_m, _mx(ke[:, ls]))
            sc = jnp.where(lower, s_lo, jnp.where(upper, s_up, 0.0))
            v = _mx(p_ref[rows, C_GV + h * DV:C_GV + (h + 1) * DV])
            kv_t = _dot_tn(v, block_diag(kk[:, ls]))
            o = _dot(_mx(sc), v) + recur(h, qe_m, kv_t, lambda c: decs[c][:, ls])
            o = o * lax.rsqrt(jnp.mean(o * o, -1, keepdims=True) + LN_EPS) * gn_ref[...]
            o = o * _silu(p_ref[rows, C_GG + h * DV:C_GG + (h + 1) * DV])
            o_ref[rows, h * DV:(h + 1) * DV] = o.astype(o_ref.dtype)

        base = pos_ref[pl.ds(pl.multiple_of(g * GROUP, GROUP), 1), :].astype(F32) * invf_ref[...]
        c0, s0 = jnp.cos(base), jnp.sin(base)
        cs = c0 * cd_ref[...] - s0 * sd_ref[...]
        sn = (s0 * sgn_ref[...]) * cd_ref[...] + (c0 * sgn_ref[...]) * sd_ref[...]
        for p in range(2):
            tq = p_ref[rows, C_RQ + p * LANES:C_RQ + (p + 1) * LANES]
            tk = p_ref[rows, C_RK + p * LANES:C_RK + (p + 1) * LANES] * (DK ** -0.5)
            rq = tq * cs + pltpu.roll(tq, LANES // 2, 1) * sn
            rk = tk * cs + pltpu.roll(tk, LANES // 2, 1) * sn
            rk_m = _mx(rk)
            for half in range(2):
                h = 2 * p + half
                q_m = _mx(jnp.where(ret_half[half], rq, 0.0))
                s = _dot_nt(q_m, rk_m) * ret_d[h]
                v = _mx(p_ref[rows, C_RV + h * DV:C_RV + (h + 1) * DV])
                kv_t = _dot_tn(v, block_diag(rk * ret_ek[h]))
                o = _dot(_mx(s), v) + ret_eb[h] * recur(HEADS + h, q_m, kv_t, lambda c: ret_dec[h])
                mu = jnp.mean(o, -1, keepdims=True)
                oc = o - mu
                var = jnp.mean(oc * oc, -1, keepdims=True)
                o = oc * lax.rsqrt(var + LN_EPS) * rng_ref[:, h * DV:(h + 1) * DV] \
                    + rnb_ref[:, h * DV:(h + 1) * DV]
                o = o * _silu(p_ref[rows, C_RG + h * DV:C_RG + (h + 1) * DV])
                o_ref[rows, (HEADS + h) * DV:(HEADS + h + 1) * DV] = o.astype(o_ref.dtype)
        return carry

    lax.fori_loop(0, n_groups, group_body, 0)


def _mixer_call(proj, pos, gw, gb, gn, rng, rnb, invf, sgn, cd, sd, bsz, seq, ts):
    t = bsz * seq
    nst = seq // ts
    const = lambda b, s: (0, 0)
    return pl.pallas_call(
        functools.partial(_mixer_kernel, n_groups=ts // GROUP),
        grid=(bsz, nst),
        in_specs=[
            pl.BlockSpec((ts, PROJ_W), lambda b, s: (b * nst + s, 0)),
            pl.BlockSpec((ts, 1), lambda b, s: (b * nst + s, 0)),
            pl.BlockSpec((LANES, 256), const),
            pl.BlockSpec((1, 256), const),
            pl.BlockSpec((1, DV), const),
            pl.BlockSpec((1, HEADS * DV), const),
            pl.BlockSpec((1, HEADS * DV), const),
            pl.BlockSpec((1, LANES), const),
            pl.BlockSpec((1, LANES), const),
            pl.BlockSpec((GROUP, LANES), const),
            pl.BlockSpec((GROUP, LANES), const),
        ],
        out_specs=pl.BlockSpec((ts, D_MODEL), lambda b, s: (b * nst + s, 0)),
        out_shape=jax.ShapeDtypeStruct((t, D_MODEL), MXU_DTYPE),
        scratch_shapes=[pltpu.VMEM((2 * HEADS, DV, LANES), F32)],
        compiler_params=pltpu.CompilerParams(
            dimension_semantics=("arbitrary", "arbitrary"), vmem_limit_bytes=VMEM_LIMIT),
        name="mixers",
    )(proj, pos, gw, gb, gn, rng, rnb, invf, sgn, cd, sd)


def _post_mix_kernel(o_ref, h0_ref, wo_ref, g_ref, b_ref, rw_ref, rb_ref,
                     h1_ref, h1t_ref, ri_ref, rg_ref, cnt_ref, tril_ref, carry_ref, *, tm):
    @pl.when(pl.program_id(0) == 0)
    def _():
        r = lax.broadcasted_iota(jnp.int32, (tm, tm), 0)
        c = lax.broadcasted_iota(jnp.int32, (tm, tm), 1)
        tril_ref[...] = (r > c).astype(MXU_DTYPE)
        carry_ref[...] = jnp.zeros_like(carry_ref)

    mix = _dot(o_ref[...], wo_ref[...])
    h1 = _layer_norm(DEEPNORM_ALPHA * h0_ref[...] + mix, g_ref[...], b_ref[...])
    h1_ref[...] = h1
    _store_token_tiles(h1t_ref, _pack_pairs(h1))

    lane = lax.broadcasted_iota(jnp.int32, (tm, LANES), 1)
    lane_f = lane.astype(F32)
    logits = _dot(_mx(h1), rw_ref[...]) + rb_ref[...]
    l = jnp.where(lane < N_EXPERTS, logits, -jnp.inf)
    vals, idxs = [], []
    for _ in range(TOP_K):
        m = jnp.max(l, -1, keepdims=True)
        i = jnp.min(jnp.where(l == m, lane_f, float(LANES)), -1, keepdims=True)
        vals.append(m)
        idxs.append(i)
        l = jnp.where(lane_f == i, -jnp.inf, l)
    exps = [jnp.exp(v - vals[0]) for v in vals]
    inv = 1.0 / (exps[0] + exps[1] + exps[2] + exps[3])

    onehot = jnp.zeros((tm, LANES), F32)
    for i in idxs:
        onehot = onehot + (lane_f == i).astype(F32)
    before = _dot(tril_ref[...], _mx(onehot)) + carry_ref[...]
    packed_i = jnp.zeros((tm, LANES), F32)
    packed_g = jnp.zeros((tm, LANES), F32)
    for k in range(TOP_K):
        rank = jnp.sum(jnp.where(lane_f == idxs[k], before, 0.0), -1, keepdims=True)
        packed_i = jnp.where(lane == k, idxs[k], packed_i)
        packed_i = jnp.where(lane == TOP_K + k, rank, packed_i)
        packed_g = jnp.where(lane == k, exps[k] * inv, packed_g)
    ri_ref[...] = jnp.transpose(packed_i)[:2 * TOP_K].astype(jnp.int32)
    rg_ref[...] = packed_g
    carry_ref[...] = carry_ref[...] + jnp.sum(onehot, 0, keepdims=True)
    cnt_ref[...] = carry_ref[...]


def _post_mix_call(o, h0, wo, g, b, rw, rb, tm):
    t = o.shape[0]
    const = lambda i: (0, 0)
    return pl.pallas_call(
        functools.partial(_post_mix_kernel, tm=tm),
        grid=(t // tm,),
        in_specs=[
            pl.BlockSpec((tm, D_MODEL), lambda i: (i, 0)),
            pl.BlockSpec((tm, D_MODEL), lambda i: (i, 0)),
            pl.BlockSpec((D_MODEL, D_MODEL), const),
            pl.BlockSpec((1, D_MODEL), const),
            pl.BlockSpec((1, D_MODEL), const),
            pl.BlockSpec((D_MODEL, LANES), const),
            pl.BlockSpec((1, LANES), const),
        ],
        out_specs=[
            pl.BlockSpec((tm, D_MODEL), lambda i: (i, 0)),
            pl.BlockSpec((tm * PK_ROWS, LANES), lambda i: (i, 0)),
            pl.BlockSpec((2 * TOP_K, tm), lambda i: (0, i)),
            pl.BlockSpec((tm, LANES), lambda i: (i, 0)),
            pl.BlockSpec((1, LANES), const),
        ],
        out_shape=[
            jax.ShapeDtypeStruct((t, D_MODEL), F32),
            jax.ShapeDtypeStruct((t * PK_ROWS, LANES), jnp.uint32),
            jax.ShapeDtypeStruct((2 * TOP_K, t), jnp.int32),
            jax.ShapeDtypeStruct((t, LANES), F32),
            jax.ShapeDtypeStruct((1, LANES), F32),
        ],
        scratch_shapes=[pltpu.VMEM((tm, tm), MXU_DTYPE), pltpu.VMEM((1, LANES), F32)],
        compiler_params=pltpu.CompilerParams(
            dimension_semantics=("arbitrary",), vmem_limit_bytes=VMEM_LIMIT),
        name="post_mix_router",
    )(o, h0, wo, g, b, rw, rb)


def _step_tables(dest, tm):
    k, t = dest.shape
    return dest.reshape(k, t // tm, tm).transpose(1, 0, 2).reshape(t // tm, 1, k * tm)


def _dispatch_kernel(pad_row_ref, pad_n_ref, dest_ref, h1p_ref, w1_ref, w2_ref,
                     xs_hbm, w1c_ref, w2c_ref, stage_ref, zero_ref, sem, *, tm):
    i = pl.program_id(0)

    def tile(ref, t):
        return ref.at[pl.ds(pl.multiple_of(t * PK_ROWS, PK_ROWS), PK_ROWS), :]

    slot = i % 2
    rows = tm * PK_ROWS
    stage = stage_ref.at[pl.ds(pl.multiple_of(slot * rows, rows), rows), :]
    stage[...] = h1p_ref[...]

    def body(it, c):
        for u in range(DMA_UNROLL):
            r = it * DMA_UNROLL + u
            for k in range(TOP_K):
                pltpu.make_async_copy(tile(stage, r), tile(xs_hbm, dest_ref[0, 0, k * tm + r]),
                                      sem.at[slot]).start(priority=k % 2)
        return c
    lax.fori_loop(0, tm // DMA_UNROLL, body, 0)
    w1c_ref[...] = _mx(w1_ref[...])
    w2c_ref[...] = _mx(w2_ref[...])

    def drain(s):
        for _ in range(TOP_K):
            pltpu.make_async_copy(h1p_ref, xs_hbm.at[pl.ds(0, rows), :], sem.at[s]).wait()

    @pl.when(i > 0)
    def _():
        drain(1 - slot)

    @pl.when(i == pl.num_programs(0) - 1)
    def _():
        drain(slot)
        zero_ref[...] = jnp.zeros_like(zero_ref)
        zero = zero_ref.at[pl.ds(0, PK_ROWS), :]

        def per_expert(e, c):
            def start(r, c2):
                pltpu.make_async_copy(zero, tile(xs_hbm, pad_row_ref[e] + r), sem.at[2]).start()
                return c2
            lax.fori_loop(0, pad_n_ref[e], start, 0)

            def wait(r, c2):
                pltpu.make_async_copy(zero, tile(xs_hbm, 0), sem.at[2]).wait()
                return c2
            lax.fori_loop(0, pad_n_ref[e], wait, 0)
            return c
        lax.fori_loop(0, pad_row_ref.shape[0], per_expert, 0)


def _dispatch_call(pad_row, pad_n, dest, h1p, w1, w2, n_slots, n_steps):
    t = h1p.shape[0] // PK_ROWS
    tm = t // n_steps
    r1, r2 = w1.shape[0] // n_steps, w2.shape[0] // n_steps
    step = lambda i, pr, pn: (i, 0)
    grid_spec = pltpu.PrefetchScalarGridSpec(
        num_scalar_prefetch=2,
        grid=(n_steps,),
        in_specs=[
            pl.BlockSpec((1, 1, TOP_K * tm), lambda i, pr, pn: (i, 0, 0), memory_space=pltpu.SMEM),
            pl.BlockSpec((tm * PK_ROWS, LANES), step),
            pl.BlockSpec((r1, w1.shape[1]), step),
            pl.BlockSpec((r2, w2.shape[1]), step),
        ],
        out_specs=[
            pl.BlockSpec(memory_space=pl.ANY),
            pl.BlockSpec((r1, w1.shape[1]), step),
            pl.BlockSpec((r2, w2.shape[1]), step),
        ],
        scratch_shapes=[pltpu.VMEM((2 * tm * PK_ROWS, LANES), jnp.uint32),
                        pltpu.VMEM((F32_ROWS, LANES), jnp.uint32), pltpu.SemaphoreType.DMA((3,))],
    )
    return pl.pallas_call(
        functools.partial(_dispatch_kernel, tm=tm),
        grid_spec=grid_spec,
        out_shape=[
            jax.ShapeDtypeStruct((n_slots * PK_ROWS, LANES), jnp.uint32),
            jax.ShapeDtypeStruct(w1.shape, MXU_DTYPE),
            jax.ShapeDtypeStruct(w2.shape, MXU_DTYPE),
        ],
        compiler_params=pltpu.CompilerParams(
            dimension_semantics=("arbitrary",), vmem_limit_bytes=VMEM_LIMIT),
        name="dispatch",
    )(pad_row, pad_n, _step_tables(dest, tm), h1p, w1, w2)


def _expert_kernel(be_ref, nr_ref, x_ref, *refs):
    y_ref = refs[-1]
    rows = MOE_BLOCK * PK_ROWS
    for half in range(BLOCKS_PER_STEP):
        w1_ref, b1_ref, w2_ref, b2_ref = refs[4 * half:4 * half + 4]
        g = pl.program_id(0) * BLOCKS_PER_STEP + half
        x_blk = x_ref.at[pl.ds(half * rows, rows), :]
        y_blk = y_ref.at[pl.ds(half * rows, rows), :]

        @pl.when(g >= nr_ref[0])
        def _():
            y_blk[...] = jnp.zeros_like(y_blk)

        @pl.when(g < nr_ref[0])
        def _():
            x = _mx(_unpack_pairs(_load_token_tiles(x_blk, MOE_BLOCK, PK_ROWS)))
            hh = _dot(x, w1_ref[0]) + b1_ref[0]
            x_glu = jnp.minimum(hh[:, :D_FF], SWIGLU_LIMIT)
            x_lin = jnp.clip(hh[:, D_FF:], -SWIGLU_LIMIT, SWIGLU_LIMIT)
            act = x_glu * (1.0 / (1.0 + jnp.exp(-SWIGLU_ALPHA * x_glu))) * (x_lin + 1.0)
            _store_token_tiles(y_blk, _pack_pairs(_dot(_mx(act), w2_ref[0]) + b2_ref[0]))


def _expert_call(block_expert, n_real, xs, w1, b1, w2, b2):
    n_blocks = block_expert.shape[0]
    assert n_blocks % BLOCKS_PER_STEP == 0
    step = lambda i, be, nr: (i, 0)
    rows = BLOCKS_PER_STEP * MOE_BLOCK * PK_ROWS
    expert_specs, expert_args = [], []
    for half in range(BLOCKS_PER_STEP):
        ex = functools.partial(lambda i, be, nr, half: (be[i * BLOCKS_PER_STEP + half], 0, 0), half=half)
        expert_specs += [pl.BlockSpec((1, D_MODEL, 2 * D_FF), ex), pl.BlockSpec((1, 1, 2 * D_FF), ex),
                         pl.BlockSpec((1, D_FF, D_MODEL), ex), pl.BlockSpec((1, 1, D_MODEL), ex)]
        expert_args += [w1, b1, w2, b2]
    grid_spec = pltpu.PrefetchScalarGridSpec(
        num_scalar_prefetch=2,
        grid=(n_blocks // BLOCKS_PER_STEP,),
        in_specs=[pl.BlockSpec((rows, LANES), step)] + expert_specs,
        out_specs=pl.BlockSpec((rows, LANES), step),
    )
    return pl.pallas_call(
        _expert_kernel,
        grid_spec=grid_spec,
        out_shape=jax.ShapeDtypeStruct((n_blocks * MOE_BLOCK * PK_ROWS, LANES), jnp.uint32),
        compiler_params=pltpu.CompilerParams(
            dimension_semantics=("arbitrary",), vmem_limit_bytes=VMEM_LIMIT),
        name="moe_experts",
    )(block_expert, n_real, xs, *expert_args)


def _gather_combine_kernel(dest0_ref, destn_ref, gate_ref, h1_ref, g_ref, b_ref, ys_hbm,
                           out_ref, ybuf, sem, *, tm):
    i = pl.program_id(0)
    slot = i % 2
    rows = tm * PK_ROWS

    def tile(ref, t):
        return ref.at[pl.ds(pl.multiple_of(t * PK_ROWS, PK_ROWS), PK_ROWS), :]

    def gather(tbl, s):
        def body(it, c):
            for u in range(DMA_UNROLL):
                r = it * DMA_UNROLL + u
                for k in range(TOP_K):
                    pltpu.make_async_copy(tile(ys_hbm, tbl[0, 0, k * tm + r]),
                                          tile(ybuf, (s * TOP_K + k) * tm + r), sem.at[s]).start(priority=k % 2)
            return c
        lax.fori_loop(0, tm // DMA_UNROLL, body, 0)

    @pl.when(i == 0)
    def _():
        gather(dest0_ref, 0)

    @pl.when(i + 1 < pl.num_programs(0))
    def _():
        gather(destn_ref, 1 - slot)

    for k in range(TOP_K):
        plane = ybuf.at[pl.ds(pl.multiple_of((slot * TOP_K + k) * rows, rows), rows), :]
        pltpu.make_async_copy(ys_hbm.at[pl.ds(0, rows), :], plane, sem.at[slot]).wait()

    gates = gate_ref[...]
    acc = DEEPNORM_ALPHA * h1_ref[...]
    for k in range(TOP_K):
        plane = ybuf.at[pl.ds(pl.multiple_of((slot * TOP_K + k) * rows, rows), rows), :]
        acc = acc + gates[:, k:k + 1] * _unpack_pairs(_load_token_tiles(plane, tm, PK_ROWS))
    out_ref[...] = _layer_norm(acc, g_ref[...], b_ref[...])


def _gather_combine_call(dest, gates, h1, g, b, ys, tm):
    t = h1.shape[0]
    nt = t // tm
    const = lambda i: (0, 0)
    smem = functools.partial(pl.BlockSpec, memory_space=pltpu.SMEM)
    dest = _step_tables(dest, tm)
    dest_next = jnp.concatenate([dest[1:], dest[:1]], axis=0)
    return pl.pallas_call(
        functools.partial(_gather_combine_kernel, tm=tm),
        grid=(nt,),
        in_specs=[
            smem((1, 1, TOP_K * tm), lambda i: (0, 0, 0)),
            smem((1, 1, TOP_K * tm), lambda i: (i, 0, 0)),
            pl.BlockSpec((tm, LANES), lambda i: (i, 0)),
            pl.BlockSpec((tm, D_MODEL), lambda i: (i, 0)),
            pl.BlockSpec((1, D_MODEL), const),
            pl.BlockSpec((1, D_MODEL), const),
            pl.BlockSpec(memory_space=pl.ANY),
        ],
        out_specs=pl.BlockSpec((tm, D_MODEL), lambda i: (i, 0)),
        out_shape=jax.ShapeDtypeStruct((t, D_MODEL), F32),
        scratch_shapes=[pltpu.VMEM((2 * TOP_K * tm * PK_ROWS, LANES), jnp.uint32),
                        pltpu.SemaphoreType.DMA((2,))],
        compiler_params=pltpu.CompilerParams(
            dimension_semantics=("arbitrary",), vmem_limit_bytes=VMEM_LIMIT),
        name="gather_combine_ln2",
    )(dest, dest_next, gates, h1, g, b, ys)


def _relayout_w_in(w):
    sizes = (256, 256, 512, 512, GATE_RANK, 256, 256, 512, 512)
    offs = np.concatenate([[0], np.cumsum(sizes)])
    gq, gk, gv, gg, glr, rq, rk, rv, rg = [w[:, offs[i]:offs[i + 1]] for i in range(9)]
    perm = np.zeros((256,), np.int32)
    for p in range(2):
        for l in range(LANES):
            part, hh, f = l // 64, (l % 64) // 32, l % 32
            perm[p * LANES + l] = (2 * p + hh) * DK + part * (DK // 2) + f
    glr = jnp.pad(glr, ((0, 0), (0, LANES - GATE_RANK)))
    return jnp.concatenate([gq, gk, gv, gg, glr, rq[:, perm], rk[:, perm], rv, rg], axis=1)


def kernel(x, positions, ln_in_g, ln_in_b, w_in, gla_gate_w, gla_gate_b, gla_norm_g, ret_norm_g,
           ret_norm_b, w_out, ln1_g, ln1_b, router_w, router_b, moe_w1, moe_b1, moe_w2, moe_b2,
           ln2_g, ln2_b):
    bsz, seq, d = x.shape
    assert d == D_MODEL and seq % GROUP == 0 and w_in.shape[0] == DEPTH == 1
    t = bsz * seq
    ts = 1024 if seq % 1024 == 0 else GROUP
    tm = min(512, t)
    assert seq % ts == 0 and t % tm == 0
    row = lambda v: v.reshape(1, -1).astype(F32)

    w_p = _relayout_w_in(w_in[0]).astype(MXU_DTYPE)
    h0, proj = _ln_proj_call(x.reshape(t, d), row(ln_in_g), row(ln_in_b), w_p, tm)

    gw = jnp.pad(gla_gate_w[0], ((0, LANES - GATE_RANK), (0, 0))).astype(MXU_DTYPE)
    half = DK // 2
    inv_freq = 1.0 / (ROPE_BASE ** np.linspace(0.0, 1.0, half, dtype=np.float32))
    invf = jnp.asarray(np.tile(inv_freq, LANES // half).reshape(1, LANES), F32)
    sgn = jnp.asarray(np.where(np.arange(LANES) < LANES // 2, -1.0, 1.0).reshape(1, LANES), F32)
    step_ang = np.arange(GROUP, dtype=np.float64)[:, None] * np.tile(inv_freq, LANES // half)[None, :]
    cd = jnp.asarray(np.cos(step_ang), F32)
    sd = jnp.asarray(np.sin(step_ang), F32)
    o = _mixer_call(proj, positions.reshape(t, 1), gw, row(gla_gate_b[0]), row(gla_norm_g[0]),
                    row(ret_norm_g[0]), row(ret_norm_b[0]), invf, sgn, cd, sd, bsz, seq, ts)

    rw = jnp.pad(router_w[0], ((0, 0), (0, LANES - N_EXPERTS))).astype(MXU_DTYPE)
    rb = jnp.pad(router_b[0], (0, LANES - N_EXPERTS)).reshape(1, LANES).astype(F32)
    h1, h1t, r_i, r_g, cnt = _post_mix_call(o, h0, w_out[0].astype(MXU_DTYPE), row(ln1_g[0]),
                                       row(ln1_b[0]), rw, rb, tm)

    e = r_i[:TOP_K].reshape(-1)
    rank = r_i[TOP_K:].reshape(-1)
    counts = cnt[0, :N_EXPERTS].astype(jnp.int32)
    padded = ((counts + MOE_BLOCK - 1) // MOE_BLOCK) * MOE_BLOCK
    pad_end = jnp.cumsum(padded)
    pad_start = pad_end - padded
    tk = t * TOP_K
    n_blocks = (tk + MOE_BLOCK - 1) // MOE_BLOCK + N_EXPERTS
    i32 = jnp.int32
    dest = rank
    for j in range(N_EXPERTS):
        dest = dest + jnp.where(e == j, pad_start[j], 0)
    dest = dest.reshape(TOP_K, t)
    block_start = jnp.arange(n_blocks, dtype=i32) * MOE_BLOCK
    block_expert = jnp.minimum(jnp.sum(block_start[:, None] >= pad_end[None, :], -1),
                               N_EXPERTS - 1).astype(i32)
    n_real = (pad_end[-1:] // MOE_BLOCK).astype(i32)

    n_slots = n_blocks * MOE_BLOCK
    hole_row = jnp.concatenate([pad_start + counts, pad_end[-1:]]).astype(i32)
    hole_n = jnp.concatenate([padded - counts, n_slots - pad_end[-1:]]).astype(i32)
    xs, w1c, w2c = _dispatch_call(hole_row, hole_n, dest, h1t,
                                  moe_w1[0].reshape(N_EXPERTS * D_MODEL, 2 * D_FF),
                                  moe_w2[0].reshape(N_EXPERTS * D_FF, D_MODEL), n_slots, DISPATCH_STEPS)
    ys = _expert_call(block_expert, n_real, xs,
                      w1c.reshape(N_EXPERTS, D_MODEL, 2 * D_FF), moe_b1[0].reshape(N_EXPERTS, 1, 2 * D_FF),
                      w2c.reshape(N_EXPERTS, D_FF, D_MODEL), moe_b2[0].reshape(N_EXPERTS, 1, D_MODEL))

    out = _gather_combine_call(dest, r_g, h1, row(ln2_g[0]), row(ln2_b[0]), ys, min(512, t))
    return out.reshape(bsz, seq, d)
```

```python
import functools
import math

import numpy as np
import jax
import jax.numpy as jnp
from jax import lax
from jax.experimental import pallas as pl
from jax.experimental.pallas import tpu as pltpu

F32 = jnp.float32
MXU_DTYPE = jnp.bfloat16

D_MODEL = 1024
CHUNK = 64
GC = 4
GROUP = GC * CHUNK
HEADS = 4
DK = 64
DV = 128
GATE_RANK = 16
GATE_NORM = 16.0
ROPE_BASE = 10000.0
N_EXPERTS = 32
TOP_K = 4
D_FF = 1024
SWIGLU_ALPHA = 1.702
SWIGLU_LIMIT = 7.0
MOE_BLOCK = 256
POST_SUBTILE = 512
BLOCKS_PER_STEP = 2
DMA_UNROLL = 8
DISPATCH_STEPS = 64
LN_EPS = 1e-5
DEPTH = 1
DEEPNORM_ALPHA = (2.0 * DEPTH) ** 0.25

LANES = 128
VMEM_LIMIT = 56 * 1024 * 1024

C_GQ, C_GK, C_GV, C_GG, C_GLR = 0, 256, 512, 1024, 1536
C_RQ, C_RK, C_RV, C_RG = 1664, 1920, 2176, 2688
PROJ_W = 3200

LOG_GAMMA = [math.log1p(-(2.0 ** (-5.0 - h))) for h in range(HEADS)]


def _dot(a, b):
    return jnp.dot(a, b, preferred_element_type=F32)


def _dot_nt(a, b):
    return lax.dot_general(a, b, (((1,), (1,)), ((), ())), preferred_element_type=F32)


def _dot_tn(a, b):
    return lax.dot_general(a, b, (((0,), (0,)), ((), ())), preferred_element_type=F32)


def _mx(a):
    return a.astype(MXU_DTYPE)


def _layer_norm(x, g, b):
    mu = jnp.mean(x, -1, keepdims=True)
    xc = x - mu
    var = jnp.mean(xc * xc, -1, keepdims=True)
    return xc * lax.rsqrt(var + LN_EPS) * g + b


def _silu(x):
    return x * (1.0 / (1.0 + jnp.exp(-x)))


F32_ROWS = D_MODEL // LANES
PK_ROWS = D_MODEL // (2 * LANES)


def _load_token_tiles(ref, n, r):
    return jnp.concatenate([ref[pl.ds(j, n, stride=r), :] for j in range(r)], axis=1)


def _store_token_tiles(ref, rows):
    n, r = rows.shape[0], rows.shape[1] // LANES
    for j in range(r):
        ref[pl.ds(j, n, stride=r), :] = rows[:, j * LANES:(j + 1) * LANES]


def _pack_pairs(x):
    h = x.shape[1] // 2
    r = x.astype(jnp.bfloat16).astype(F32)
    lo = lax.bitcast_convert_type(r[:, :h], jnp.uint32) >> 16
    hi = lax.bitcast_convert_type(r[:, h:], jnp.uint32) & jnp.uint32(0xFFFF0000)
    return lo | hi


def _unpack_pairs(u):
    lo = lax.bitcast_convert_type(u << 16, F32)
    hi = lax.bitcast_convert_type(u & jnp.uint32(0xFFFF0000), F32)
    return jnp.concatenate([lo, hi], axis=1)


def _ln_proj_kernel(x_ref, g_ref, b_ref, w_ref, h_ref, p_ref):
    h = _layer_norm(x_ref[...], g_ref[...], b_ref[...])
    h_ref[...] = h
    p_ref[...] = _dot(_mx(h), w_ref[...])


def _ln_proj_call(x2, g, b, w, tm):
    t = x2.shape[0]
    return pl.pallas_call(
        _ln_proj_kernel,
        grid=(t // tm,),
        in_specs=[
            pl.BlockSpec((tm, D_MODEL), lambda i: (i, 0)),
            pl.BlockSpec((1, D_MODEL), lambda i: (0, 0)),
            pl.BlockSpec((1, D_MODEL), lambda i: (0, 0)),
            pl.BlockSpec((D_MODEL, PROJ_W), lambda i: (0, 0)),
        ],
        out_specs=[
            pl.BlockSpec((tm, D_MODEL), lambda i: (i, 0)),
            pl.BlockSpec((tm, PROJ_W), lambda i: (i, 0)),
        ],
        out_shape=[
            jax.ShapeDtypeStruct((t, D_MODEL), F32),
            jax.ShapeDtypeStruct((t, PROJ_W), F32),
        ],
        compiler_params=pltpu.CompilerParams(
            dimension_semantics=("arbitrary",), vmem_limit_bytes=VMEM_LIMIT),
        name="ln_proj",
    )(x2, g, b, w)


def _mixer_kernel(p_ref, pos_ref, gw_ref, gb_ref, gn_ref, rng_ref, rnb_ref, invf_ref, sgn_ref,
                  cd_ref, sd_ref, o_ref, st_ref, *, n_groups):
    @pl.when(pl.program_id(1) == 0)
    def _():
        st_ref[...] = jnp.zeros_like(st_ref)

    rr = lax.broadcasted_iota(jnp.int32, (GROUP, GROUP), 0)
    cc = lax.broadcasted_iota(jnp.int32, (GROUP, GROUP), 1)
    same = (rr >> 6) == (cc >> 6)
    lower = same & (rr >= cc)
    upper = same & (rr < cc)
    tri = lower.astype(MXU_DTYPE)
    dist = jnp.abs(rr - cc).astype(F32)
    rowi = lax.broadcasted_iota(jnp.int32, (GROUP, LANES), 0)
    rin = (rowi & (CHUNK - 1)).astype(F32)
    in_chunk = [(rowi >> 6) == c for c in range(GC)]
    lane = lax.broadcasted_iota(jnp.int32, (GROUP, LANES), 1)
    gla_half = [((lane >> 6) & 1) == i for i in range(2)]
    ret_half = [((lane >> 5) & 1) == i for i in range(2)]
    ret_d = [jnp.where(same, jnp.exp(LOG_GAMMA[h] * dist), 0.0) for h in range(HEADS)]
    ret_eb = [jnp.exp(LOG_GAMMA[h] * (rin + 1.0)) for h in range(HEADS)]
    ret_ek = [jnp.exp(LOG_GAMMA[h] * (CHUNK - 1.0 - rin)) for h in range(HEADS)]
    ret_dec = [math.exp(LOG_GAMMA[h] * CHUNK) for h in range(HEADS)]

    def block_diag(x):
        return _mx(jnp.concatenate([jnp.where(in_chunk[c], x, 0.0) for c in range(GC)], axis=1))

    lower2 = jnp.concatenate([lower, lower], axis=0)
    upper2 = jnp.concatenate([upper, upper], axis=0)
    ret_d2 = [jnp.concatenate([ret_d[2 * p], ret_d[2 * p + 1]], axis=0) for p in range(2)]
    zeros_v = jnp.zeros((GROUP, DV), MXU_DTYPE)

    def pair_products(hs0, q_m, sc, v2, v2_kv, k_bd, decay):
        sc_b = _mx(sc)
        v_bd = jnp.concatenate([jnp.concatenate([v2[:, :DV], zeros_v], axis=1),
                                jnp.concatenate([zeros_v, v2[:, DV:]], axis=1)], axis=0)
        o_intra = _dot(jnp.concatenate([sc_b[:GROUP], sc_b[GROUP:]], axis=1), v_bd)
        kv_t = _dot_tn(v2_kv, k_bd)
        st = [st_ref[hs0], st_ref[hs0 + 1]]
        parts = []
        for c in range(GC):
            cr = slice(c * CHUNK, (c + 1) * CHUNK)
            qc = jnp.concatenate([q_m[0][cr], q_m[1][cr]], axis=0)
            r = _dot_nt(qc, _mx(jnp.concatenate(st, axis=0)))
            parts.append(jnp.concatenate([r[:CHUNK, :DV], r[CHUNK:, DV:]], axis=1))
            for i in range(2):
                st[i] = st[i] * decay(c, i) + kv_t[i * DV:(i + 1) * DV, c * LANES:(c + 1) * LANES]
        st_ref[hs0] = st[0]
        st_ref[hs0 + 1] = st[1]
        return o_intra, jnp.concatenate(parts, axis=0)

    def group_body(g, carry):
        rows = pl.ds(pl.multiple_of(g * GROUP, GROUP), GROUP)

        z = _dot(_mx(p_ref[rows, C_GLR:C_GLR + LANES]), gw_ref[...]) + gb_ref[...]
        la = (jnp.minimum(z, 0.0) - jnp.log1p(jnp.exp(-jnp.abs(z)))) * (1.0 / GATE_NORM)
        la_hi = _mx(la)
        r1 = la - la_hi.astype(F32)
        la_mid = _mx(r1)
        la_lo = _mx(r1 - la_mid.astype(F32))
        b3 = _dot(tri, jnp.concatenate([la_hi, la_mid, la_lo], axis=1))
        b = b3[:, :256] + b3[:, 256:512] + b3[:, 512:]
        b_ends = [b[(c + 1) * CHUNK - 1:(c + 1) * CHUNK, :] for c in range(GC)]
        b_last = jnp.concatenate([jnp.broadcast_to(e, (CHUNK, 256)) for e in b_ends], axis=0)
        decs = [jnp.exp(e) for e in b_ends]
        eb = jnp.exp(b)
        enb = jnp.exp(-b)
        ekv = jnp.exp(b_last - b)
        q = p_ref[rows, C_GQ:C_GQ + 256] * (DK ** -0.5)
        k = p_ref[rows, C_GK:C_GK + 256]
        qe, qn = q * eb, q * enb
        ke, kn, kk = k * eb, k * enb, k * ekv
        for p in range(2):
            ls = slice(p * LANES, (p + 1) * LANES)
            qe_m = [_mx(jnp.where(gla_half[i], qe[:, ls], 0.0)) for i in range(2)]
            qn_m = [_mx(jnp.where(gla_half[i], qn[:, ls], 0.0)) for i in range(2)]
            s_lo = _dot_nt(jnp.concatenate(qe_m, axis=0), _mx(kn[:, ls]))
            s_up = _dot_nt(jnp.concatenate(qn_m, axis=0), _mx(ke[:, ls]))
            sc = jnp.where(lower2, s_lo, jnp.where(upper2, s_up, 0.0))
            v2 = _mx(p_ref[rows, C_GV + 2 * p * DV:C_GV + (2 * p + 2) * DV])
            o_intra, o_inter = pair_products(2 * p, qe_m, sc, v2, v2, block_diag(kk[:, ls]),
                                             lambda c, i: decs[c][:, ls])
            o2 = o_intra + o_inter
            for i in range(2):
                h = 2 * p + i
                o = o2[:, i * DV:(i + 1) * DV]
                o = o * lax.rsqrt(jnp.mean(o * o, -1, keepdims=True) + LN_EPS) * gn_ref[...]
                o = o * _silu(p_ref[rows, C_GG + h * DV:C_GG + (h + 1) * DV])
                o_ref[rows, h * DV:(h + 1) * DV] = o.astype(o_ref.dtype)

        base = pos_ref[pl.ds(pl.multiple_of(g * GROUP, GROUP), 1), :].astype(F32) * invf_ref[...]
        c0, s0 = jnp.cos(base), jnp.sin(base)
        cs = c0 * cd_ref[...] - s0 * sd_ref[...]
        sn = (s0 * sgn_ref[...]) * cd_ref[...] + (c0 * sgn_ref[...]) * sd_ref[...]
        for p in range(2):
            tq = p_ref[rows, C_RQ + p * LANES:C_RQ + (p + 1) * LANES]
            tk = p_ref[rows, C_RK + p * LANES:C_RK + (p + 1) * LANES] * (DK ** -0.5)
            rq = tq * cs + pltpu.roll(tq, LANES // 2, 1) * sn
            rk = tk * cs + pltpu.roll(tk, LANES // 2, 1) * sn
            q_m = [_mx(jnp.where(ret_half[i], rq, 0.0)) for i in range(2)]
            s = _dot_nt(jnp.concatenate(q_m, axis=0), _mx(rk)) * ret_d2[p]
            v2f = p_ref[rows, C_RV + 2 * p * DV:C_RV + (2 * p + 2) * DV]
            v2_kv = jnp.concatenate([v2f[:, :DV] * ret_ek[2 * p], v2f[:, DV:] * ret_ek[2 * p + 1]], axis=1)
            o_intra, o_inter = pair_products(HEADS + 2 * p, q_m, s, _mx(v2f), _mx(v2_kv), block_diag(rk),
                                             lambda c, i: ret_dec[2 * p + i])
            for i in range(2):
                h = 2 * p + i
                o = o_intra[:, i * DV:(i + 1) * DV] + ret_eb[h] * o_inter[:, i * DV:(i + 1) * DV]
                mu = jnp.mean(o, -1, keepdims=True)
                oc = o - mu
                var = jnp.mean(oc * oc, -1, keepdims=True)
                o = oc * lax.rsqrt(var + LN_EPS) * rng_ref[:, h * DV:(h + 1) * DV] \
                    + rnb_ref[:, h * DV:(h + 1) * DV]
                o = o * _silu(p_ref[rows, C_RG + h * DV:C_RG + (h + 1) * DV])
                o_ref[rows, (HEADS + h) * DV:(HEADS + h + 1) * DV] = o.astype(o_ref.dtype)
        return carry

    lax.fori_loop(0, n_groups, group_body, 0)


def _mixer_call(proj, pos, gw, gb, gn, rng, rnb, invf, sgn, cd, sd, bsz, seq, ts):
    t = bsz * seq
    nst = seq // ts
    const = lambda b, s: (0, 0)
    return pl.pallas_call(
        functools.partial(_mixer_kernel, n_groups=ts // GROUP),
        grid=(bsz, nst),
        in_specs=[
            pl.BlockSpec((ts, PROJ_W), lambda b, s: (b * nst + s, 0)),
            pl.BlockSpec((ts, 1), lambda b, s: (b * nst + s, 0)),
            pl.BlockSpec((LANES, 256), const),
            pl.BlockSpec((1, 256), const),
            pl.BlockSpec((1, DV), const),
            pl.BlockSpec((1, HEADS * DV), const),
            pl.BlockSpec((1, HEADS * DV), const),
            pl.BlockSpec((1, LANES), const),
            pl.BlockSpec((1, LANES), const),
            pl.BlockSpec((GROUP, LANES), const),
            pl.BlockSpec((GROUP, LANES), const),
        ],
        out_specs=pl.BlockSpec((ts, D_MODEL), lambda b, s: (b * nst + s, 0)),
        out_shape=jax.ShapeDtypeStruct((t, D_MODEL), MXU_DTYPE),
        scratch_shapes=[pltpu.VMEM((2 * HEADS, DV, LANES), F32)],
        compiler_params=pltpu.CompilerParams(
            dimension_semantics=("arbitrary", "arbitrary"), vmem_limit_bytes=VMEM_LIMIT),
        name="mixers",
    )(proj, pos, gw, gb, gn, rng, rnb, invf, sgn, cd, sd)


def _post_mix_kernel(o_ref, h0_ref, wo_ref, g_ref, b_ref, rw_ref, rb_ref,
                     h1_ref, h1t_ref, ri_ref, rg_ref, cnt_ref, tril_ref, carry_ref, *, tm, sub):
    @pl.when(pl.program_id(0) == 0)
    def _():
        r = lax.broadcasted_iota(jnp.int32, (sub, sub), 0)
        c = lax.broadcasted_iota(jnp.int32, (sub, sub), 1)
        tril_ref[...] = (r > c).astype(MXU_DTYPE)
        carry_ref[...] = jnp.zeros_like(carry_ref)

    lane = lax.broadcasted_iota(jnp.int32, (sub, LANES), 1)
    lane_f = lane.astype(F32)
    carry = carry_ref[...]
    for s in range(tm // sub):
        rs = slice(s * sub, (s + 1) * sub)
        mix = _dot(o_ref[rs, :], wo_ref[...])
        h1 = _layer_norm(DEEPNORM_ALPHA * h0_ref[rs, :] + mix, g_ref[...], b_ref[...])
        h1_ref[rs, :] = h1
        _store_token_tiles(h1t_ref.at[pl.ds(s * sub * PK_ROWS, sub * PK_ROWS), :], _pack_pairs(h1))

        logits = _dot(_mx(h1), rw_ref[...]) + rb_ref[...]
        l = jnp.where(lane < N_EXPERTS, logits, -jnp.inf)
        vals, idxs = [], []
        for _ in range(TOP_K):
            m = jnp.max(l, -1, keepdims=True)
            i = jnp.min(jnp.where(l == m, lane_f, float(LANES)), -1, keepdims=True)
            vals.append(m)
            idxs.append(i)
            l = jnp.where(lane_f == i, -jnp.inf, l)
        exps = [jnp.exp(v - vals[0]) for v in vals]
        inv = 1.0 / (exps[0] + exps[1] + exps[2] + exps[3])

        onehot = jnp.zeros((sub, LANES), F32)
        for i in idxs:
            onehot = onehot + (lane_f == i).astype(F32)
        before = _dot(tril_ref[...], _mx(onehot)) + carry
        packed_i = jnp.zeros((sub, LANES), F32)
        packed_g = jnp.zeros((sub, LANES), F32)
        for k in range(TOP_K):
            rank = jnp.sum(jnp.where(lane_f == idxs[k], before, 0.0), -1, keepdims=True)
            packed_i = jnp.where(lane == k, idxs[k], packed_i)
            packed_i = jnp.where(lane == TOP_K + k, rank, packed_i)
            packed_g = jnp.where(lane == k, exps[k] * inv, packed_g)
        ri_ref[:, rs] = jnp.transpose(packed_i)[:2 * TOP_K].astype(jnp.int32)
        rg_ref[rs, :] = packed_g
        carry = carry + jnp.sum(onehot, 0, keepdims=True)
    carry_ref[...] = carry
    cnt_ref[...] = carry


def _post_mix_call(o, h0, wo, g, b, rw, rb, tm):
    t = o.shape[0]
    sub = min(POST_SUBTILE, tm)
    assert tm % sub == 0
    const = lambda i: (0, 0)
    return pl.pallas_call(
        functools.partial(_post_mix_kernel, tm=tm, sub=sub),
        grid=(t // tm,),
        in_specs=[
            pl.BlockSpec((tm, D_MODEL), lambda i: (i, 0)),
            pl.BlockSpec((tm, D_MODEL), lambda i: (i, 0)),
            pl.BlockSpec((D_MODEL, D_MODEL), const),
            pl.BlockSpec((1, D_MODEL), const),
            pl.BlockSpec((1, D_MODEL), const),
            pl.BlockSpec((D_MODEL, LANES), const),
            pl.BlockSpec((1, LANES), const),
        ],
        out_specs=[
            pl.BlockSpec((tm, D_MODEL), lambda i: (i, 0)),
            pl.BlockSpec((tm * PK_ROWS, LANES), lambda i: (i, 0)),
            pl.BlockSpec((2 * TOP_K, tm), lambda i: (0, i)),
            pl.BlockSpec((tm, LANES), lambda i: (i, 0)),
            pl.BlockSpec((1, LANES), const),
        ],
        out_shape=[
            jax.ShapeDtypeStruct((t, D_MODEL), F32),
            jax.ShapeDtypeStruct((t * PK_ROWS, LANES), jnp.uint32),
            jax.ShapeDtypeStruct((2 * TOP_K, t), jnp.int32),
            jax.ShapeDtypeStruct((t, LANES), F32),
            jax.ShapeDtypeStruct((1, LANES), F32),
        ],
        scratch_shapes=[pltpu.VMEM((sub, sub), MXU_DTYPE), pltpu.VMEM((1, LANES), F32)],
        compiler_params=pltpu.CompilerParams(
            dimension_semantics=("arbitrary",), vmem_limit_bytes=VMEM_LIMIT),
        name="post_mix_router",
    )(o, h0, wo, g, b, rw, rb)


def _step_tables(dest, tm):
    k, t = dest.shape
    return dest.reshape(k, t // tm, tm).transpose(1, 0, 2).reshape(t // tm, 1, k * tm)


def _dispatch_kernel(pad_row_ref, pad_n_ref, dest_ref, h1p_ref, w1_ref, w2_ref,
                     xs_hbm, w1c_ref, w2c_ref, stage_ref, zero_ref, sem, *, tm):
    i = pl.program_id(0)

    def tile(ref, t):
        return ref.at[pl.ds(pl.multiple_of(t * PK_ROWS, PK_ROWS), PK_ROWS), :]

    slot = i % 2
    rows = tm * PK_ROWS
    stage = stage_ref.at[pl.ds(pl.multiple_of(slot * rows, rows), rows), :]
    stage[...] = h1p_ref[...]

    def body(it, c):
        for u in range(DMA_UNROLL):
            r = it * DMA_UNROLL + u
            for k in range(TOP_K):
                pltpu.make_async_copy(tile(stage, r), tile(xs_hbm, dest_ref[0, 0, k * tm + r]),
                                      sem.at[slot]).start(priority=k % 2)
        return c
    lax.fori_loop(0, tm // DMA_UNROLL, body, 0)
    w1c_ref[...] = _mx(w1_ref[...])
    w2c_ref[...] = _mx(w2_ref[...])

    def drain(s):
        for _ in range(TOP_K):
            pltpu.make_async_copy(h1p_ref, xs_hbm.at[pl.ds(0, rows), :], sem.at[s]).wait()

    @pl.when(i > 0)
    def _():
        drain(1 - slot)

    @pl.when(i == pl.num_programs(0) - 1)
    def _():
        drain(slot)
        zero_ref[...] = jnp.zeros_like(zero_ref)
        zero = zero_ref.at[pl.ds(0, PK_ROWS), :]

        def per_expert(e, c):
            def start(r, c2):
                pltpu.make_async_copy(zero, tile(xs_hbm, pad_row_ref[e] + r), sem.at[2]).start()
                return c2
            lax.fori_loop(0, pad_n_ref[e], start, 0)

            def wait(r, c2):
                pltpu.make_async_copy(zero, tile(xs_hbm, 0), sem.at[2]).wait()
                return c2
            lax.fori_loop(0, pad_n_ref[e], wait, 0)
            return c
        lax.fori_loop(0, pad_row_ref.shape[0], per_expert, 0)


def _dispatch_call(pad_row, pad_n, dest, h1p, w1, w2, n_slots, n_steps):
    t = h1p.shape[0] // PK_ROWS
    tm = t // n_steps
    r1, r2 = w1.shape[0] // n_steps, w2.shape[0] // n_steps
    step = lambda i, pr, pn: (i, 0)
    grid_spec = pltpu.PrefetchScalarGridSpec(
        num_scalar_prefetch=2,
        grid=(n_steps,),
        in_specs=[
            pl.BlockSpec((1, 1, TOP_K * tm), lambda i, pr, pn: (i, 0, 0), memory_space=pltpu.SMEM),
            pl.BlockSpec((tm * PK_ROWS, LANES), step),
            pl.BlockSpec((r1, w1.shape[1]), step),
            pl.BlockSpec((r2, w2.shape[1]), step),
        ],
        out_specs=[
            pl.BlockSpec(memory_space=pl.ANY),
            pl.BlockSpec((r1, w1.shape[1]), step),
            pl.BlockSpec((r2, w2.shape[1]), step),
        ],
        scratch_shapes=[pltpu.VMEM((2 * tm * PK_ROWS, LANES), jnp.uint32),
                        pltpu.VMEM((F32_ROWS, LANES), jnp.uint32), pltpu.SemaphoreType.DMA((3,))],
    )
    return pl.pallas_call(
        functools.partial(_dispatch_kernel, tm=tm),
        grid_spec=grid_spec,
        out_shape=[
            jax.ShapeDtypeStruct((n_slots * PK_ROWS, LANES), jnp.uint32),
            jax.ShapeDtypeStruct(w1.shape, MXU_DTYPE),
            jax.ShapeDtypeStruct(w2.shape, MXU_DTYPE),
        ],
        compiler_params=pltpu.CompilerParams(
            dimension_semantics=("arbitrary",), vmem_limit_bytes=VMEM_LIMIT),
        name="dispatch",
    )(pad_row, pad_n, _step_tables(dest, tm), h1p, w1, w2)


def _expert_kernel(be_ref, nr_ref, x_ref, *refs):
    y_ref = refs[-1]
    rows = MOE_BLOCK * PK_ROWS
    for half in range(BLOCKS_PER_STEP):
        w1_ref, b1_ref, w2_ref, b2_ref = refs[4 * half:4 * half + 4]
        g = pl.program_id(0) * BLOCKS_PER_STEP + half
        x_blk = x_ref.at[pl.ds(half * rows, rows), :]
        y_blk = y_ref.at[pl.ds(half * rows, rows), :]

        @pl.when(g >= nr_ref[0])
        def _():
            y_blk[...] = jnp.zeros_like(y_blk)

        @pl.when(g < nr_ref[0])
        def _():
            x = _mx(_unpack_pairs(_load_token_tiles(x_blk, MOE_BLOCK, PK_ROWS)))
            hh = _dot(x, w1_ref[0]) + b1_ref[0]
            x_glu = jnp.minimum(hh[:, :D_FF], SWIGLU_LIMIT)
            x_lin = jnp.clip(hh[:, D_FF:], -SWIGLU_LIMIT, SWIGLU_LIMIT)
            act = x_glu * (1.0 / (1.0 + jnp.exp(-SWIGLU_ALPHA * x_glu))) * (x_lin + 1.0)
            _store_token_tiles(y_blk, _pack_pairs(_dot(_mx(act), w2_ref[0]) + b2_ref[0]))


def _expert_call(block_expert, n_real, xs, w1, b1, w2, b2):
    n_blocks = block_expert.shape[0]
    assert n_blocks % BLOCKS_PER_STEP == 0
    step = lambda i, be, nr: (i, 0)
    rows = BLOCKS_PER_STEP * MOE_BLOCK * PK_ROWS
    expert_specs, expert_args = [], []
    for half in range(BLOCKS_PER_STEP):
        ex = functools.partial(lambda i, be, nr, half: (be[i * BLOCKS_PER_STEP + half], 0, 0), half=half)
        expert_specs += [pl.BlockSpec((1, D_MODEL, 2 * D_FF), ex), pl.BlockSpec((1, 1, 2 * D_FF), ex),
                         pl.BlockSpec((1, D_FF, D_MODEL), ex), pl.BlockSpec((1, 1, D_MODEL), ex)]
        expert_args += [w1, b1, w2, b2]
    grid_spec = pltpu.PrefetchScalarGridSpec(
        num_scalar_prefetch=2,
        grid=(n_blocks // BLOCKS_PER_STEP,),
        in_specs=[pl.BlockSpec((rows, LANES), step)] + expert_specs,
        out_specs=pl.BlockSpec((rows, LANES), step),
    )
    return pl.pallas_call(
        _expert_kernel,
        grid_spec=grid_spec,
        out_shape=jax.ShapeDtypeStruct((n_blocks * MOE_BLOCK * PK_ROWS, LANES), jnp.uint32),
        compiler_params=pltpu.CompilerParams(
            dimension_semantics=("arbitrary",), vmem_limit_bytes=VMEM_LIMIT),
        name="moe_experts",
    )(block_expert, n_real, xs, *expert_args)


def _gather_combine_kernel(dest0_ref, destn_ref, gate_ref, h1_ref, g_ref, b_ref, ys_hbm,
                           out_ref, ybuf, sem, *, tm):
    i = pl.program_id(0)
    slot = i % 2
    rows = tm * PK_ROWS

    def tile(ref, t):
        return ref.at[pl.ds(pl.multiple_of(t * PK_ROWS, PK_ROWS), PK_ROWS), :]

    def gather(tbl, s):
        def body(it, c):
            for u in range(DMA_UNROLL):
                r = it * DMA_UNROLL + u
                for k in range(TOP_K):
                    pltpu.make_async_copy(tile(ys_hbm, tbl[0, 0, k * tm + r]),
                                          tile(ybuf, (s * TOP_K + k) * tm + r), sem.at[s]).start(priority=k % 2)
            return c
        lax.fori_loop(0, tm // DMA_UNROLL, body, 0)

    @pl.when(i == 0)
    def _():
        gather(dest0_ref, 0)

    @pl.when(i + 1 < pl.num_programs(0))
    def _():
        gather(destn_ref, 1 - slot)

    for k in range(TOP_K):
        plane = ybuf.at[pl.ds(pl.multiple_of((slot * TOP_K + k) * rows, rows), rows), :]
        pltpu.make_async_copy(ys_hbm.at[pl.ds(0, rows), :], plane, sem.at[slot]).wait()

    gates = gate_ref[...]
    acc = DEEPNORM_ALPHA * h1_ref[...]
    for k in range(TOP_K):
        plane = ybuf.at[pl.ds(pl.multiple_of((slot * TOP_K + k) * rows, rows), rows), :]
        acc = acc + gates[:, k:k + 1] * _unpack_pairs(_load_token_tiles(plane, tm, PK_ROWS))
    out_ref[...] = _layer_norm(acc, g_ref[...], b_ref[...])


def _gather_combine_call(dest, gates, h1, g, b, ys, tm):
    t = h1.shape[0]
    nt = t // tm
    const = lambda i: (0, 0)
    smem = functools.partial(pl.BlockSpec, memory_space=pltpu.SMEM)
    dest = _step_tables(dest, tm)
    dest_next = jnp.concatenate([dest[1:], dest[:1]], axis=0)
    return pl.pallas_call(
        functools.partial(_gather_combine_kernel, tm=tm),
        grid=(nt,),
        in_specs=[
            smem((1, 1, TOP_K * tm), lambda i: (0, 0, 0)),
            smem((1, 1, TOP_K * tm), lambda i: (i, 0, 0)),
            pl.BlockSpec((tm, LANES), lambda i: (i, 0)),
            pl.BlockSpec((tm, D_MODEL), lambda i: (i, 0)),
            pl.BlockSpec((1, D_MODEL), const),
            pl.BlockSpec((1, D_MODEL), const),
            pl.BlockSpec(memory_space=pl.ANY),
        ],
        out_specs=pl.BlockSpec((tm, D_MODEL), lambda i: (i, 0)),
        out_shape=jax.ShapeDtypeStruct((t, D_MODEL), F32),
        scratch_shapes=[pltpu.VMEM((2 * TOP_K * tm * PK_ROWS, LANES), jnp.uint32),
                        pltpu.SemaphoreType.DMA((2,))],
        compiler_params=pltpu.CompilerParams(
            dimension_semantics=("arbitrary",), vmem_limit_bytes=VMEM_LIMIT),
        name="gather_combine_ln2",
    )(dest, dest_next, gates, h1, g, b, ys)


def _relayout_w_in(w):
    sizes = (256, 256, 512, 512, GATE_RANK, 256, 256, 512, 512)
    offs = np.concatenate([[0], np.cumsum(sizes)])
    gq, gk, gv, gg, glr, rq, rk, rv, rg = [w[:, offs[i]:offs[i + 1]] for i in range(9)]
    perm = np.zeros((256,), np.int32)
    for p in range(2):
        for l in range(LANES):
            part, hh, f = l // 64, (l % 64) // 32, l % 32
            perm[p * LANES + l] = (2 * p + hh) * DK + part * (DK // 2) + f
    glr = jnp.pad(glr, ((0, 0), (0, LANES - GATE_RANK)))
    return jnp.concatenate([gq, gk, gv, gg, glr, rq[:, perm], rk[:, perm], rv, rg], axis=1)


def kernel(x, positions, ln_in_g, ln_in_b, w_in, gla_gate_w, gla_gate_b, gla_norm_g, ret_norm_g,
           ret_norm_b, w_out, ln1_g, ln1_b, router_w, router_b, moe_w1, moe_b1, moe_w2, moe_b2,
           ln2_g, ln2_b):
    bsz, seq, d = x.shape
    assert d == D_MODEL and seq % GROUP == 0 and w_in.shape[0] == DEPTH == 1
    t = bsz * seq
    ts = 1024 if seq % 1024 == 0 else GROUP
    tm = min(512, t)
    assert seq % ts == 0 and t % tm == 0
    row = lambda v: v.reshape(1, -1).astype(F32)

    w_p = _relayout_w_in(w_in[0]).astype(MXU_DTYPE)
    h0, proj = _ln_proj_call(x.reshape(t, d), row(ln_in_g), row(ln_in_b), w_p, tm)

    gw = jnp.pad(gla_gate_w[0], ((0, LANES - GATE_RANK), (0, 0))).astype(MXU_DTYPE)
    half = DK // 2
    inv_freq = 1.0 / (ROPE_BASE ** np.linspace(0.0, 1.0, half, dtype=np.float32))
    invf = jnp.asarray(np.tile(inv_freq, LANES // half).reshape(1, LANES), F32)
    sgn = jnp.asarray(np.where(np.arange(LANES) < LANES // 2, -1.0, 1.0).reshape(1, LANES), F32)
    step_ang = np.arange(GROUP, dtype=np.float64)[:, None] * np.tile(inv_freq, LANES // half)[None, :]
    cd = jnp.asarray(np.cos(step_ang), F32)
    sd = jnp.asarray(np.sin(step_ang), F32)
    o = _mixer_call(proj, positions.reshape(t, 1), gw, row(gla_gate_b[0]), row(gla_norm_g[0]),
                    row(ret_norm_g[0]), row(ret_norm_b[0]), invf, sgn, cd, sd, bsz, seq, ts)

    rw = jnp.pad(router_w[0], ((0, 0), (0, LANES - N_EXPERTS))).astype(MXU_DTYPE)
    rb = jnp.pad(router_b[0], (0, LANES - N_EXPERTS)).reshape(1, LANES).astype(F32)
    h1, h1t, r_i, r_g, cnt = _post_mix_call(o, h0, w_out[0].astype(MXU_DTYPE), row(ln1_g[0]),
                                       row(ln1_b[0]), rw, rb, tm)

    e = r_i[:TOP_K].reshape(-1)
    rank = r_i[TOP_K:].reshape(-1)
    counts = cnt[0, :N_EXPERTS].astype(jnp.int32)
    padded = ((counts + MOE_BLOCK - 1) // MOE_BLOCK) * MOE_BLOCK
    pad_end = jnp.cumsum(padded)
    pad_start = pad_end - padded
    tk = t * TOP_K
    n_blocks = (tk + MOE_BLOCK - 1) // MOE_BLOCK + N_EXPERTS
    i32 = jnp.int32
    dest = rank
    for j in range(N_EXPERTS):
        dest = dest + jnp.where(e == j, pad_start[j], 0)
    dest = dest.reshape(TOP_K, t)
    block_start = jnp.arange(n_blocks, dtype=i32) * MOE_BLOCK
    block_expert = jnp.minimum(jnp.sum(block_start[:, None] >= pad_end[None, :], -1),
                               N_EXPERTS - 1).astype(i32)
    n_real = (pad_end[-1:] // MOE_BLOCK).astype(i32)

    n_slots = n_blocks * MOE_BLOCK
    hole_row = jnp.concatenate([pad_start + counts, pad_end[-1:]]).astype(i32)
    hole_n = jnp.concatenate([padded - counts, n_slots - pad_end[-1:]]).astype(i32)
    xs, w1c, w2c = _dispatch_call(hole_row, hole_n, dest, h1t,
                                  moe_w1[0].reshape(N_EXPERTS * D_MODEL, 2 * D_FF),
                                  moe_w2[0].reshape(N_EXPERTS * D_FF, D_MODEL), n_slots, DISPATCH_STEPS)
    ys = _expert_call(block_expert, n_real, xs,
                      w1c.reshape(N_EXPERTS, D_MODEL, 2 * D_FF), moe_b1[0].reshape(N_EXPERTS, 1, 2 * D_FF),
                      w2c.reshape(N_EXPERTS, D_FF, D_MODEL), moe_b2[0].reshape(N_EXPERTS, 1, D_MODEL))

    out = _gather_combine_call(dest, r_g, h1, row(ln2_g[0]), row(ln2_b[0]), ys, min(512, t))
    return out.reshape(bsz, seq, d)
```

```python
import functools
import math

import numpy as np
import jax
import jax.numpy as jnp
from jax import lax
from jax.experimental import pallas as pl
from jax.experimental.pallas import tpu as pltpu

F32 = jnp.float32
MXU_DTYPE = jnp.bfloat16

D_MODEL = 1024
CHUNK = 64
GC = 4
GROUP = GC * CHUNK
HEADS = 4
DK = 64
DV = 128
GATE_RANK = 16
GATE_NORM = 16.0
ROPE_BASE = 10000.0
N_EXPERTS = 32
TOP_K = 4
D_FF = 1024
SWIGLU_ALPHA = 1.702
SWIGLU_LIMIT = 7.0
MOE_BLOCK = 256
BLOCKS_PER_STEP = 2
DMA_UNROLL = 8
DISPATCH_STEPS = 64
LN_EPS = 1e-5
DEPTH = 1
DEEPNORM_ALPHA = (2.0 * DEPTH) ** 0.25

LANES = 128
VMEM_LIMIT = 56 * 1024 * 1024

C_GQ, C_GK, C_GV, C_GG, C_GLR = 0, 256, 512, 1024, 1536
C_RQ, C_RK, C_RV, C_RG = 1664, 1920, 2176, 2688
PROJ_W = 3200

LOG_GAMMA = [math.log1p(-(2.0 ** (-5.0 - h))) for h in range(HEADS)]


def _dot(a, b):
    return jnp.dot(a, b, preferred_element_type=F32)


def _dot_nt(a, b):
    return lax.dot_general(a, b, (((1,), (1,)), ((), ())), preferred_element_type=F32)


def _dot_tn(a, b):
    return lax.dot_general(a, b, (((0,), (0,)), ((), ())), preferred_element_type=F32)


def _mx(a):
    return a.astype(MXU_DTYPE)


def _layer_norm(x, g, b):
    mu = jnp.mean(x, -1, keepdims=True)
    xc = x - mu
    var = jnp.mean(xc * xc, -1, keepdims=True)
    return xc * lax.rsqrt(var + LN_EPS) * g + b


def _silu(x):
    return x * (1.0 / (1.0 + jnp.exp(-x)))


F32_ROWS = D_MODEL // LANES
PK_ROWS = D_MODEL // (2 * LANES)


def _load_token_tiles(ref, n, r):
    return jnp.concatenate([ref[pl.ds(j, n, stride=r), :] for j in range(r)], axis=1)


def _store_token_tiles(ref, rows):
    n, r = rows.shape[0], rows.shape[1] // LANES
    for j in range(r):
        ref[pl.ds(j, n, stride=r), :] = rows[:, j * LANES:(j + 1) * LANES]


def _pack_pairs(x):
    h = x.shape[1] // 2
    r = x.astype(jnp.bfloat16).astype(F32)
    lo = lax.bitcast_convert_type(r[:, :h], jnp.uint32) >> 16
    hi = lax.bitcast_convert_type(r[:, h:], jnp.uint32) & jnp.uint32(0xFFFF0000)
    return lo | hi


def _unpack_pairs(u):
    lo = lax.bitcast_convert_type(u << 16, F32)
    hi = lax.bitcast_convert_type(u & jnp.uint32(0xFFFF0000), F32)
    return jnp.concatenate([lo, hi], axis=1)


def _ln_proj_kernel(x_ref, g_ref, b_ref, w_ref, h_ref, p_ref):
    h = _layer_norm(x_ref[...], g_ref[...], b_ref[...])
    h_ref[...] = h
    p_ref[...] = _dot(_mx(h), w_ref[...])


def _ln_proj_call(x2, g, b, w, tm):
    t = x2.shape[0]
    return pl.pallas_call(
        _ln_proj_kernel,
        grid=(t // tm,),
        in_specs=[
            pl.BlockSpec((tm, D_MODEL), lambda i: (i, 0)),
            pl.BlockSpec((1, D_MODEL), lambda i: (0, 0)),
            pl.BlockSpec((1, D_MODEL), lambda i: (0, 0)),
            pl.BlockSpec((D_MODEL, PROJ_W), lambda i: (0, 0)),
        ],
        out_specs=[
            pl.BlockSpec((tm, D_MODEL), lambda i: (i, 0)),
            pl.BlockSpec((tm, PROJ_W), lambda i: (i, 0)),
        ],
        out_shape=[
            jax.ShapeDtypeStruct((t, D_MODEL), F32),
            jax.ShapeDtypeStruct((t, PROJ_W), F32),
        ],
        compiler_params=pltpu.CompilerParams(
            dimension_semantics=("arbitrary",), vmem_limit_bytes=VMEM_LIMIT),
        name="ln_proj",
    )(x2, g, b, w)


def _mixer_kernel(p_ref, pos_ref, gw_ref, gb_ref, gn_ref, rng_ref, rnb_ref, invf_ref, sgn_ref,
                  cd_ref, sd_ref, o_ref, st_ref, *, n_groups):
    @pl.when(pl.program_id(1) == 0)
    def _():
        st_ref[...] = jnp.zeros_like(st_ref)

    rr = lax.broadcasted_iota(jnp.int32, (GROUP, GROUP), 0)
    cc = lax.broadcasted_iota(jnp.int32, (GROUP, GROUP), 1)
    same = (rr >> 6) == (cc >> 6)
    lower = same & (rr >= cc)
    upper = same & (rr < cc)
    tri = lower.astype(MXU_DTYPE)
    dist = jnp.abs(rr - cc).astype(F32)
    rowi = lax.broadcasted_iota(jnp.int32, (GROUP, LANES), 0)
    rin = (rowi & (CHUNK - 1)).astype(F32)
    in_chunk = [(rowi >> 6) == c for c in range(GC)]
    lane = lax.broadcasted_iota(jnp.int32, (GROUP, LANES), 1)
    gla_half = [((lane >> 6) & 1) == i for i in range(2)]
    ret_half = [((lane >> 5) & 1) == i for i in range(2)]
    ret_d = [jnp.where(same, jnp.exp(LOG_GAMMA[h] * dist), 0.0) for h in range(HEADS)]
    ret_eb = [jnp.exp(LOG_GAMMA[h] * (rin + 1.0)) for h in range(HEADS)]
    ret_ek = [jnp.exp(LOG_GAMMA[h] * (CHUNK - 1.0 - rin)) for h in range(HEADS)]
    ret_dec = [math.exp(LOG_GAMMA[h] * CHUNK) for h in range(HEADS)]

    def block_diag(x):
        return _mx(jnp.concatenate([jnp.where(in_chunk[c], x, 0.0) for c in range(GC)], axis=1))

    lower2 = jnp.concatenate([lower, lower], axis=0)
    upper2 = jnp.concatenate([upper, upper], axis=0)
    ret_d2 = [jnp.concatenate([ret_d[2 * p], ret_d[2 * p + 1]], axis=0) for p in range(2)]
    zeros_v = jnp.zeros((GROUP, DV), MXU_DTYPE)

    def pair_products(hs0, q_m, sc, v2, v2_kv, k_bd, decay):
        sc_b = _mx(sc)
        v_bd = jnp.concatenate([jnp.concatenate([v2[:, :DV], zeros_v], axis=1),
                                jnp.concatenate([zeros_v, v2[:, DV:]], axis=1)], axis=0)
        o_intra = _dot(jnp.concatenate([sc_b[:GROUP], sc_b[GROUP:]], axis=1), v_bd)
        kv_t = _dot_tn(v2_kv, k_bd)
        st = [st_ref[hs0], st_ref[hs0 + 1]]
        parts = []
        for c in range(GC):
            cr = slice(c * CHUNK, (c + 1) * CHUNK)
            qc = jnp.concatenate([q_m[0][cr], q_m[1][cr]], axis=0)
            r = _dot_nt(qc, _mx(jnp.concatenate(st, axis=0)))
            parts.append(jnp.concatenate([r[:CHUNK, :DV], r[CHUNK:, DV:]], axis=1))
            for i in range(2):
                st[i] = st[i] * decay(c, i) + kv_t[i * DV:(i + 1) * DV, c * LANES:(c + 1) * LANES]
        st_ref[hs0] = st[0]
        st_ref[hs0 + 1] = st[1]
        return o_intra, jnp.concatenate(parts, axis=0)

    def group_body(g, carry):
        rows = pl.ds(pl.multiple_of(g * GROUP, GROUP), GROUP)

        z = _dot(_mx(p_ref[rows, C_GLR:C_GLR + LANES]), gw_ref[...]) + gb_ref[...]
        la = (jnp.minimum(z, 0.0) - jnp.log1p(jnp.exp(-jnp.abs(z)))) * (1.0 / GATE_NORM)
        la_hi = _mx(la)
        r1 = la - la_hi.astype(F32)
        la_mid = _mx(r1)
        la_lo = _mx(r1 - la_mid.astype(F32))
        b3 = _dot(tri, jnp.concatenate([la_hi, la_mid, la_lo], axis=1))
        b = b3[:, :256] + b3[:, 256:512] + b3[:, 512:]
        b_ends = [b[(c + 1) * CHUNK - 1:(c + 1) * CHUNK, :] for c in range(GC)]
        b_last = jnp.concatenate([jnp.broadcast_to(e, (CHUNK, 256)) for e in b_ends], axis=0)
        decs = [jnp.exp(e) for e in b_ends]
        eb = jnp.exp(b)
        enb = jnp.exp(-b)
        ekv = jnp.exp(b_last - b)
        q = p_ref[rows, C_GQ:C_GQ + 256] * (DK ** -0.5)
        k = p_ref[rows, C_GK:C_GK + 256]
        qe, qn = q * eb, q * enb
        ke, kn, kk = k * eb, k * enb, k * ekv
        for p in range(2):
            ls = slice(p * LANES, (p + 1) * LANES)
            qe_m = [_mx(jnp.where(gla_half[i], qe[:, ls], 0.0)) for i in range(2)]
            qn_m = [_mx(jnp.where(gla_half[i], qn[:, ls], 0.0)) for i in range(2)]
            s_lo = _dot_nt(jnp.concatenate(qe_m, axis=0), _mx(kn[:, ls]))
            s_up = _dot_nt(jnp.concatenate(qn_m, axis=0), _mx(ke[:, ls]))
            sc = jnp.where(lower2, s_lo, jnp.where(upper2, s_up, 0.0))
            v2 = _mx(p_ref[rows, C_GV + 2 * p * DV:C_GV + (2 * p + 2) * DV])
            o_intra, o_inter = pair_products(2 * p, qe_m, sc, v2, v2, block_diag(kk[:, ls]),
                                             lambda c, i: decs[c][:, ls])
            o2 = o_intra + o_inter
            for i in range(2):
                h = 2 * p + i
                o = o2[:, i * DV:(i + 1) * DV]
                o = o * lax.rsqrt(jnp.mean(o * o, -1, keepdims=True) + LN_EPS) * gn_ref[...]
                o = o * _silu(p_ref[rows, C_GG + h * DV:C_GG + (h + 1) * DV])
                o_ref[rows, h * DV:(h + 1) * DV] = o.astype(o_ref.dtype)

        base = pos_ref[pl.ds(pl.multiple_of(g * GROUP, GROUP), 1), :].astype(F32) * invf_ref[...]
        c0, s0 = jnp.cos(base), jnp.sin(base)
        cs = c0 * cd_ref[...] - s0 * sd_ref[...]
        sn = (s0 * sgn_ref[...]) * cd_ref[...] + (c0 * sgn_ref[...]) * sd_ref[...]
        for p in range(2):
            tq = p_ref[rows, C_RQ + p * LANES:C_RQ + (p + 1) * LANES]
            tk = p_ref[rows, C_RK + p * LANES:C_RK + (p + 1) * LANES] * (DK ** -0.5)
            rq = tq * cs + pltpu.roll(tq, LANES // 2, 1) * sn
            rk = tk * cs + pltpu.roll(tk, LANES // 2, 1) * sn
            q_m = [_mx(jnp.where(ret_half[i], rq, 0.0)) for i in range(2)]
            s = _dot_nt(jnp.concatenate(q_m, axis=0), _mx(rk)) * ret_d2[p]
            v2f = p_ref[rows, C_RV + 2 * p * DV:C_RV + (2 * p + 2) * DV]
            v2_kv = jnp.concatenate([v2f[:, :DV] * ret_ek[2 * p], v2f[:, DV:] * ret_ek[2 * p + 1]], axis=1)
            o_intra, o_inter = pair_products(HEADS + 2 * p, q_m, s, _mx(v2f), _mx(v2_kv), block_diag(rk),
                                             lambda c, i: ret_dec[2 * p + i])
            for i in range(2):
                h = 2 * p + i
                o = o_intra[:, i * DV:(i + 1) * DV] + ret_eb[h] * o_inter[:, i * DV:(i + 1) * DV]
                mu = jnp.mean(o, -1, keepdims=True)
                oc = o - mu
                var = jnp.mean(oc * oc, -1, keepdims=True)
                o = oc * lax.rsqrt(var + LN_EPS) * rng_ref[:, h * DV:(h + 1) * DV] \
                    + rnb_ref[:, h * DV:(h + 1) * DV]
                o = o * _silu(p_ref[rows, C_RG + h * DV:C_RG + (h + 1) * DV])
                o_ref[rows, (HEADS + h) * DV:(HEADS + h + 1) * DV] = o.astype(o_ref.dtype)
        return carry

    lax.fori_loop(0, n_groups, group_body, 0)


def _mixer_call(proj, pos, gw, gb, gn, rng, rnb, invf, sgn, cd, sd, bsz, seq, ts):
    t = bsz * seq
    nst = seq // ts
    const = lambda b, s: (0, 0)
    return pl.pallas_call(
        functools.partial(_mixer_kernel, n_groups=ts // GROUP),
        grid=(bsz, nst),
        in_specs=[
            pl.BlockSpec((ts, PROJ_W), lambda b, s: (b * nst + s, 0)),
            pl.BlockSpec((ts, 1), lambda b, s: (b * nst + s, 0)),
            pl.BlockSpec((LANES, 256), const),
            pl.BlockSpec((1, 256), const),
            pl.BlockSpec((1, DV), const),
            pl.BlockSpec((1, HEADS * DV), const),
            pl.BlockSpec((1, HEADS * DV), const),
            pl.BlockSpec((1, LANES), const),
            pl.BlockSpec((1, LANES), const),
            pl.BlockSpec((GROUP, LANES), const),
            pl.BlockSpec((GROUP, LANES), const),
        ],
        out_specs=pl.BlockSpec((ts, D_MODEL), lambda b, s: (b * nst + s, 0)),
        out_shape=jax.ShapeDtypeStruct((t, D_MODEL), MXU_DTYPE),
        scratch_shapes=[pltpu.VMEM((2 * HEADS, DV, LANES), F32)],
        compiler_params=pltpu.CompilerParams(
            dimension_semantics=("arbitrary", "arbitrary"), vmem_limit_bytes=VMEM_LIMIT),
        name="mixers",
    )(proj, pos, gw, gb, gn, rng, rnb, invf, sgn, cd, sd)


def _post_mix_kernel(o_ref, h0_ref, wo_ref, g_ref, b_ref, rwt_ref, rbt_ref,
                     h1_ref, h1t_ref, ri_ref, rg_ref, cnt_ref, triu_ref, carry_ref, *, tm):
    @pl.when(pl.program_id(0) == 0)
    def _():
        r = lax.broadcasted_iota(jnp.int32, (tm, tm), 0)
        c = lax.broadcasted_iota(jnp.int32, (tm, tm), 1)
        triu_ref[...] = (r < c).astype(MXU_DTYPE)
        carry_ref[...] = jnp.zeros_like(carry_ref)

    mix = _dot(o_ref[...], wo_ref[...])
    h1 = _layer_norm(DEEPNORM_ALPHA * h0_ref[...] + mix, g_ref[...], b_ref[...])
    h1_ref[...] = h1
    _store_token_tiles(h1t_ref, _pack_pairs(h1))

    l = (_dot_nt(rwt_ref[...], _mx(h1)) + rbt_ref[...])[:N_EXPERTS]
    erow = lax.broadcasted_iota(jnp.int32, (N_EXPERTS, tm), 0).astype(F32)
    vals, idxs = [], []
    for _ in range(TOP_K):
        m = jnp.max(l, 0, keepdims=True)
        i = jnp.min(jnp.where(l == m, erow, float(LANES)), 0, keepdims=True)
        vals.append(m)
        idxs.append(i)
        l = jnp.where(erow == i, -jnp.inf, l)
    exps = [jnp.exp(v - vals[0]) for v in vals]
    inv = 1.0 / (exps[0] + exps[1] + exps[2] + exps[3])

    onehot = jnp.zeros((N_EXPERTS, tm), F32)
    for i in idxs:
        onehot = onehot + (erow == i).astype(F32)
    before = _dot(_mx(onehot), triu_ref[...]) + carry_ref[:, :1]
    ranks = [jnp.sum(jnp.where(erow == i, before, 0.0), 0, keepdims=True) for i in idxs]
    ri_ref[...] = jnp.concatenate(idxs + ranks, axis=0).astype(jnp.int32)
    gates_t = jnp.concatenate([e * inv for e in exps] + [jnp.zeros((LANES - TOP_K, tm), F32)], axis=0)
    rg_ref[...] = jnp.transpose(gates_t)
    carry_ref[...] = carry_ref[...] + jnp.sum(onehot, 1, keepdims=True)
    cnt_ref[...] = carry_ref[...]


def _post_mix_call(o, h0, wo, g, b, rwt, rbt, tm):
    t = o.shape[0]
    const = lambda i: (0, 0)
    return pl.pallas_call(
        functools.partial(_post_mix_kernel, tm=tm),
        grid=(t // tm,),
        in_specs=[
            pl.BlockSpec((tm, D_MODEL), lambda i: (i, 0)),
            pl.BlockSpec((tm, D_MODEL), lambda i: (i, 0)),
            pl.BlockSpec((D_MODEL, D_MODEL), const),
            pl.BlockSpec((1, D_MODEL), const),
            pl.BlockSpec((1, D_MODEL), const),
            pl.BlockSpec((LANES, D_MODEL), const),
            pl.BlockSpec((LANES, 1), const),
        ],
        out_specs=[
            pl.BlockSpec((tm, D_MODEL), lambda i: (i, 0)),
            pl.BlockSpec((tm * PK_ROWS, LANES), lambda i: (i, 0)),
            pl.BlockSpec((2 * TOP_K, tm), lambda i: (0, i)),
            pl.BlockSpec((tm, LANES), lambda i: (i, 0)),
            pl.BlockSpec((N_EXPERTS, LANES), const),
        ],
        out_shape=[
            jax.ShapeDtypeStruct((t, D_MODEL), F32),
            jax.ShapeDtypeStruct((t * PK_ROWS, LANES), jnp.uint32),
            jax.ShapeDtypeStruct((2 * TOP_K, t), jnp.int32),
            jax.ShapeDtypeStruct((t, LANES), F32),
            jax.ShapeDtypeStruct((N_EXPERTS, LANES), F32),
        ],
        scratch_shapes=[pltpu.VMEM((tm, tm), MXU_DTYPE), pltpu.VMEM((N_EXPERTS, LANES), F32)],
        compiler_params=pltpu.CompilerParams(
            dimension_semantics=("arbitrary",), vmem_limit_bytes=VMEM_LIMIT),
        name="post_mix_router",
    )(o, h0, wo, g, b, rwt, rbt)


def _step_tables(dest, tm):
    k, t = dest.shape
    return dest.reshape(k, t // tm, tm).transpose(1, 0, 2).reshape(t // tm, 1, k * tm)


def _dispatch_kernel(pad_row_ref, pad_n_ref, dest_ref, h1p_ref, w1_ref, w2_ref,
                     xs_hbm, w1c_ref, w2c_ref, stage_ref, zero_ref, sem, *, tm):
    i = pl.program_id(0)

    def tile(ref, t):
        return ref.at[pl.ds(pl.multiple_of(t * PK_ROWS, PK_ROWS), PK_ROWS), :]

    slot = i % 2
    rows = tm * PK_ROWS
    stage = stage_ref.at[pl.ds(pl.multiple_of(slot * rows, rows), rows), :]
    stage[...] = h1p_ref[...]

    def body(it, c):
        for u in range(DMA_UNROLL):
            r = it * DMA_UNROLL + u
            for k in range(TOP_K):
                pltpu.make_async_copy(tile(stage, r), tile(xs_hbm, dest_ref[0, 0, k * tm + r]),
                                      sem.at[slot]).start(priority=k % 2)
        return c
    lax.fori_loop(0, tm // DMA_UNROLL, body, 0)
    w1c_ref[...] = _mx(w1_ref[...])
    w2c_ref[...] = _mx(w2_ref[...])

    def drain(s):
        for _ in range(TOP_K):
            pltpu.make_async_copy(h1p_ref, xs_hbm.at[pl.ds(0, rows), :], sem.at[s]).wait()

    @pl.when(i > 0)
    def _():
        drain(1 - slot)

    @pl.when(i == pl.num_programs(0) - 1)
    def _():
        drain(slot)
        zero_ref[...] = jnp.zeros_like(zero_ref)
        zero = zero_ref.at[pl.ds(0, PK_ROWS), :]

        def per_expert(e, c):
            def start(r, c2):
                pltpu.make_async_copy(zero, tile(xs_hbm, pad_row_ref[e] + r), sem.at[2]).start()
                return c2
            lax.fori_loop(0, pad_n_ref[e], start, 0)

            def wait(r, c2):
                pltpu.make_async_copy(zero, tile(xs_hbm, 0), sem.at[2]).wait()
                return c2
            lax.fori_loop(0, pad_n_ref[e], wait, 0)
            return c
        lax.fori_loop(0, pad_row_ref.shape[0], per_expert, 0)


def _dispatch_call(pad_row, pad_n, dest, h1p, w1, w2, n_slots, n_steps):
    t = h1p.shape[0] // PK_ROWS
    tm = t // n_steps
    r1, r2 = w1.shape[0] // n_steps, w2.shape[0] // n_steps
    step = lambda i, pr, pn: (i, 0)
    grid_spec = pltpu.PrefetchScalarGridSpec(
        num_scalar_prefetch=2,
        grid=(n_steps,),
        in_specs=[
            pl.BlockSpec((1, 1, TOP_K * tm), lambda i, pr, pn: (i, 0, 0), memory_space=pltpu.SMEM),
            pl.BlockSpec((tm * PK_ROWS, LANES), step),
            pl.BlockSpec((r1, w1.shape[1]), step),
            pl.BlockSpec((r2, w2.shape[1]), step),
        ],
        out_specs=[
            pl.BlockSpec(memory_space=pl.ANY),
            pl.BlockSpec((r1, w1.shape[1]), step),
            pl.BlockSpec((r2, w2.shape[1]), step),
        ],
        scratch_shapes=[pltpu.VMEM((2 * tm * PK_ROWS, LANES), jnp.uint32),
                        pltpu.VMEM((F32_ROWS, LANES), jnp.uint32), pltpu.SemaphoreType.DMA((3,))],
    )
    return pl.pallas_call(
        functools.partial(_dispatch_kernel, tm=tm),
        grid_spec=grid_spec,
        out_shape=[
            jax.ShapeDtypeStruct((n_slots * PK_ROWS, LANES), jnp.uint32),
            jax.ShapeDtypeStruct(w1.shape, MXU_DTYPE),
            jax.ShapeDtypeStruct(w2.shape, MXU_DTYPE),
        ],
        compiler_params=pltpu.CompilerParams(
            dimension_semantics=("arbitrary",), vmem_limit_bytes=VMEM_LIMIT),
        name="dispatch",
    )(pad_row, pad_n, _step_tables(dest, tm), h1p, w1, w2)


def _expert_kernel(be_ref, nr_ref, x_ref, *refs):
    y_ref = refs[-1]
    rows = MOE_BLOCK * PK_ROWS
    for half in range(BLOCKS_PER_STEP):
        w1_ref, b1_ref, w2_ref, b2_ref = refs[4 * half:4 * half + 4]
        g = pl.program_id(0) * BLOCKS_PER_STEP + half
        x_blk = x_ref.at[pl.ds(half * rows, rows), :]
        y_blk = y_ref.at[pl.ds(half * rows, rows), :]

        @pl.when(g >= nr_ref[0])
        def _():
            y_blk[...] = jnp.zeros_like(y_blk)

        @pl.when(g < nr_ref[0])
        def _():
            x = _mx(_unpack_pairs(_load_token_tiles(x_blk, MOE_BLOCK, PK_ROWS)))
            hh = _dot(x, w1_ref[0]) + b1_ref[0]
            x_glu = jnp.minimum(hh[:, :D_FF], SWIGLU_LIMIT)
            x_lin = jnp.clip(hh[:, D_FF:], -SWIGLU_LIMIT, SWIGLU_LIMIT)
            act = x_glu * (1.0 / (1.0 + jnp.exp(-SWIGLU_ALPHA * x_glu))) * (x_lin + 1.0)
            _store_token_tiles(y_blk, _pack_pairs(_dot(_mx(act), w2_ref[0]) + b2_ref[0]))


def _expert_call(block_expert, n_real, xs, w1, b1, w2, b2):
    n_blocks = block_expert.shape[0]
    assert n_blocks % BLOCKS_PER_STEP == 0
    step = lambda i, be, nr: (i, 0)
    rows = BLOCKS_PER_STEP * MOE_BLOCK * PK_ROWS
    expert_specs, expert_args = [], []
    for half in range(BLOCKS_PER_STEP):
        ex = functools.partial(lambda i, be, nr, half: (be[i * BLOCKS_PER_STEP + half], 0, 0), half=half)
        expert_specs += [pl.BlockSpec((1, D_MODEL, 2 * D_FF), ex), pl.BlockSpec((1, 1, 2 * D_FF), ex),
                         pl.BlockSpec((1, D_FF, D_MODEL), ex), pl.BlockSpec((1, 1, D_MODEL), ex)]
        expert_args += [w1, b1, w2, b2]
    grid_spec = pltpu.PrefetchScalarGridSpec(
        num_scalar_prefetch=2,
        grid=(n_blocks // BLOCKS_PER_STEP,),
        in_specs=[pl.BlockSpec((rows, LANES), step)] + expert_specs,
        out_specs=pl.BlockSpec((rows, LANES), step),
    )
    return pl.pallas_call(
        _expert_kernel,
        grid_spec=grid_spec,
        out_shape=jax.ShapeDtypeStruct((n_blocks * MOE_BLOCK * PK_ROWS, LANES), jnp.uint32),
        compiler_params=pltpu.CompilerParams(
            dimension_semantics=("arbitrary",), vmem_limit_bytes=VMEM_LIMIT),
        name="moe_experts",
    )(block_expert, n_real, xs, *expert_args)


def _gather_combine_kernel(dest0_ref, destn_ref, gate_ref, h1_ref, g_ref, b_ref, ys_hbm,
                           out_ref, ybuf, sem, *, tm):
    i = pl.program_id(0)
    slot = i % 2
    rows = tm * PK_ROWS

    def tile(ref, t):
        return ref.at[pl.ds(pl.multiple_of(t * PK_ROWS, PK_ROWS), PK_ROWS), :]

    def gather(tbl, s):
        def body(it, c):
            for u in range(DMA_UNROLL):
                r = it * DMA_UNROLL + u
                for k in range(TOP_K):
                    pltpu.make_async_copy(tile(ys_hbm, tbl[0, 0, k * tm + r]),
                                          tile(ybuf, (s * TOP_K + k) * tm + r), sem.at[s]).start(priority=k % 2)
            return c
        lax.fori_loop(0, tm // DMA_UNROLL, body, 0)

    @pl.when(i == 0)
    def _():
        gather(dest0_ref, 0)

    @pl.when(i + 1 < pl.num_programs(0))
    def _():
        gather(destn_ref, 1 - slot)

    for k in range(TOP_K):
        plane = ybuf.at[pl.ds(pl.multiple_of((slot * TOP_K + k) * rows, rows), rows), :]
        pltpu.make_async_copy(ys_hbm.at[pl.ds(0, rows), :], plane, sem.at[slot]).wait()

    gates = gate_ref[...]
    acc = DEEPNORM_ALPHA * h1_ref[...]
    for k in range(TOP_K):
        plane = ybuf.at[pl.ds(pl.multiple_of((slot * TOP_K + k) * rows, rows), rows), :]
        acc = acc + gates[:, k:k + 1] * _unpack_pairs(_load_token_tiles(plane, tm, PK_ROWS))
    out_ref[...] = _layer_norm(acc, g_ref[...], b_ref[...])


def _gather_combine_call(dest, gates, h1, g, b, ys, tm):
    t = h1.shape[0]
    nt = t // tm
    const = lambda i: (0, 0)
    smem = functools.partial(pl.BlockSpec, memory_space=pltpu.SMEM)
    dest = _step_tables(dest, tm)
    dest_next = jnp.concatenate([dest[1:], dest[:1]], axis=0)
    return pl.pallas_call(
        functools.partial(_gather_combine_kernel, tm=tm),
        grid=(nt,),
        in_specs=[
            smem((1, 1, TOP_K * tm), lambda i: (0, 0, 0)),
            smem((1, 1, TOP_K * tm), lambda i: (i, 0, 0)),
            pl.BlockSpec((tm, LANES), lambda i: (i, 0)),
            pl.BlockSpec((tm, D_MODEL), lambda i: (i, 0)),
            pl.BlockSpec((1, D_MODEL), const),
            pl.BlockSpec((1, D_MODEL), const),
            pl.BlockSpec(memory_space=pl.ANY),
        ],
        out_specs=pl.BlockSpec((tm, D_MODEL), lambda i: (i, 0)),
        out_shape=jax.ShapeDtypeStruct((t, D_MODEL), F32),
        scratch_shapes=[pltpu.VMEM((2 * TOP_K * tm * PK_ROWS, LANES), jnp.uint32),
                        pltpu.SemaphoreType.DMA((2,))],
        compiler_params=pltpu.CompilerParams(
            dimension_semantics=("arbitrary",), vmem_limit_bytes=VMEM_LIMIT),
        name="gather_combine_ln2",
    )(dest, dest_next, gates, h1, g, b, ys)


def _relayout_w_in(w):
    sizes = (256, 256, 512, 512, GATE_RANK, 256, 256, 512, 512)
    offs = np.concatenate([[0], np.cumsum(sizes)])
    gq, gk, gv, gg, glr, rq, rk, rv, rg = [w[:, offs[i]:offs[i + 1]] for i in range(9)]
    perm = np.zeros((256,), np.int32)
    for p in range(2):
        for l in range(LANES):
            part, hh, f = l // 64, (l % 64) // 32, l % 32
            perm[p * LANES + l] = (2 * p + hh) * DK + part * (DK // 2) + f
    glr = jnp.pad(glr, ((0, 0), (0, LANES - GATE_RANK)))
    return jnp.concatenate([gq, gk, gv, gg, glr, rq[:, perm], rk[:, perm], rv, rg], axis=1)


def kernel(x, positions, ln_in_g, ln_in_b, w_in, gla_gate_w, gla_gate_b, gla_norm_g, ret_norm_g,
           ret_norm_b, w_out, ln1_g, ln1_b, router_w, router_b, moe_w1, moe_b1, moe_w2, moe_b2,
           ln2_g, ln2_b):
    bsz, seq, d = x.shape
    assert d == D_MODEL and seq % GROUP == 0 and w_in.shape[0] == DEPTH == 1
    t = bsz * seq
    ts = 1024 if seq % 1024 == 0 else GROUP
    tm = min(512, t)
    assert seq % ts == 0 and t % tm == 0
    row = lambda v: v.reshape(1, -1).astype(F32)

    w_p = _relayout_w_in(w_in[0]).astype(MXU_DTYPE)
    h0, proj = _ln_proj_call(x.reshape(t, d), row(ln_in_g), row(ln_in_b), w_p, tm)

    gw = jnp.pad(gla_gate_w[0], ((0, LANES - GATE_RANK), (0, 0))).astype(MXU_DTYPE)
    half = DK // 2
    inv_freq = 1.0 / (ROPE_BASE ** np.linspace(0.0, 1.0, half, dtype=np.float32))
    invf = jnp.asarray(np.tile(inv_freq, LANES // half).reshape(1, LANES), F32)
    sgn = jnp.asarray(np.where(np.arange(LANES) < LANES // 2, -1.0, 1.0).reshape(1, LANES), F32)
    step_ang = np.arange(GROUP, dtype=np.float64)[:, None] * np.tile(inv_freq, LANES // half)[None, :]
    cd = jnp.asarray(np.cos(step_ang), F32)
    sd = jnp.asarray(np.sin(step_ang), F32)
    o = _mixer_call(proj, positions.reshape(t, 1), gw, row(gla_gate_b[0]), row(gla_norm_g[0]),
                    row(ret_norm_g[0]), row(ret_norm_b[0]), invf, sgn, cd, sd, bsz, seq, ts)

    rwt = jnp.pad(router_w[0].T, ((0, LANES - N_EXPERTS), (0, 0))).astype(MXU_DTYPE)
    rbt = jnp.pad(router_b[0], (0, LANES - N_EXPERTS)).reshape(LANES, 1).astype(F32)
    h1, h1t, r_i, r_g, cnt = _post_mix_call(o, h0, w_out[0].astype(MXU_DTYPE), row(ln1_g[0]),
                                       row(ln1_b[0]), rwt, rbt, tm)

    e = r_i[:TOP_K].reshape(-1)
    rank = r_i[TOP_K:].reshape(-1)
    counts = cnt[:, 0].astype(jnp.int32)
    padded = ((counts + MOE_BLOCK - 1) // MOE_BLOCK) * MOE_BLOCK
    pad_end = jnp.cumsum(padded)
    pad_start = pad_end - padded
    tk = t * TOP_K
    n_blocks = (tk + MOE_BLOCK - 1) // MOE_BLOCK + N_EXPERTS
    i32 = jnp.int32
    dest = rank
    for j in range(N_EXPERTS):
        dest = dest + jnp.where(e == j, pad_start[j], 0)
    dest = dest.reshape(TOP_K, t)
    block_start = jnp.arange(n_blocks, dtype=i32) * MOE_BLOCK
    block_expert = jnp.minimum(jnp.sum(block_start[:, None] >= pad_end[None, :], -1),
                               N_EXPERTS - 1).astype(i32)
    n_real = (pad_end[-1:] // MOE_BLOCK).astype(i32)

    n_slots = n_blocks * MOE_BLOCK
    hole_row = jnp.concatenate([pad_start + counts, pad_end[-1:]]).astype(i32)
    hole_n = jnp.concatenate([padded - counts, n_slots - pad_end[-1:]]).astype(i32)
    xs, w1c, w2c = _dispatch_call(hole_row, hole_n, dest, h1t,
                                  moe_w1[0].reshape(N_EXPERTS * D_MODEL, 2 * D_FF),
                                  moe_w2[0].reshape(N_EXPERTS * D_FF, D_MODEL), n_slots, DISPATCH_STEPS)
    ys = _expert_call(block_expert, n_real, xs,
                      w1c.reshape(N_EXPERTS, D_MODEL, 2 * D_FF), moe_b1[0].reshape(N_EXPERTS, 1, 2 * D_FF),
                      w2c.reshape(N_EXPERTS, D_FF, D_MODEL), moe_b2[0].reshape(N_EXPERTS, 1, D_MODEL))

    out = _gather_combine_call(dest, r_g, h1, row(ln2_g[0]), row(ln2_b[0]), ys, min(512, t))
    return out.reshape(bsz, seq, d)
```

```python
import functools
import math

import numpy as np
import jax
import jax.numpy as jnp
from jax import lax
from jax.experimental import pallas as pl
from jax.experimental.pallas import tpu as pltpu

F32 = jnp.float32
MXU_DTYPE = jnp.bfloat16

D_MODEL = 1024
CHUNK = 64
GC = 4
GROUP = GC * CHUNK
HEADS = 4
DK = 64
DV = 128
GATE_RANK = 16
GATE_NORM = 16.0
ROPE_BASE = 10000.0
N_EXPERTS = 32
TOP_K = 4
D_FF = 1024
SWIGLU_ALPHA = 1.702
SWIGLU_LIMIT = 7.0
MOE_BLOCK = 256
LN_PROJ_SLABS = 4
BLOCKS_PER_STEP = 2
DMA_UNROLL = 8
DISPATCH_STEPS = 64
LN_EPS = 1e-5
DEPTH = 1
DEEPNORM_ALPHA = (2.0 * DEPTH) ** 0.25

LANES = 128
VMEM_LIMIT = 56 * 1024 * 1024

C_GQ, C_GK, C_GV, C_GG, C_GLR = 0, 256, 512, 1024, 1536
C_RQ, C_RK, C_RV, C_RG = 1664, 1920, 2176, 2688
PROJ_W = 3200

LOG_GAMMA = [math.log1p(-(2.0 ** (-5.0 - h))) for h in range(HEADS)]


def _dot(a, b):
    return jnp.dot(a, b, preferred_element_type=F32)


def _dot_nt(a, b):
    return lax.dot_general(a, b, (((1,), (1,)), ((), ())), preferred_element_type=F32)


def _dot_tn(a, b):
    return lax.dot_general(a, b, (((0,), (0,)), ((), ())), preferred_element_type=F32)


def _mx(a):
    return a.astype(MXU_DTYPE)


def _layer_norm(x, g, b):
    mu = jnp.mean(x, -1, keepdims=True)
    xc = x - mu
    var = jnp.mean(xc * xc, -1, keepdims=True)
    return xc * lax.rsqrt(var + LN_EPS) * g + b


def _silu(x):
    return x * (1.0 / (1.0 + jnp.exp(-x)))


F32_ROWS = D_MODEL // LANES
PK_ROWS = D_MODEL // (2 * LANES)


def _load_token_tiles(ref, n, r):
    return jnp.concatenate([ref[pl.ds(j, n, stride=r), :] for j in range(r)], axis=1)


def _store_token_tiles(ref, rows):
    n, r = rows.shape[0], rows.shape[1] // LANES
    for j in range(r):
        ref[pl.ds(j, n, stride=r), :] = rows[:, j * LANES:(j + 1) * LANES]


def _pack_pairs(x):
    h = x.shape[1] // 2
    r = x.astype(jnp.bfloat16).astype(F32)
    lo = lax.bitcast_convert_type(r[:, :h], jnp.uint32) >> 16
    hi = lax.bitcast_convert_type(r[:, h:], jnp.uint32) & jnp.uint32(0xFFFF0000)
    return lo | hi


def _unpack_pairs(u):
    lo = lax.bitcast_convert_type(u << 16, F32)
    hi = lax.bitcast_convert_type(u & jnp.uint32(0xFFFF0000), F32)
    return jnp.concatenate([lo, hi], axis=1)


def _ln_proj_kernel(x_ref, g_ref, b_ref, w_ref, h_ref, p_ref):
    n = x_ref.shape[0] // LN_PROJ_SLABS
    for j in range(LN_PROJ_SLABS):
        rs = slice(j * n, (j + 1) * n)
        h = _layer_norm(x_ref[rs, :], g_ref[...], b_ref[...])
        h_ref[rs, :] = h
        p_ref[rs, :] = _dot(_mx(h), w_ref[...])


def _ln_proj_call(x2, g, b, w, tm):
    t = x2.shape[0]
    return pl.pallas_call(
        _ln_proj_kernel,
        grid=(t // tm,),
        in_specs=[
            pl.BlockSpec((tm, D_MODEL), lambda i: (i, 0)),
            pl.BlockSpec((1, D_MODEL), lambda i: (0, 0)),
            pl.BlockSpec((1, D_MODEL), lambda i: (0, 0)),
            pl.BlockSpec((D_MODEL, PROJ_W), lambda i: (0, 0)),
        ],
        out_specs=[
            pl.BlockSpec((tm, D_MODEL), lambda i: (i, 0)),
            pl.BlockSpec((tm, PROJ_W), lambda i: (i, 0)),
        ],
        out_shape=[
            jax.ShapeDtypeStruct((t, D_MODEL), F32),
            jax.ShapeDtypeStruct((t, PROJ_W), F32),
        ],
        compiler_params=pltpu.CompilerParams(
            dimension_semantics=("arbitrary",), vmem_limit_bytes=VMEM_LIMIT),
        name="ln_proj",
    )(x2, g, b, w)


def _mixer_kernel(p_ref, pos_ref, gw_ref, gb_ref, gn_ref, rng_ref, rnb_ref, invf_ref, sgn_ref,
                  cd_ref, sd_ref, o_ref, st_ref, *, n_groups):
    @pl.when(pl.program_id(1) == 0)
    def _():
        st_ref[...] = jnp.zeros_like(st_ref)

    rr = lax.broadcasted_iota(jnp.int32, (GROUP, GROUP), 0)
    cc = lax.broadcasted_iota(jnp.int32, (GROUP, GROUP), 1)
    same = (rr >> 6) == (cc >> 6)
    lower = same & (rr >= cc)
    upper = same & (rr < cc)
    tri = lower.astype(MXU_DTYPE)
    dist = jnp.abs(rr - cc).astype(F32)
    rowi = lax.broadcasted_iota(jnp.int32, (GROUP, LANES), 0)
    rin = (rowi & (CHUNK - 1)).astype(F32)
    in_chunk = [(rowi >> 6) == c for c in range(GC)]
    lane = lax.broadcasted_iota(jnp.int32, (GROUP, LANES), 1)
    gla_half = [((lane >> 6) & 1) == i for i in range(2)]
    ret_half = [((lane >> 5) & 1) == i for i in range(2)]
    ret_d = [jnp.where(same, jnp.exp(LOG_GAMMA[h] * dist), 0.0) for h in range(HEADS)]
    ret_eb = [jnp.exp(LOG_GAMMA[h] * (rin + 1.0)) for h in range(HEADS)]
    ret_ek = [jnp.exp(LOG_GAMMA[h] * (CHUNK - 1.0 - rin)) for h in range(HEADS)]
    ret_dec = [math.exp(LOG_GAMMA[h] * CHUNK) for h in range(HEADS)]

    def block_diag(x):
        return _mx(jnp.concatenate([jnp.where(in_chunk[c], x, 0.0) for c in range(GC)], axis=1))

    lower2 = jnp.concatenate([lower, lower], axis=0)
    upper2 = jnp.concatenate([upper, upper], axis=0)
    ret_d2 = [jnp.concatenate([ret_d[2 * p], ret_d[2 * p + 1]], axis=0) for p in range(2)]
    zeros_v = jnp.zeros((GROUP, DV), MXU_DTYPE)

    def pair_products(hs0, q_m, sc, v2, v2_kv, k_bd, decay):
        sc_b = _mx(sc)
        v_bd = jnp.concatenate([jnp.concatenate([v2[:, :DV], zeros_v], axis=1),
                                jnp.concatenate([zeros_v, v2[:, DV:]], axis=1)], axis=0)
        o_intra = _dot(jnp.concatenate([sc_b[:GROUP], sc_b[GROUP:]], axis=1), v_bd)
        kv_t = _dot_tn(v2_kv, k_bd)
        st = [st_ref[hs0], st_ref[hs0 + 1]]
        parts = []
        for c in range(GC):
            cr = slice(c * CHUNK, (c + 1) * CHUNK)
            qc = jnp.concatenate([q_m[0][cr], q_m[1][cr]], axis=0)
            r = _dot_nt(qc, _mx(jnp.concatenate(st, axis=0)))
            parts.append(jnp.concatenate([r[:CHUNK, :DV], r[CHUNK:, DV:]], axis=1))
            for i in range(2):
                st[i] = st[i] * decay(c, i) + kv_t[i * DV:(i + 1) * DV, c * LANES:(c + 1) * LANES]
        st_ref[hs0] = st[0]
        st_ref[hs0 + 1] = st[1]
        return o_intra, jnp.concatenate(parts, axis=0)

    def group_body(g, carry):
        rows = pl.ds(pl.multiple_of(g * GROUP, GROUP), GROUP)

        z = _dot(_mx(p_ref[rows, C_GLR:C_GLR + LANES]), gw_ref[...]) + gb_ref[...]
        la = (jnp.minimum(z, 0.0) - jnp.log1p(jnp.exp(-jnp.abs(z)))) * (1.0 / GATE_NORM)
        la_hi = _mx(la)
        r1 = la - la_hi.astype(F32)
        la_mid = _mx(r1)
        la_lo = _mx(r1 - la_mid.astype(F32))
        b3 = _dot(tri, jnp.concatenate([la_hi, la_mid, la_lo], axis=1))
        b = b3[:, :256] + b3[:, 256:512] + b3[:, 512:]
        b_ends = [b[(c + 1) * CHUNK - 1:(c + 1) * CHUNK, :] for c in range(GC)]
        b_last = jnp.concatenate([jnp.broadcast_to(e, (CHUNK, 256)) for e in b_ends], axis=0)
        decs = [jnp.exp(e) for e in b_ends]
        eb = jnp.exp(b)
        enb = jnp.exp(-b)
        ekv = jnp.exp(b_last - b)
        q = p_ref[rows, C_GQ:C_GQ + 256] * (DK ** -0.5)
        k = p_ref[rows, C_GK:C_GK + 256]
        qe, qn = q * eb, q * enb
        ke, kn, kk = k * eb, k * enb, k * ekv
        for p in range(2):
            ls = slice(p * LANES, (p + 1) * LANES)
            qe_m = [_mx(jnp.where(gla_half[i], qe[:, ls], 0.0)) for i in range(2)]
            qn_m = [_mx(jnp.where(gla_half[i], qn[:, ls], 0.0)) for i in range(2)]
            s_lo = _dot_nt(jnp.concatenate(qe_m, axis=0), _mx(kn[:, ls]))
            s_up = _dot_nt(jnp.concatenate(qn_m, axis=0), _mx(ke[:, ls]))
            sc = jnp.where(lower2, s_lo, jnp.where(upper2, s_up, 0.0))
            v2 = _mx(p_ref[rows, C_GV + 2 * p * DV:C_GV + (2 * p + 2) * DV])
            o_intra, o_inter = pair_products(2 * p, qe_m, sc, v2, v2, block_diag(kk[:, ls]),
                                             lambda c, i: decs[c][:, ls])
            o2 = o_intra + o_inter
            for i in range(2):
                h = 2 * p + i
                o = o2[:, i * DV:(i + 1) * DV]
                o = o * lax.rsqrt(jnp.mean(o * o, -1, keepdims=True) + LN_EPS) * gn_ref[...]
                o = o * _silu(p_ref[rows, C_GG + h * DV:C_GG + (h + 1) * DV])
                o_ref[rows, h * DV:(h + 1) * DV] = o.astype(o_ref.dtype)

        base = pos_ref[pl.ds(pl.multiple_of(g * GROUP, GROUP), 1), :].astype(F32) * invf_ref[...]
        c0, s0 = jnp.cos(base), jnp.sin(base)
        cs = c0 * cd_ref[...] - s0 * sd_ref[...]
        sn = (s0 * sgn_ref[...]) * cd_ref[...] + (c0 * sgn_ref[...]) * sd_ref[...]
        for p in range(2):
            tq = p_ref[rows, C_RQ + p * LANES:C_RQ + (p + 1) * LANES]
            tk = p_ref[rows, C_RK + p * LANES:C_RK + (p + 1) * LANES] * (DK ** -0.5)
            rq = tq * cs + pltpu.roll(tq, LANES // 2, 1) * sn
            rk = tk * cs + pltpu.roll(tk, LANES // 2, 1) * sn
            q_m = [_mx(jnp.where(ret_half[i], rq, 0.0)) for i in range(2)]
            s = _dot_nt(jnp.concatenate(q_m, axis=0), _mx(rk)) * ret_d2[p]
            v2f = p_ref[rows, C_RV + 2 * p * DV:C_RV + (2 * p + 2) * DV]
            v2_kv = jnp.concatenate([v2f[:, :DV] * ret_ek[2 * p], v2f[:, DV:] * ret_ek[2 * p + 1]], axis=1)
            o_intra, o_inter = pair_products(HEADS + 2 * p, q_m, s, _mx(v2f), _mx(v2_kv), block_diag(rk),
                                             lambda c, i: ret_dec[2 * p + i])
            for i in range(2):
                h = 2 * p + i
                o = o_intra[:, i * DV:(i + 1) * DV] + ret_eb[h] * o_inter[:, i * DV:(i + 1) * DV]
                mu = jnp.mean(o, -1, keepdims=True)
                oc = o - mu
                var = jnp.mean(oc * oc, -1, keepdims=True)
                o = oc * lax.rsqrt(var + LN_EPS) * rng_ref[:, h * DV:(h + 1) * DV] \
                    + rnb_ref[:, h * DV:(h + 1) * DV]
                o = o * _silu(p_ref[rows, C_RG + h * DV:C_RG + (h + 1) * DV])
                o_ref[rows, (HEADS + h) * DV:(HEADS + h + 1) * DV] = o.astype(o_ref.dtype)
        return carry

    lax.fori_loop(0, n_groups, group_body, 0)


def _mixer_call(proj, pos, gw, gb, gn, rng, rnb, invf, sgn, cd, sd, bsz, seq, ts):
    t = bsz * seq
    nst = seq // ts
    const = lambda b, s: (0, 0)
    return pl.pallas_call(
        functools.partial(_mixer_kernel, n_groups=ts // GROUP),
        grid=(bsz, nst),
        in_specs=[
            pl.BlockSpec((ts, PROJ_W), lambda b, s: (b * nst + s, 0)),
            pl.BlockSpec((ts, 1), lambda b, s: (b * nst + s, 0)),
            pl.BlockSpec((LANES, 256), const),
            pl.BlockSpec((1, 256), const),
            pl.BlockSpec((1, DV), const),
            pl.BlockSpec((1, HEADS * DV), const),
            pl.BlockSpec((1, HEADS * DV), const),
            pl.BlockSpec((1, LANES), const),
            pl.BlockSpec((1, LANES), const),
            pl.BlockSpec((GROUP, LANES), const),
            pl.BlockSpec((GROUP, LANES), const),
        ],
        out_specs=pl.BlockSpec((ts, D_MODEL), lambda b, s: (b * nst + s, 0)),
        out_shape=jax.ShapeDtypeStruct((t, D_MODEL), MXU_DTYPE),
        scratch_shapes=[pltpu.VMEM((2 * HEADS, DV, LANES), F32)],
        compiler_params=pltpu.CompilerParams(
            dimension_semantics=("arbitrary", "arbitrary"), vmem_limit_bytes=VMEM_LIMIT),
        name="mixers",
    )(proj, pos, gw, gb, gn, rng, rnb, invf, sgn, cd, sd)


def _post_mix_kernel(o_ref, h0_ref, wo_ref, g_ref, b_ref, rwt_ref, rbt_ref,
                     h1_ref, h1t_ref, ri_ref, rg_ref, cnt_ref, triu_ref, carry_ref, *, tm):
    @pl.when(pl.program_id(0) == 0)
    def _():
        r = lax.broadcasted_iota(jnp.int32, (tm, tm), 0)
        c = lax.broadcasted_iota(jnp.int32, (tm, tm), 1)
        triu_ref[...] = (r < c).astype(MXU_DTYPE)
        carry_ref[...] = jnp.zeros_like(carry_ref)

    mix = _dot(o_ref[...], wo_ref[...])
    h1 = _layer_norm(DEEPNORM_ALPHA * h0_ref[...] + mix, g_ref[...], b_ref[...])
    h1_ref[...] = h1
    _store_token_tiles(h1t_ref, _pack_pairs(h1))

    l = (_dot_nt(rwt_ref[...], _mx(h1)) + rbt_ref[...])[:N_EXPERTS]
    erow = lax.broadcasted_iota(jnp.int32, (N_EXPERTS, tm), 0).astype(F32)
    vals, idxs = [], []
    for _ in range(TOP_K):
        m = jnp.max(l, 0, keepdims=True)
        i = jnp.min(jnp.where(l == m, erow, float(LANES)), 0, keepdims=True)
        vals.append(m)
        idxs.append(i)
        l = jnp.where(erow == i, -jnp.inf, l)
    exps = [jnp.exp(v - vals[0]) for v in vals]
    inv = 1.0 / (exps[0] + exps[1] + exps[2] + exps[3])

    onehot = jnp.zeros((N_EXPERTS, tm), F32)
    for i in idxs:
        onehot = onehot + (erow == i).astype(F32)
    before = _dot(_mx(onehot), triu_ref[...]) + carry_ref[:, :1]
    ranks = [jnp.sum(jnp.where(erow == i, before, 0.0), 0, keepdims=True) for i in idxs]
    ri_ref[...] = jnp.concatenate(idxs + ranks, axis=0).astype(jnp.int32)
    gates_t = jnp.concatenate([e * inv for e in exps] + [jnp.zeros((LANES - TOP_K, tm), F32)], axis=0)
    rg_ref[...] = jnp.transpose(gates_t)
    carry_ref[...] = carry_ref[...] + jnp.sum(onehot, 1, keepdims=True)
    cnt_ref[...] = carry_ref[...]


def _post_mix_call(o, h0, wo, g, b, rwt, rbt, tm):
    t = o.shape[0]
    const = lambda i: (0, 0)
    return pl.pallas_call(
        functools.partial(_post_mix_kernel, tm=tm),
        grid=(t // tm,),
        in_specs=[
            pl.BlockSpec((tm, D_MODEL), lambda i: (i, 0)),
            pl.BlockSpec((tm, D_MODEL), lambda i: (i, 0)),
            pl.BlockSpec((D_MODEL, D_MODEL), const),
            pl.BlockSpec((1, D_MODEL), const),
            pl.BlockSpec((1, D_MODEL), const),
            pl.BlockSpec((LANES, D_MODEL), const),
            pl.BlockSpec((LANES, 1), const),
        ],
        out_specs=[
            pl.BlockSpec((tm, D_MODEL), lambda i: (i, 0)),
            pl.BlockSpec((tm * PK_ROWS, LANES), lambda i: (i, 0)),
            pl.BlockSpec((2 * TOP_K, tm), lambda i: (0, i)),
            pl.BlockSpec((tm, LANES), lambda i: (i, 0)),
            pl.BlockSpec((N_EXPERTS, LANES), const),
        ],
        out_shape=[
            jax.ShapeDtypeStruct((t, D_MODEL), F32),
            jax.ShapeDtypeStruct((t * PK_ROWS, LANES), jnp.uint32),
            jax.ShapeDtypeStruct((2 * TOP_K, t), jnp.int32),
            jax.ShapeDtypeStruct((t, LANES), F32),
            jax.ShapeDtypeStruct((N_EXPERTS, LANES), F32),
        ],
        scratch_shapes=[pltpu.VMEM((tm, tm), MXU_DTYPE), pltpu.VMEM((N_EXPERTS, LANES), F32)],
        compiler_params=pltpu.CompilerParams(
            dimension_semantics=("arbitrary",), vmem_limit_bytes=VMEM_LIMIT),
        name="post_mix_router",
    )(o, h0, wo, g, b, rwt, rbt)


def _step_tables(dest, tm):
    k, t = dest.shape
    return dest.reshape(k, t // tm, tm).transpose(1, 0, 2).reshape(t // tm, 1, k * tm)


def _dispatch_kernel(pad_row_ref, pad_n_ref, dest_ref, h1p_ref, w1_ref, w2_ref,
                     xs_hbm, w1c_ref, w2c_ref, stage_ref, zero_ref, sem, *, tm):
    i = pl.program_id(0)

    def tile(ref, t):
        return ref.at[pl.ds(pl.multiple_of(t * PK_ROWS, PK_ROWS), PK_ROWS), :]

    slot = i % 2
    rows = tm * PK_ROWS
    stage = stage_ref.at[pl.ds(pl.multiple_of(slot * rows, rows), rows), :]
    stage[...] = h1p_ref[...]

    def body(it, c):
        for u in range(DMA_UNROLL):
            r = it * DMA_UNROLL + u
            for k in range(TOP_K):
                pltpu.make_async_copy(tile(stage, r), tile(xs_hbm, dest_ref[0, 0, k * tm + r]),
                                      sem.at[slot]).start(priority=k % 2)
        return c
    lax.fori_loop(0, tm // DMA_UNROLL, body, 0)
    w1c_ref[...] = _mx(w1_ref[...])
    w2c_ref[...] = _mx(w2_ref[...])

    def drain(s):
        for _ in range(TOP_K):
            pltpu.make_async_copy(h1p_ref, xs_hbm.at[pl.ds(0, rows), :], sem.at[s]).wait()

    @pl.when(i > 0)
    def _():
        drain(1 - slot)

    @pl.when(i == pl.num_programs(0) - 1)
    def _():
        drain(slot)
        zero_ref[...] = jnp.zeros_like(zero_ref)
        zero = zero_ref.at[pl.ds(0, PK_ROWS), :]

        def per_expert(e, c):
            def start(r, c2):
                pltpu.make_async_copy(zero, tile(xs_hbm, pad_row_ref[e] + r), sem.at[2]).start()
                return c2
            lax.fori_loop(0, pad_n_ref[e], start, 0)

            def wait(r, c2):
                pltpu.make_async_copy(zero, tile(xs_hbm, 0), sem.at[2]).wait()
                return c2
            lax.fori_loop(0, pad_n_ref[e], wait, 0)
            return c
        lax.fori_loop(0, pad_row_ref.shape[0], per_expert, 0)


def _dispatch_call(pad_row, pad_n, dest, h1p, w1, w2, n_slots, n_steps):
    t = h1p.shape[0] // PK_ROWS
    tm = t // n_steps
    r1, r2 = w1.shape[0] // n_steps, w2.shape[0] // n_steps
    step = lambda i, pr, pn: (i, 0)
    grid_spec = pltpu.PrefetchScalarGridSpec(
        num_scalar_prefetch=2,
        grid=(n_steps,),
        in_specs=[
            pl.BlockSpec((1, 1, TOP_K * tm), lambda i, pr, pn: (i, 0, 0), memory_space=pltpu.SMEM),
            pl.BlockSpec((tm * PK_ROWS, LANES), step),
            pl.BlockSpec((r1, w1.shape[1]), step),
            pl.BlockSpec((r2, w2.shape[1]), step),
        ],
        out_specs=[
            pl.BlockSpec(memory_space=pl.ANY),
            pl.BlockSpec((r1, w1.shape[1]), step),
            pl.BlockSpec((r2, w2.shape[1]), step),
        ],
        scratch_shapes=[pltpu.VMEM((2 * tm * PK_ROWS, LANES), jnp.uint32),
                        pltpu.VMEM((F32_ROWS, LANES), jnp.uint32), pltpu.SemaphoreType.DMA((3,))],
    )
    return pl.pallas_call(
        functools.partial(_dispatch_kernel, tm=tm),
        grid_spec=grid_spec,
        out_shape=[
            jax.ShapeDtypeStruct((n_slots * PK_ROWS, LANES), jnp.uint32),
            jax.ShapeDtypeStruct(w1.shape, MXU_DTYPE),
            jax.ShapeDtypeStruct(w2.shape, MXU_DTYPE),
        ],
        compiler_params=pltpu.CompilerParams(
            dimension_semantics=("arbitrary",), vmem_limit_bytes=VMEM_LIMIT),
        name="dispatch",
    )(pad_row, pad_n, _step_tables(dest, tm), h1p, w1, w2)


def _expert_kernel(be_ref, nr_ref, x_ref, *refs):
    y_ref = refs[-1]
    rows = MOE_BLOCK * PK_ROWS
    for half in range(BLOCKS_PER_STEP):
        w1_ref, b1_ref, w2_ref, b2_ref = refs[4 * half:4 * half + 4]
        g = pl.program_id(0) * BLOCKS_PER_STEP + half
        x_blk = x_ref.at[pl.ds(half * rows, rows), :]
        y_blk = y_ref.at[pl.ds(half * rows, rows), :]

        @pl.when(g >= nr_ref[0])
        def _():
            y_blk[...] = jnp.zeros_like(y_blk)

        @pl.when(g < nr_ref[0])
        def _():
            x = _mx(_unpack_pairs(_load_token_tiles(x_blk, MOE_BLOCK, PK_ROWS)))
            hh = _dot(x, w1_ref[0]) + b1_ref[0]
            x_glu = jnp.minimum(hh[:, :D_FF], SWIGLU_LIMIT)
            x_lin = jnp.clip(hh[:, D_FF:], -SWIGLU_LIMIT, SWIGLU_LIMIT)
            act = x_glu * (1.0 / (1.0 + jnp.exp(-SWIGLU_ALPHA * x_glu))) * (x_lin + 1.0)
            _store_token_tiles(y_blk, _pack_pairs(_dot(_mx(act), w2_ref[0]) + b2_ref[0]))


def _expert_call(block_expert, n_real, xs, w1, b1, w2, b2):
    n_blocks = block_expert.shape[0]
    assert n_blocks % BLOCKS_PER_STEP == 0
    step = lambda i, be, nr: (i, 0)
    rows = BLOCKS_PER_STEP * MOE_BLOCK * PK_ROWS
    expert_specs, expert_args = [], []
    for half in range(BLOCKS_PER_STEP):
        ex = functools.partial(lambda i, be, nr, half: (be[i * BLOCKS_PER_STEP + half], 0, 0), half=half)
        expert_specs += [pl.BlockSpec((1, D_MODEL, 2 * D_FF), ex), pl.BlockSpec((1, 1, 2 * D_FF), ex),
                         pl.BlockSpec((1, D_FF, D_MODEL), ex), pl.BlockSpec((1, 1, D_MODEL), ex)]
        expert_args += [w1, b1, w2, b2]
    grid_spec = pltpu.PrefetchScalarGridSpec(
        num_scalar_prefetch=2,
        grid=(n_blocks // BLOCKS_PER_STEP,),
        in_specs=[pl.BlockSpec((rows, LANES), step)] + expert_specs,
        out_specs=pl.BlockSpec((rows, LANES), step),
    )
    return pl.pallas_call(
        _expert_kernel,
        grid_spec=grid_spec,
        out_shape=jax.ShapeDtypeStruct((n_blocks * MOE_BLOCK * PK_ROWS, LANES), jnp.uint32),
        compiler_params=pltpu.CompilerParams(
            dimension_semantics=("arbitrary",), vmem_limit_bytes=VMEM_LIMIT),
        name="moe_experts",
    )(block_expert, n_real, xs, *expert_args)


def _gather_combine_kernel(dest0_ref, destn_ref, gate_ref, h1_ref, g_ref, b_ref, ys_hbm,
                           out_ref, ybuf, sem, *, tm):
    i = pl.program_id(0)
    slot = i % 2
    rows = tm * PK_ROWS

    def tile(ref, t):
        return ref.at[pl.ds(pl.multiple_of(t * PK_ROWS, PK_ROWS), PK_ROWS), :]

    def gather(tbl, s):
        def body(it, c):
            for u in range(DMA_UNROLL):
                r = it * DMA_UNROLL + u
                for k in range(TOP_K):
                    pltpu.make_async_copy(tile(ys_hbm, tbl[0, 0, k * tm + r]),
                                          tile(ybuf, (s * TOP_K + k) * tm + r), sem.at[s]).start(priority=k % 2)
            return c
        lax.fori_loop(0, tm // DMA_UNROLL, body, 0)

    @pl.when(i == 0)
    def _():
        gather(dest0_ref, 0)

    @pl.when(i + 1 < pl.num_programs(0))
    def _():
        gather(destn_ref, 1 - slot)

    for k in range(TOP_K):
        plane = ybuf.at[pl.ds(pl.multiple_of((slot * TOP_K + k) * rows, rows), rows), :]
        pltpu.make_async_copy(ys_hbm.at[pl.ds(0, rows), :], plane, sem.at[slot]).wait()

    gates = gate_ref[...]
    acc = DEEPNORM_ALPHA * h1_ref[...]
    for k in range(TOP_K):
        plane = ybuf.at[pl.ds(pl.multiple_of((slot * TOP_K + k) * rows, rows), rows), :]
        acc = acc + gates[:, k:k + 1] * _unpack_pairs(_load_token_tiles(plane, tm, PK_ROWS))
    out_ref[...] = _layer_norm(acc, g_ref[...], b_ref[...])


def _gather_combine_call(dest, gates, h1, g, b, ys, tm):
    t = h1.shape[0]
    nt = t // tm
    const = lambda i: (0, 0)
    smem = functools.partial(pl.BlockSpec, memory_space=pltpu.SMEM)
    dest = _step_tables(dest, tm)
    dest_next = jnp.concatenate([dest[1:], dest[:1]], axis=0)
    return pl.pallas_call(
        functools.partial(_gather_combine_kernel, tm=tm),
        grid=(nt,),
        in_specs=[
            smem((1, 1, TOP_K * tm), lambda i: (0, 0, 0)),
            smem((1, 1, TOP_K * tm), lambda i: (i, 0, 0)),
            pl.BlockSpec((tm, LANES), lambda i: (i, 0)),
            pl.BlockSpec((tm, D_MODEL), lambda i: (i, 0)),
            pl.BlockSpec((1, D_MODEL), const),
            pl.BlockSpec((1, D_MODEL), const),
            pl.BlockSpec(memory_space=pl.ANY),
        ],
        out_specs=pl.BlockSpec((tm, D_MODEL), lambda i: (i, 0)),
        out_shape=jax.ShapeDtypeStruct((t, D_MODEL), F32),
        scratch_shapes=[pltpu.VMEM((2 * TOP_K * tm * PK_ROWS, LANES), jnp.uint32),
                        pltpu.SemaphoreType.DMA((2,))],
        compiler_params=pltpu.CompilerParams(
            dimension_semantics=("arbitrary",), vmem_limit_bytes=VMEM_LIMIT),
        name="gather_combine_ln2",
    )(dest, dest_next, gates, h1, g, b, ys)


def _relayout_w_in(w):
    sizes = (256, 256, 512, 512, GATE_RANK, 256, 256, 512, 512)
    offs = np.concatenate([[0], np.cumsum(sizes)])
    gq, gk, gv, gg, glr, rq, rk, rv, rg = [w[:, offs[i]:offs[i + 1]] for i in range(9)]
    perm = np.zeros((256,), np.int32)
    for p in range(2):
        for l in range(LANES):
            part, hh, f = l // 64, (l % 64) // 32, l % 32
            perm[p * LANES + l] = (2 * p + hh) * DK + part * (DK // 2) + f
    glr = jnp.pad(glr, ((0, 0), (0, LANES - GATE_RANK)))
    return jnp.concatenate([gq, gk, gv, gg, glr, rq[:, perm], rk[:, perm], rv, rg], axis=1)


def kernel(x, positions, ln_in_g, ln_in_b, w_in, gla_gate_w, gla_gate_b, gla_norm_g, ret_norm_g,
           ret_norm_b, w_out, ln1_g, ln1_b, router_w, router_b, moe_w1, moe_b1, moe_w2, moe_b2,
           ln2_g, ln2_b):
    bsz, seq, d = x.shape
    assert d == D_MODEL and seq % GROUP == 0 and w_in.shape[0] == DEPTH == 1
    t = bsz * seq
    ts = 1024 if seq % 1024 == 0 else GROUP
    tm = min(512, t)
    assert seq % ts == 0 and t % tm == 0
    row = lambda v: v.reshape(1, -1).astype(F32)

    w_p = _relayout_w_in(w_in[0]).astype(MXU_DTYPE)
    h0, proj = _ln_proj_call(x.reshape(t, d), row(ln_in_g), row(ln_in_b), w_p, tm)

    gw = jnp.pad(gla_gate_w[0], ((0, LANES - GATE_RANK), (0, 0))).astype(MXU_DTYPE)
    half = DK // 2
    inv_freq = 1.0 / (ROPE_BASE ** np.linspace(0.0, 1.0, half, dtype=np.float32))
    invf = jnp.asarray(np.tile(inv_freq, LANES // half).reshape(1, LANES), F32)
    sgn = jnp.asarray(np.where(np.arange(LANES) < LANES // 2, -1.0, 1.0).reshape(1, LANES), F32)
    step_ang = np.arange(GROUP, dtype=np.float64)[:, None] * np.tile(inv_freq, LANES // half)[None, :]
    cd = jnp.asarray(np.cos(step_ang), F32)
    sd = jnp.asarray(np.sin(step_ang), F32)
    o = _mixer_call(proj, positions.reshape(t, 1), gw, row(gla_gate_b[0]), row(gla_norm_g[0]),
                    row(ret_norm_g[0]), row(ret_norm_b[0]), invf, sgn, cd, sd, bsz, seq, ts)

    rwt = jnp.pad(router_w[0].T, ((0, LANES - N_EXPERTS), (0, 0))).astype(MXU_DTYPE)
    rbt = jnp.pad(router_b[0], (0, LANES - N_EXPERTS)).reshape(LANES, 1).astype(F32)
    h1, h1t, r_i, r_g, cnt = _post_mix_call(o, h0, w_out[0].astype(MXU_DTYPE), row(ln1_g[0]),
                                       row(ln1_b[0]), rwt, rbt, tm)

    e = r_i[:TOP_K].reshape(-1)
    rank = r_i[TOP_K:].reshape(-1)
    counts = cnt[:, 0].astype(jnp.int32)
    padded = ((counts + MOE_BLOCK - 1) // MOE_BLOCK) * MOE_BLOCK
    pad_end = jnp.cumsum(padded)
    pad_start = pad_end - padded
    tk = t * TOP_K
    n_blocks = (tk + MOE_BLOCK - 1) // MOE_BLOCK + N_EXPERTS
    i32 = jnp.int32
    dest = rank
    for j in range(N_EXPERTS):
        dest = dest + jnp.where(e == j, pad_start[j], 0)
    dest = dest.reshape(TOP_K, t)
    block_start = jnp.arange(n_blocks, dtype=i32) * MOE_BLOCK
    block_expert = jnp.minimum(jnp.sum(block_start[:, None] >= pad_end[None, :], -1),
                               N_EXPERTS - 1).astype(i32)
    n_real = (pad_end[-1:] // MOE_BLOCK).astype(i32)

    n_slots = n_blocks * MOE_BLOCK
    hole_row = jnp.concatenate([pad_start + counts, pad_end[-1:]]).astype(i32)
    hole_n = jnp.concatenate([padded - counts, n_slots - pad_end[-1:]]).astype(i32)
    xs, w1c, w2c = _dispatch_call(hole_row, hole_n, dest, h1t,
                                  moe_w1[0].reshape(N_EXPERTS * D_MODEL, 2 * D_FF),
                                  moe_w2[0].reshape(N_EXPERTS * D_FF, D_MODEL), n_slots, DISPATCH_STEPS)
    ys = _expert_call(block_expert, n_real, xs,
                      w1c.reshape(N_EXPERTS, D_MODEL, 2 * D_FF), moe_b1[0].reshape(N_EXPERTS, 1, 2 * D_FF),
                      w2c.reshape(N_EXPERTS, D_FF, D_MODEL), moe_b2[0].reshape(N_EXPERTS, 1, D_MODEL))

    out = _gather_combine_call(dest, r_g, h1, row(ln2_g[0]), row(ln2_b[0]), ys, min(256, t))
    return out.reshape(bsz, seq, d)
```

```python
import functools
import math

import numpy as np
import jax
import jax.numpy as jnp
from jax import lax
from jax.experimental import pallas as pl
from jax.experimental.pallas import tpu as pltpu

F32 = jnp.float32
MXU_DTYPE = jnp.bfloat16

D_MODEL = 1024
CHUNK = 64
GC = 4
GROUP = GC * CHUNK
HEADS = 4
DK = 64
DV = 128
GATE_RANK = 16
GATE_NORM = 16.0
ROPE_BASE = 10000.0
N_EXPERTS = 32
TOP_K = 4
D_FF = 1024
SWIGLU_ALPHA = 1.702
SWIGLU_LIMIT = 7.0
MOE_BLOCK = 256
LN_PROJ_SLABS = 4
BLOCKS_PER_STEP = 2
DMA_UNROLL = 8
DISPATCH_STEPS = 64
LN_EPS = 1e-5
DEPTH = 1
DEEPNORM_ALPHA = (2.0 * DEPTH) ** 0.25

LANES = 128
VMEM_LIMIT = 56 * 1024 * 1024

C_GQ, C_GK, C_GV, C_GG, C_GLR = 0, 256, 512, 1024, 1536
C_RQ, C_RK, C_RV, C_RG = 1664, 1920, 2176, 2688
PROJ_W = 3200

LOG_GAMMA = [math.log1p(-(2.0 ** (-5.0 - h))) for h in range(HEADS)]


def _dot(a, b):
    return jnp.dot(a, b, preferred_element_type=F32)


def _dot_nt(a, b):
    return lax.dot_general(a, b, (((1,), (1,)), ((), ())), preferred_element_type=F32)


def _dot_tn(a, b):
    return lax.dot_general(a, b, (((0,), (0,)), ((), ())), preferred_element_type=F32)


def _mx(a):
    return a.astype(MXU_DTYPE)


def _layer_norm(x, g, b):
    mu = jnp.mean(x, -1, keepdims=True)
    xc = x - mu
    var = jnp.mean(xc * xc, -1, keepdims=True)
    return xc * lax.rsqrt(var + LN_EPS) * g + b


def _silu(x):
    return x * (1.0 / (1.0 + jnp.exp(-x)))


F32_ROWS = D_MODEL // LANES
PK_ROWS = D_MODEL // (2 * LANES)


def _load_token_tiles(ref, n, r):
    return jnp.concatenate([ref[pl.ds(j, n, stride=r), :] for j in range(r)], axis=1)


def _store_token_tiles(ref, rows):
    n, r = rows.shape[0], rows.shape[1] // LANES
    for j in range(r):
        ref[pl.ds(j, n, stride=r), :] = rows[:, j * LANES:(j + 1) * LANES]


def _pack_pairs(x):
    h = x.shape[1] // 2
    r = x.astype(jnp.bfloat16).astype(F32)
    lo = lax.bitcast_convert_type(r[:, :h], jnp.uint32) >> 16
    hi = lax.bitcast_convert_type(r[:, h:], jnp.uint32) & jnp.uint32(0xFFFF0000)
    return lo | hi


def _unpack_pairs(u):
    lo = lax.bitcast_convert_type(u << 16, F32)
    hi = lax.bitcast_convert_type(u & jnp.uint32(0xFFFF0000), F32)
    return jnp.concatenate([lo, hi], axis=1)


def _ln_proj_kernel(x_ref, g_ref, b_ref, w_ref, h_ref, p_ref):
    n = x_ref.shape[0] // LN_PROJ_SLABS
    for j in range(LN_PROJ_SLABS):
        rs = slice(j * n, (j + 1) * n)
        h = _layer_norm(x_ref[rs, :], g_ref[...], b_ref[...])
        h_ref[rs, :] = h
        p_ref[rs, :] = _dot(_mx(h), w_ref[...])


def _ln_proj_call(x2, g, b, w, tm):
    t = x2.shape[0]
    return pl.pallas_call(
        _ln_proj_kernel,
        grid=(t // tm,),
        in_specs=[
            pl.BlockSpec((tm, D_MODEL), lambda i: (i, 0)),
            pl.BlockSpec((1, D_MODEL), lambda i: (0, 0)),
            pl.BlockSpec((1, D_MODEL), lambda i: (0, 0)),
            pl.BlockSpec((D_MODEL, PROJ_W), lambda i: (0, 0)),
        ],
        out_specs=[
            pl.BlockSpec((tm, D_MODEL), lambda i: (i, 0)),
            pl.BlockSpec((tm, PROJ_W), lambda i: (i, 0)),
        ],
        out_shape=[
            jax.ShapeDtypeStruct((t, D_MODEL), F32),
            jax.ShapeDtypeStruct((t, PROJ_W), F32),
        ],
        compiler_params=pltpu.CompilerParams(
            dimension_semantics=("arbitrary",), vmem_limit_bytes=VMEM_LIMIT),
        name="ln_proj",
    )(x2, g, b, w)


def _mixer_kernel(p_ref, pos_ref, gw_ref, gb_ref, gn_ref, rng_ref, rnb_ref, invf_ref, sgn_ref,
                  cd_ref, sd_ref, o_ref, st_ref, *, n_groups):
    @pl.when(pl.program_id(1) == 0)
    def _():
        st_ref[...] = jnp.zeros_like(st_ref)

    rr = lax.broadcasted_iota(jnp.int32, (GROUP, GROUP), 0)
    cc = lax.broadcasted_iota(jnp.int32, (GROUP, GROUP), 1)
    same = (rr >> 6) == (cc >> 6)
    lower = same & (rr >= cc)
    upper = same & (rr < cc)
    tri = lower.astype(MXU_DTYPE)
    dist = jnp.abs(rr - cc).astype(F32)
    rowi = lax.broadcasted_iota(jnp.int32, (GROUP, LANES), 0)
    rin = (rowi & (CHUNK - 1)).astype(F32)
    in_chunk = [(rowi >> 6) == c for c in range(GC)]
    lane = lax.broadcasted_iota(jnp.int32, (GROUP, LANES), 1)
    gla_half = [((lane >> 6) & 1) == i for i in range(2)]
    ret_half = [((lane >> 5) & 1) == i for i in range(2)]
    ret_d = [jnp.where(same, jnp.exp(LOG_GAMMA[h] * dist), 0.0) for h in range(HEADS)]
    ret_eb = [jnp.exp(LOG_GAMMA[h] * (rin + 1.0)) for h in range(HEADS)]
    ret_ek = [jnp.exp(LOG_GAMMA[h] * (CHUNK - 1.0 - rin)) for h in range(HEADS)]
    ret_dec = [math.exp(LOG_GAMMA[h] * CHUNK) for h in range(HEADS)]

    def block_diag(x):
        return _mx(jnp.concatenate([jnp.where(in_chunk[c], x, 0.0) for c in range(GC)], axis=1))

    lower2 = jnp.concatenate([lower, lower], axis=0)
    upper2 = jnp.concatenate([upper, upper], axis=0)
    ret_d2 = [jnp.concatenate([ret_d[2 * p], ret_d[2 * p + 1]], axis=0) for p in range(2)]
    zeros_v = jnp.zeros((GROUP, DV), MXU_DTYPE)

    def pair_products(hs0, q_m, sc, v2, v2_kv, k_bd, decay):
        sc_b = _mx(sc)
        v_bd = jnp.concatenate([jnp.concatenate([v2[:, :DV], zeros_v], axis=1),
                                jnp.concatenate([zeros_v, v2[:, DV:]], axis=1)], axis=0)
        o_intra = _dot(jnp.concatenate([sc_b[:GROUP], sc_b[GROUP:]], axis=1), v_bd)
        kv_t = _dot_tn(v2_kv, k_bd)
        st = [st_ref[hs0], st_ref[hs0 + 1]]
        parts = []
        for c in range(GC):
            cr = slice(c * CHUNK, (c + 1) * CHUNK)
            qc = jnp.concatenate([q_m[0][cr], q_m[1][cr]], axis=0)
            r = _dot_nt(qc, _mx(jnp.concatenate(st, axis=0)))
            parts.append(jnp.concatenate([r[:CHUNK, :DV], r[CHUNK:, DV:]], axis=1))
            for i in range(2):
                st[i] = st[i] * decay(c, i) + kv_t[i * DV:(i + 1) * DV, c * LANES:(c + 1) * LANES]
        st_ref[hs0] = st[0]
        st_ref[hs0 + 1] = st[1]
        return o_intra, jnp.concatenate(parts, axis=0)

    def group_body(g, carry):
        rows = pl.ds(pl.multiple_of(g * GROUP, GROUP), GROUP)

        z = _dot(_mx(p_ref[rows, C_GLR:C_GLR + LANES]), gw_ref[...]) + gb_ref[...]
        la = (jnp.minimum(z, 0.0) - jnp.log1p(jnp.exp(-jnp.abs(z)))) * (1.0 / GATE_NORM)
        la_hi = _mx(la)
        r1 = la - la_hi.astype(F32)
        la_mid = _mx(r1)
        la_lo = _mx(r1 - la_mid.astype(F32))
        b3 = _dot(tri, jnp.concatenate([la_hi, la_mid, la_lo], axis=1))
        b = b3[:, :256] + b3[:, 256:512] + b3[:, 512:]
        b_ends = [b[(c + 1) * CHUNK - 1:(c + 1) * CHUNK, :] for c in range(GC)]
        b_last = jnp.concatenate([jnp.broadcast_to(e, (CHUNK, 256)) for e in b_ends], axis=0)
        decs = [jnp.exp(e) for e in b_ends]
        eb = jnp.exp(b)
        enb = jnp.exp(-b)
        ekv = jnp.exp(b_last - b)
        q = p_ref[rows, C_GQ:C_GQ + 256] * (DK ** -0.5)
        k = p_ref[rows, C_GK:C_GK + 256]
        qe, qn = q * eb, q * enb
        ke, kn, kk = k * eb, k * enb, k * ekv
        for p in range(2):
            ls = slice(p * LANES, (p + 1) * LANES)
            qe_m = [_mx(jnp.where(gla_half[i], qe[:, ls], 0.0)) for i in range(2)]
            qn_m = [_mx(jnp.where(gla_half[i], qn[:, ls], 0.0)) for i in range(2)]
            s_lo = _dot_nt(jnp.concatenate(qe_m, axis=0), _mx(kn[:, ls]))
            s_up = _dot_nt(jnp.concatenate(qn_m, axis=0), _mx(ke[:, ls]))
            sc = jnp.where(lower2, s_lo, jnp.where(upper2, s_up, 0.0))
            v2 = _mx(p_ref[rows, C_GV + 2 * p * DV:C_GV + (2 * p + 2) * DV])
            o_intra, o_inter = pair_products(2 * p, qe_m, sc, v2, v2, block_diag(kk[:, ls]),
                                             lambda c, i: decs[c][:, ls])
            o2 = o_intra + o_inter
            for i in range(2):
                h = 2 * p + i
                o = o2[:, i * DV:(i + 1) * DV]
                o = o * lax.rsqrt(jnp.mean(o * o, -1, keepdims=True) + LN_EPS) * gn_ref[...]
                o = o * _silu(p_ref[rows, C_GG + h * DV:C_GG + (h + 1) * DV])
                o_ref[rows, h * DV:(h + 1) * DV] = o.astype(o_ref.dtype)

        base = pos_ref[pl.ds(pl.multiple_of(g * GROUP, GROUP), 1), :].astype(F32) * invf_ref[...]
        c0, s0 = jnp.cos(base), jnp.sin(base)
        cs = c0 * cd_ref[...] - s0 * sd_ref[...]
        sn = (s0 * sgn_ref[...]) * cd_ref[...] + (c0 * sgn_ref[...]) * sd_ref[...]
        for p in range(2):
            tq = p_ref[rows, C_RQ + p * LANES:C_RQ + (p + 1) * LANES]
            tk = p_ref[rows, C_RK + p * LANES:C_RK + (p + 1) * LANES] * (DK ** -0.5)
            rq = tq * cs + pltpu.roll(tq, LANES // 2, 1) * sn
            rk = tk * cs + pltpu.roll(tk, LANES // 2, 1) * sn
            q_m = [_mx(jnp.where(ret_half[i], rq, 0.0)) for i in range(2)]
            s = _dot_nt(jnp.concatenate(q_m, axis=0), _mx(rk)) * ret_d2[p]
            v2f = p_ref[rows, C_RV + 2 * p * DV:C_RV + (2 * p + 2) * DV]
            v2_kv = jnp.concatenate([v2f[:, :DV] * ret_ek[2 * p], v2f[:, DV:] * ret_ek[2 * p + 1]], axis=1)
            o_intra, o_inter = pair_products(HEADS + 2 * p, q_m, s, _mx(v2f), _mx(v2_kv), block_diag(rk),
                                             lambda c, i: ret_dec[2 * p + i])
            for i in range(2):
                h = 2 * p + i
                o = o_intra[:, i * DV:(i + 1) * DV] + ret_eb[h] * o_inter[:, i * DV:(i + 1) * DV]
                mu = jnp.mean(o, -1, keepdims=True)
                oc = o - mu
                var = jnp.mean(oc * oc, -1, keepdims=True)
                o = oc * lax.rsqrt(var + LN_EPS) * rng_ref[:, h * DV:(h + 1) * DV] \
                    + rnb_ref[:, h * DV:(h + 1) * DV]
                o = o * _silu(p_ref[rows, C_RG + h * DV:C_RG + (h + 1) * DV])
                o_ref[rows, (HEADS + h) * DV:(HEADS + h + 1) * DV] = o.astype(o_ref.dtype)
        return carry

    lax.fori_loop(0, n_groups, group_body, 0)


def _mixer_call(proj, pos, gw, gb, gn, rng, rnb, invf, sgn, cd, sd, bsz, seq, ts):
    t = bsz * seq
    nst = seq // ts
    const = lambda b, s: (0, 0)
    return pl.pallas_call(
        functools.partial(_mixer_kernel, n_groups=ts // GROUP),
        grid=(bsz, nst),
        in_specs=[
            pl.BlockSpec((ts, PROJ_W), lambda b, s: (b * nst + s, 0)),
            pl.BlockSpec((ts, 1), lambda b, s: (b * nst + s, 0)),
            pl.BlockSpec((LANES, 256), const),
            pl.BlockSpec((1, 256), const),
            pl.BlockSpec((1, DV), const),
            pl.BlockSpec((1, HEADS * DV), const),
            pl.BlockSpec((1, HEADS * DV), const),
            pl.BlockSpec((1, LANES), const),
            pl.BlockSpec((1, LANES), const),
            pl.BlockSpec((GROUP, LANES), const),
            pl.BlockSpec((GROUP, LANES), const),
        ],
        out_specs=pl.BlockSpec((ts, D_MODEL), lambda b, s: (b * nst + s, 0)),
        out_shape=jax.ShapeDtypeStruct((t, D_MODEL), MXU_DTYPE),
        scratch_shapes=[pltpu.VMEM((2 * HEADS, DV, LANES), F32)],
        compiler_params=pltpu.CompilerParams(
            dimension_semantics=("arbitrary", "arbitrary"), vmem_limit_bytes=VMEM_LIMIT),
        name="mixers",
    )(proj, pos, gw, gb, gn, rng, rnb, invf, sgn, cd, sd)


def _post_mix_kernel(o_ref, h0_ref, wo_ref, g_ref, b_ref, rwt_ref, rbt_ref,
                     h1_ref, h1t_ref, ri_ref, rg_ref, cnt_ref, triu_ref, carry_ref, *, tm):
    @pl.when(pl.program_id(0) == 0)
    def _():
        r = lax.broadcasted_iota(jnp.int32, (tm, tm), 0)
        c = lax.broadcasted_iota(jnp.int32, (tm, tm), 1)
        triu_ref[...] = (r < c).astype(MXU_DTYPE)
        carry_ref[...] = jnp.zeros_like(carry_ref)

    mix = _dot(o_ref[...], wo_ref[...])
    h1 = _layer_norm(DEEPNORM_ALPHA * h0_ref[...] + mix, g_ref[...], b_ref[...])
    h1_ref[...] = h1
    _store_token_tiles(h1t_ref, _pack_pairs(h1))

    l = (_dot_nt(rwt_ref[...], _mx(h1)) + rbt_ref[...])[:N_EXPERTS]
    erow = lax.broadcasted_iota(jnp.int32, (N_EXPERTS, tm), 0).astype(F32)
    vals, idxs = [], []
    for _ in range(TOP_K):
        m = jnp.max(l, 0, keepdims=True)
        i = jnp.min(jnp.where(l == m, erow, float(LANES)), 0, keepdims=True)
        vals.append(m)
        idxs.append(i)
        l = jnp.where(erow == i, -jnp.inf, l)
    exps = [jnp.exp(v - vals[0]) for v in vals]
    inv = 1.0 / (exps[0] + exps[1] + exps[2] + exps[3])

    onehot = jnp.zeros((N_EXPERTS, tm), F32)
    for i in idxs:
        onehot = onehot + (erow == i).astype(F32)
    before = _dot(_mx(onehot), triu_ref[...]) + carry_ref[:, :1]
    ranks = [jnp.sum(jnp.where(erow == i, before, 0.0), 0, keepdims=True) for i in idxs]
    ri_ref[...] = jnp.concatenate(idxs + ranks, axis=0).astype(jnp.int32)
    gates_t = jnp.concatenate([e * inv for e in exps] + [jnp.zeros((LANES - TOP_K, tm), F32)], axis=0)
    rg_ref[...] = jnp.transpose(gates_t)
    carry_ref[...] = carry_ref[...] + jnp.sum(onehot, 1, keepdims=True)
    cnt_ref[...] = carry_ref[...]


def _post_mix_call(o, h0, wo, g, b, rwt, rbt, tm):
    t = o.shape[0]
    const = lambda i: (0, 0)
    return pl.pallas_call(
        functools.partial(_post_mix_kernel, tm=tm),
        grid=(t // tm,),
        in_specs=[
            pl.BlockSpec((tm, D_MODEL), lambda i: (i, 0)),
            pl.BlockSpec((tm, D_MODEL), lambda i: (i, 0)),
            pl.BlockSpec((D_MODEL, D_MODEL), const),
            pl.BlockSpec((1, D_MODEL), const),
            pl.BlockSpec((1, D_MODEL), const),
            pl.BlockSpec((LANES, D_MODEL), const),
            pl.BlockSpec((LANES, 1), const),
        ],
        out_specs=[
            pl.BlockSpec((tm, D_MODEL), lambda i: (i, 0)),
            pl.BlockSpec((tm * PK_ROWS, LANES), lambda i: (i, 0)),
            pl.BlockSpec((2 * TOP_K, tm), lambda i: (0, i)),
            pl.BlockSpec((tm, LANES), lambda i: (i, 0)),
            pl.BlockSpec((N_EXPERTS, LANES), const),
        ],
        out_shape=[
            jax.ShapeDtypeStruct((t, D_MODEL), F32),
            jax.ShapeDtypeStruct((t * PK_ROWS, LANES), jnp.uint32),
            jax.ShapeDtypeStruct((2 * TOP_K, t), jnp.int32),
            jax.ShapeDtypeStruct((t, LANES), F32),
            jax.ShapeDtypeStruct((N_EXPERTS, LANES), F32),
        ],
        scratch_shapes=[pltpu.VMEM((tm, tm), MXU_DTYPE), pltpu.VMEM((N_EXPERTS, LANES), F32)],
        compiler_params=pltpu.CompilerParams(
            dimension_semantics=("arbitrary",), vmem_limit_bytes=VMEM_LIMIT),
        name="post_mix_router",
    )(o, h0, wo, g, b, rwt, rbt)


def _step_tables(dest, tm):
    k, t = dest.shape
    return dest.reshape(k, t // tm, tm).transpose(1, 0, 2).reshape(t // tm, 1, k * tm)


def _dispatch_kernel(pad_row_ref, pad_n_ref, dest_ref, h1p_ref, w1_ref, w2_ref,
                     xs_hbm, w1c_ref, w2c_ref, stage_ref, zero_ref, sem, *, tm):
    i = pl.program_id(0)

    def tile(ref, t):
        return ref.at[pl.ds(pl.multiple_of(t * PK_ROWS, PK_ROWS), PK_ROWS), :]

    slot = i % 2
    rows = tm * PK_ROWS
    stage = stage_ref.at[pl.ds(pl.multiple_of(slot * rows, rows), rows), :]
    stage[...] = h1p_ref[...]

    def body(it, c):
        for u in range(DMA_UNROLL):
            r = it * DMA_UNROLL + u
            for k in range(TOP_K):
                pltpu.make_async_copy(tile(stage, r), tile(xs_hbm, dest_ref[0, 0, k * tm + r]),
                                      sem.at[slot]).start(priority=k % 2)
        return c
    lax.fori_loop(0, tm // DMA_UNROLL, body, 0)
    w1c_ref[...] = _mx(w1_ref[...])
    w2c_ref[...] = _mx(w2_ref[...])

    def drain(s):
        for _ in range(TOP_K):
            pltpu.make_async_copy(h1p_ref, xs_hbm.at[pl.ds(0, rows), :], sem.at[s]).wait()

    @pl.when(i > 0)
    def _():
        drain(1 - slot)

    @pl.when(i == pl.num_programs(0) - 1)
    def _():
        drain(slot)
        zero_ref[...] = jnp.zeros_like(zero_ref)
        zero = zero_ref.at[pl.ds(0, PK_ROWS), :]

        def per_expert(e, c):
            def start(r, c2):
                pltpu.make_async_copy(zero, tile(xs_hbm, pad_row_ref[e] + r), sem.at[2]).start()
                return c2
            lax.fori_loop(0, pad_n_ref[e], start, 0)

            def wait(r, c2):
                pltpu.make_async_copy(zero, tile(xs_hbm, 0), sem.at[2]).wait()
                return c2
            lax.fori_loop(0, pad_n_ref[e], wait, 0)
            return c
        lax.fori_loop(0, pad_row_ref.shape[0], per_expert, 0)


def _dispatch_call(pad_row, pad_n, dest, h1p, w1, w2, n_slots, n_steps):
    t = h1p.shape[0] // PK_ROWS
    tm = t // n_steps
    r1, r2 = w1.shape[0] // n_steps, w2.shape[0] // n_steps
    step = lambda i, pr, pn: (i, 0)
    grid_spec = pltpu.PrefetchScalarGridSpec(
        num_scalar_prefetch=2,
        grid=(n_steps,),
        in_specs=[
            pl.BlockSpec((1, 1, TOP_K * tm), lambda i, pr, pn: (i, 0, 0), memory_space=pltpu.SMEM),
            pl.BlockSpec((tm * PK_ROWS, LANES), step),
            pl.BlockSpec((r1, w1.shape[1]), step),
            pl.BlockSpec((r2, w2.shape[1]), step),
        ],
        out_specs=[
            pl.BlockSpec(memory_space=pl.ANY),
            pl.BlockSpec((r1, w1.shape[1]), step),
            pl.BlockSpec((r2, w2.shape[1]), step),
        ],
        scratch_shapes=[pltpu.VMEM((2 * tm * PK_ROWS, LANES), jnp.uint32),
                        pltpu.VMEM((F32_ROWS, LANES), jnp.uint32), pltpu.SemaphoreType.DMA((3,))],
    )
    return pl.pallas_call(
        functools.partial(_dispatch_kernel, tm=tm),
        grid_spec=grid_spec,
        out_shape=[
            jax.ShapeDtypeStruct((n_slots * PK_ROWS, LANES), jnp.uint32),
            jax.ShapeDtypeStruct(w1.shape, MXU_DTYPE),
            jax.ShapeDtypeStruct(w2.shape, MXU_DTYPE),
        ],
        compiler_params=pltpu.CompilerParams(
            dimension_semantics=("arbitrary",), vmem_limit_bytes=VMEM_LIMIT),
        name="dispatch",
    )(pad_row, pad_n, _step_tables(dest, tm), h1p, w1, w2)


def _expert_kernel(be_ref, nr_ref, x_ref, *refs):
    y_ref = refs[-1]
    rows = MOE_BLOCK * PK_ROWS
    g0 = pl.program_id(0) * BLOCKS_PER_STEP
    fused = (be_ref[g0] == be_ref[g0 + 1]) & (g0 + 1 < nr_ref[0])

    def ffn(x_blk, y_blk, n, w1_ref, b1_ref, w2_ref, b2_ref):
        x = _mx(_unpack_pairs(_load_token_tiles(x_blk, n, PK_ROWS)))
        hh = _dot(x, w1_ref[0]) + b1_ref[0]
        x_glu = jnp.minimum(hh[:, :D_FF], SWIGLU_LIMIT)
        x_lin = jnp.clip(hh[:, D_FF:], -SWIGLU_LIMIT, SWIGLU_LIMIT)
        act = x_glu * (1.0 / (1.0 + jnp.exp(-SWIGLU_ALPHA * x_glu))) * (x_lin + 1.0)
        _store_token_tiles(y_blk, _pack_pairs(_dot(_mx(act), w2_ref[0]) + b2_ref[0]))

    @pl.when(fused)
    def _():
        ffn(x_ref, y_ref, BLOCKS_PER_STEP * MOE_BLOCK, *refs[0:4])

    @pl.when(jnp.logical_not(fused))
    def _():
        for half in range(BLOCKS_PER_STEP):
            weights = refs[4 * half:4 * half + 4]
            g = g0 + half
            x_blk = x_ref.at[pl.ds(half * rows, rows), :]
            y_blk = y_ref.at[pl.ds(half * rows, rows), :]

            @pl.when(g >= nr_ref[0])
            def _():
                y_blk[...] = jnp.zeros_like(y_blk)

            @pl.when(g < nr_ref[0])
            def _():
                ffn(x_blk, y_blk, MOE_BLOCK, *weights)


def _expert_call(block_expert, n_real, xs, w1, b1, w2, b2):
    n_blocks = block_expert.shape[0]
    assert BLOCKS_PER_STEP == 2 and n_blocks % BLOCKS_PER_STEP == 0
    step = lambda i, be, nr: (i, 0)
    rows = BLOCKS_PER_STEP * MOE_BLOCK * PK_ROWS
    expert_specs, expert_args = [], []
    for half in range(BLOCKS_PER_STEP):
        ex = functools.partial(lambda i, be, nr, half: (be[i * BLOCKS_PER_STEP + half], 0, 0), half=half)
        expert_specs += [pl.BlockSpec((1, D_MODEL, 2 * D_FF), ex), pl.BlockSpec((1, 1, 2 * D_FF), ex),
                         pl.BlockSpec((1, D_FF, D_MODEL), ex), pl.BlockSpec((1, 1, D_MODEL), ex)]
        expert_args += [w1, b1, w2, b2]
    grid_spec = pltpu.PrefetchScalarGridSpec(
        num_scalar_prefetch=2,
        grid=(n_blocks // BLOCKS_PER_STEP,),
        in_specs=[pl.BlockSpec((rows, LANES), step)] + expert_specs,
        out_specs=pl.BlockSpec((rows, LANES), step),
    )
    return pl.pallas_call(
        _expert_kernel,
        grid_spec=grid_spec,
        out_shape=jax.ShapeDtypeStruct((n_blocks * MOE_BLOCK * PK_ROWS, LANES), jnp.uint32),
        compiler_params=pltpu.CompilerParams(
            dimension_semantics=("arbitrary",), vmem_limit_bytes=VMEM_LIMIT),
        name="moe_experts",
    )(block_expert, n_real, xs, *expert_args)


def _gather_combine_kernel(dest0_ref, destn_ref, gate_ref, h1_ref, g_ref, b_ref, ys_hbm,
                           out_ref, ybuf, sem, *, tm):
    i = pl.program_id(0)
    slot = i % 2
    rows = tm * PK_ROWS

    def tile(ref, t):
        return ref.at[pl.ds(pl.multiple_of(t * PK_ROWS, PK_ROWS), PK_ROWS), :]

    def gather(tbl, s):
        def body(it, c):
            for u in range(DMA_UNROLL):
                r = it * DMA_UNROLL + u
                for k in range(TOP_K):
                    pltpu.make_async_copy(tile(ys_hbm, tbl[0, 0, k * tm + r]),
                                          tile(ybuf, (s * TOP_K + k) * tm + r), sem.at[s]).start(priority=k % 2)
            return c
        lax.fori_loop(0, tm // DMA_UNROLL, body, 0)

    @pl.when(i == 0)
    def _():
        gather(dest0_ref, 0)

    @pl.when(i + 1 < pl.num_programs(0))
    def _():
        gather(destn_ref, 1 - slot)

    for k in range(TOP_K):
        plane = ybuf.at[pl.ds(pl.multiple_of((slot * TOP_K + k) * rows, rows), rows), :]
        pltpu.make_async_copy(ys_hbm.at[pl.ds(0, rows), :], plane, sem.at[slot]).wait()

    gates = gate_ref[...]
    acc = DEEPNORM_ALPHA * h1_ref[...]
    for k in range(TOP_K):
        plane = ybuf.at[pl.ds(pl.multiple_of((slot * TOP_K + k) * rows, rows), rows), :]
        acc = acc + gates[:, k:k + 1] * _unpack_pairs(_load_token_tiles(plane, tm, PK_ROWS))
    out_ref[...] = _layer_norm(acc, g_ref[...], b_ref[...])


def _gather_combine_call(dest, gates, h1, g, b, ys, tm):
    t = h1.shape[0]
    nt = t // tm
    const = lambda i: (0, 0)
    smem = functools.partial(pl.BlockSpec, memory_space=pltpu.SMEM)
    dest = _step_tables(dest, tm)
    dest_next = jnp.concatenate([dest[1:], dest[:1]], axis=0)
    return pl.pallas_call(
        functools.partial(_gather_combine_kernel, tm=tm),
        grid=(nt,),
        in_specs=[
            smem((1, 1, TOP_K * tm), lambda i: (0, 0, 0)),
            smem((1, 1, TOP_K * tm), lambda i: (i, 0, 0)),
            pl.BlockSpec((tm, LANES), lambda i: (i, 0)),
            pl.BlockSpec((tm, D_MODEL), lambda i: (i, 0)),
            pl.BlockSpec((1, D_MODEL), const),
            pl.BlockSpec((1, D_MODEL), const),
            pl.BlockSpec(memory_space=pl.ANY),
        ],
        out_specs=pl.BlockSpec((tm, D_MODEL), lambda i: (i, 0)),
        out_shape=jax.ShapeDtypeStruct((t, D_MODEL), F32),
        scratch_shapes=[pltpu.VMEM((2 * TOP_K * tm * PK_ROWS, LANES), jnp.uint32),
                        pltpu.SemaphoreType.DMA((2,))],
        compiler_params=pltpu.CompilerParams(
            dimension_semantics=("arbitrary",), vmem_limit_bytes=VMEM_LIMIT),
        name="gather_combine_ln2",
    )(dest, dest_next, gates, h1, g, b, ys)


def _relayout_w_in(w):
    sizes = (256, 256, 512, 512, GATE_RANK, 256, 256, 512, 512)
    offs = np.concatenate([[0], np.cumsum(sizes)])
    gq, gk, gv, gg, glr, rq, rk, rv, rg = [w[:, offs[i]:offs[i + 1]] for i in range(9)]
    perm = np.zeros((256,), np.int32)
    for p in range(2):
        for l in range(LANES):
            part, hh, f = l // 64, (l % 64) // 32, l % 32
            perm[p * LANES + l] = (2 * p + hh) * DK + part * (DK // 2) + f
    glr = jnp.pad(glr, ((0, 0), (0, LANES - GATE_RANK)))
    return jnp.concatenate([gq, gk, gv, gg, glr, rq[:, perm], rk[:, perm], rv, rg], axis=1)


def kernel(x, positions, ln_in_g, ln_in_b, w_in, gla_gate_w, gla_gate_b, gla_norm_g, ret_norm_g,
           ret_norm_b, w_out, ln1_g, ln1_b, router_w, router_b, moe_w1, moe_b1, moe_w2, moe_b2,
           ln2_g, ln2_b):
    bsz, seq, d = x.shape
    assert d == D_MODEL and seq % GROUP == 0 and w_in.shape[0] == DEPTH == 1
    t = bsz * seq
    ts = 1024 if seq % 1024 == 0 else GROUP
    tm = min(512, t)
    assert seq % ts == 0 and t % tm == 0
    row = lambda v: v.reshape(1, -1).astype(F32)

    w_p = _relayout_w_in(w_in[0]).astype(MXU_DTYPE)
    h0, proj = _ln_proj_call(x.reshape(t, d), row(ln_in_g), row(ln_in_b), w_p, tm)

    gw = jnp.pad(gla_gate_w[0], ((0, LANES - GATE_RANK), (0, 0))).astype(MXU_DTYPE)
    half = DK // 2
    inv_freq = 1.0 / (ROPE_BASE ** np.linspace(0.0, 1.0, half, dtype=np.float32))
    invf = jnp.asarray(np.tile(inv_freq, LANES // half).reshape(1, LANES), F32)
    sgn = jnp.asarray(np.where(np.arange(LANES) < LANES // 2, -1.0, 1.0).reshape(1, LANES), F32)
    step_ang = np.arange(GROUP, dtype=np.float64)[:, None] * np.tile(inv_freq, LANES // half)[None, :]
    cd = jnp.asarray(np.cos(step_ang), F32)
    sd = jnp.asarray(np.sin(step_ang), F32)
    o = _mixer_call(proj, positions.reshape(t, 1), gw, row(gla_gate_b[0]), row(gla_norm_g[0]),
                    row(ret_norm_g[0]), row(ret_norm_b[0]), invf, sgn, cd, sd, bsz, seq, ts)

    rwt = jnp.pad(router_w[0].T, ((0, LANES - N_EXPERTS), (0, 0))).astype(MXU_DTYPE)
    rbt = jnp.pad(router_b[0], (0, LANES - N_EXPERTS)).reshape(LANES, 1).astype(F32)
    h1, h1t, r_i, r_g, cnt = _post_mix_call(o, h0, w_out[0].astype(MXU_DTYPE), row(ln1_g[0]),
                                       row(ln1_b[0]), rwt, rbt, tm)

    e = r_i[:TOP_K].reshape(-1)
    rank = r_i[TOP_K:].reshape(-1)
    counts = cnt[:, 0].astype(jnp.int32)
    padded = ((counts + MOE_BLOCK - 1) // MOE_BLOCK) * MOE_BLOCK
    pad_end = jnp.cumsum(padded)
    pad_start = pad_end - padded
    tk = t * TOP_K
    n_blocks = (tk + MOE_BLOCK - 1) // MOE_BLOCK + N_EXPERTS
    i32 = jnp.int32
    dest = rank
    for j in range(N_EXPERTS):
        dest = dest + jnp.where(e == j, pad_start[j], 0)
    dest = dest.reshape(TOP_K, t)
    block_start = jnp.arange(n_blocks, dtype=i32) * MOE_BLOCK
    block_expert = jnp.minimum(jnp.sum(block_start[:, None] >= pad_end[None, :], -1),
                               N_EXPERTS - 1).astype(i32)
    n_real = (pad_end[-1:] // MOE_BLOCK).astype(i32)

    n_slots = n_blocks * MOE_BLOCK
    hole_row = jnp.concatenate([pad_start + counts, pad_end[-1:]]).astype(i32)
    hole_n = jnp.concatenate([padded - counts, n_slots - pad_end[-1:]]).astype(i32)
    xs, w1c, w2c = _dispatch_call(hole_row, hole_n, dest, h1t,
                                  moe_w1[0].reshape(N_EXPERTS * D_MODEL, 2 * D_FF),
                                  moe_w2[0].reshape(N_EXPERTS * D_FF, D_MODEL), n_slots, DISPATCH_STEPS)
    ys = _expert_call(block_expert, n_real, xs,
                      w1c.reshape(N_EXPERTS, D_MODEL, 2 * D_FF), moe_b1[0].reshape(N_EXPERTS, 1, 2 * D_FF),
                      w2c.reshape(N_EXPERTS, D_FF, D_MODEL), moe_b2[0].reshape(N_EXPERTS, 1, D_MODEL))

    out = _gather_combine_call(dest, r_g, h1, row(ln2_g[0]), row(ln2_b[0]), ys, min(256, t))
    return out.reshape(bsz, seq, d)
```

```python
import functools
import math

import numpy as np
import jax
import jax.numpy as jnp
from jax import lax
from jax.experimental import pallas as pl
from jax.experimental.pallas import tpu as pltpu

F32 = jnp.float32
MXU_DTYPE = jnp.bfloat16

D_MODEL = 1024
CHUNK = 64
GC = 4
GROUP = GC * CHUNK
HEADS = 4
DK = 64
DV = 128
GATE_RANK = 16
GATE_NORM = 16.0
ROPE_BASE = 10000.0
N_EXPERTS = 32
TOP_K = 4
D_FF = 1024
SWIGLU_ALPHA = 1.702
SWIGLU_LIMIT = 7.0
MOE_BLOCK = 256
ROW_TILE = 512
SEQ_TILE = 1024
COMBINE_TILE = 256
LN_PROJ_SLABS = 4
BLOCKS_PER_STEP = 2
DMA_UNROLL = 8
DISPATCH_STEPS = 64
LN_EPS = 1e-5
DEPTH = 1
DEEPNORM_ALPHA = (2.0 * DEPTH) ** 0.25

LANES = 128
VMEM_LIMIT = 56 * 1024 * 1024

C_GQ, C_GK, C_GV, C_GG, C_GLR = 0, 256, 512, 1024, 1536
C_RQ, C_RK, C_RV, C_RG = 1664, 1920, 2176, 2688
PROJ_W = 3200

LOG_GAMMA = [math.log1p(-(2.0 ** (-5.0 - h))) for h in range(HEADS)]


def _dot(a, b):
    return jnp.dot(a, b, preferred_element_type=F32)


def _dot_nt(a, b):
    return lax.dot_general(a, b, (((1,), (1,)), ((), ())), preferred_element_type=F32)


def _dot_tn(a, b):
    return lax.dot_general(a, b, (((0,), (0,)), ((), ())), preferred_element_type=F32)


def _mx(a):
    return a.astype(MXU_DTYPE)


def _layer_norm(x, g, b):
    mu = jnp.mean(x, -1, keepdims=True)
    xc = x - mu
    var = jnp.mean(xc * xc, -1, keepdims=True)
    return xc * lax.rsqrt(var + LN_EPS) * g + b


def _silu(x):
    return x * (1.0 / (1.0 + jnp.exp(-x)))


F32_ROWS = D_MODEL // LANES
PK_ROWS = D_MODEL // (2 * LANES)


def _load_token_tiles(ref, n, r):
    return jnp.concatenate([ref[pl.ds(j, n, stride=r), :] for j in range(r)], axis=1)


def _store_token_tiles(ref, rows):
    n, r = rows.shape[0], rows.shape[1] // LANES
    for j in range(r):
        ref[pl.ds(j, n, stride=r), :] = rows[:, j * LANES:(j + 1) * LANES]


def _pack_pairs(x):
    h = x.shape[1] // 2
    r = x.astype(jnp.bfloat16).astype(F32)
    lo = lax.bitcast_convert_type(r[:, :h], jnp.uint32) >> 16
    hi = lax.bitcast_convert_type(r[:, h:], jnp.uint32) & jnp.uint32(0xFFFF0000)
    return lo | hi


def _unpack_pairs(u):
    lo = lax.bitcast_convert_type(u << 16, F32)
    hi = lax.bitcast_convert_type(u & jnp.uint32(0xFFFF0000), F32)
    return jnp.concatenate([lo, hi], axis=1)


def _ln_proj_kernel(x_ref, g_ref, b_ref, w_ref, h_ref, p_ref):
    n = x_ref.shape[0] // LN_PROJ_SLABS
    for j in range(LN_PROJ_SLABS):
        rs = slice(j * n, (j + 1) * n)
        h = _layer_norm(x_ref[rs, :], g_ref[...], b_ref[...])
        h_ref[rs, :] = h
        p_ref[rs, :] = _dot(_mx(h), w_ref[...])


def _ln_proj_call(x2, g, b, w, tm):
    t = x2.shape[0]
    return pl.pallas_call(
        _ln_proj_kernel,
        grid=(t // tm,),
        in_specs=[
            pl.BlockSpec((tm, D_MODEL), lambda i: (i, 0)),
            pl.BlockSpec((1, D_MODEL), lambda i: (0, 0)),
            pl.BlockSpec((1, D_MODEL), lambda i: (0, 0)),
            pl.BlockSpec((D_MODEL, PROJ_W), lambda i: (0, 0)),
        ],
        out_specs=[
            pl.BlockSpec((tm, D_MODEL), lambda i: (i, 0)),
            pl.BlockSpec((tm, PROJ_W), lambda i: (i, 0)),
        ],
        out_shape=[
            jax.ShapeDtypeStruct((t, D_MODEL), F32),
            jax.ShapeDtypeStruct((t, PROJ_W), F32),
        ],
        compiler_params=pltpu.CompilerParams(
            dimension_semantics=("arbitrary",), vmem_limit_bytes=VMEM_LIMIT),
        name="ln_proj",
    )(x2, g, b, w)


def _mixer_kernel(p_ref, pos_ref, gw_ref, gb_ref, gn_ref, rng_ref, rnb_ref, invf_ref, sgn_ref,
                  cd_ref, sd_ref, o_ref, st_ref, *, n_groups):
    @pl.when(pl.program_id(1) == 0)
    def _():
        st_ref[...] = jnp.zeros_like(st_ref)

    rr = lax.broadcasted_iota(jnp.int32, (GROUP, GROUP), 0)
    cc = lax.broadcasted_iota(jnp.int32, (GROUP, GROUP), 1)
    same = (rr >> 6) == (cc >> 6)
    lower = same & (rr >= cc)
    upper = same & (rr < cc)
    tri = lower.astype(MXU_DTYPE)
    dist = jnp.abs(rr - cc).astype(F32)
    rowi = lax.broadcasted_iota(jnp.int32, (GROUP, LANES), 0)
    rin = (rowi & (CHUNK - 1)).astype(F32)
    in_chunk = [(rowi >> 6) == c for c in range(GC)]
    lane = lax.broadcasted_iota(jnp.int32, (GROUP, LANES), 1)
    gla_half = [((lane >> 6) & 1) == i for i in range(2)]
    ret_half = [((lane >> 5) & 1) == i for i in range(2)]
    ret_d = [jnp.where(same, jnp.exp(LOG_GAMMA[h] * dist), 0.0) for h in range(HEADS)]
    ret_eb = [jnp.exp(LOG_GAMMA[h] * (rin + 1.0)) for h in range(HEADS)]
    ret_ek = [jnp.exp(LOG_GAMMA[h] * (CHUNK - 1.0 - rin)) for h in range(HEADS)]
    ret_dec = [math.exp(LOG_GAMMA[h] * CHUNK) for h in range(HEADS)]

    def block_diag(x):
        return _mx(jnp.concatenate([jnp.where(in_chunk[c], x, 0.0) for c in range(GC)], axis=1))

    lower2 = jnp.concatenate([lower, lower], axis=0)
    upper2 = jnp.concatenate([upper, upper], axis=0)
    ret_d2 = [jnp.concatenate([ret_d[2 * p], ret_d[2 * p + 1]], axis=0) for p in range(2)]
    zeros_v = jnp.zeros((GROUP, DV), MXU_DTYPE)

    def pair_products(hs0, q_m, sc, v2, v2_kv, k_bd, decay):
        sc_b = _mx(sc)
        v_bd = jnp.concatenate([jnp.concatenate([v2[:, :DV], zeros_v], axis=1),
                                jnp.concatenate([zeros_v, v2[:, DV:]], axis=1)], axis=0)
        o_intra = _dot(jnp.concatenate([sc_b[:GROUP], sc_b[GROUP:]], axis=1), v_bd)
        kv_t = _dot_tn(v2_kv, k_bd)
        st = [st_ref[hs0], st_ref[hs0 + 1]]
        parts = []
        for c in range(GC):
            cr = slice(c * CHUNK, (c + 1) * CHUNK)
            qc = jnp.concatenate([q_m[0][cr], q_m[1][cr]], axis=0)
            r = _dot_nt(qc, _mx(jnp.concatenate(st, axis=0)))
            parts.append(jnp.concatenate([r[:CHUNK, :DV], r[CHUNK:, DV:]], axis=1))
            for i in range(2):
                st[i] = st[i] * decay(c, i) + kv_t[i * DV:(i + 1) * DV, c * LANES:(c + 1) * LANES]
        st_ref[hs0] = st[0]
        st_ref[hs0 + 1] = st[1]
        return o_intra, jnp.concatenate(parts, axis=0)

    def group_body(g, carry):
        rows = pl.ds(pl.multiple_of(g * GROUP, GROUP), GROUP)

        z = _dot(_mx(p_ref[rows, C_GLR:C_GLR + LANES]), gw_ref[...]) + gb_ref[...]
        la = (jnp.minimum(z, 0.0) - jnp.log1p(jnp.exp(-jnp.abs(z)))) * (1.0 / GATE_NORM)
        la_hi = _mx(la)
        r1 = la - la_hi.astype(F32)
        la_mid = _mx(r1)
        la_lo = _mx(r1 - la_mid.astype(F32))
        b3 = _dot(tri, jnp.concatenate([la_hi, la_mid, la_lo], axis=1))
        b = b3[:, :256] + b3[:, 256:512] + b3[:, 512:]
        b_ends = [b[(c + 1) * CHUNK - 1:(c + 1) * CHUNK, :] for c in range(GC)]
        b_last = jnp.concatenate([jnp.broadcast_to(e, (CHUNK, 256)) for e in b_ends], axis=0)
        decs = [jnp.exp(e) for e in b_ends]
        eb = jnp.exp(b)
        enb = jnp.exp(-b)
        ekv = jnp.exp(b_last - b)
        q = p_ref[rows, C_GQ:C_GQ + 256] * (DK ** -0.5)
        k = p_ref[rows, C_GK:C_GK + 256]
        qe, qn = q * eb, q * enb
        ke, kn, kk = k * eb, k * enb, k * ekv
        for p in range(2):
            ls = slice(p * LANES, (p + 1) * LANES)
            qe_m = [_mx(jnp.where(gla_half[i], qe[:, ls], 0.0)) for i in range(2)]
            qn_m = [_mx(jnp.where(gla_half[i], qn[:, ls], 0.0)) for i in range(2)]
            s_lo = _dot_nt(jnp.concatenate(qe_m, axis=0), _mx(kn[:, ls]))
            s_up = _dot_nt(jnp.concatenate(qn_m, axis=0), _mx(ke[:, ls]))
            sc = jnp.where(lower2, s_lo, jnp.where(upper2, s_up, 0.0))
            v2 = _mx(p_ref[rows, C_GV + 2 * p * DV:C_GV + (2 * p + 2) * DV])
            o_intra, o_inter = pair_products(2 * p, qe_m, sc, v2, v2, block_diag(kk[:, ls]),
                                             lambda c, i: decs[c][:, ls])
            o2 = o_intra + o_inter
            for i in range(2):
                h = 2 * p + i
                o = o2[:, i * DV:(i + 1) * DV]
                o = o * lax.rsqrt(jnp.mean(o * o, -1, keepdims=True) + LN_EPS) * gn_ref[...]
                o = o * _silu(p_ref[rows, C_GG + h * DV:C_GG + (h + 1) * DV])
                o_ref[rows, h * DV:(h + 1) * DV] = o.astype(o_ref.dtype)

        base = pos_ref[pl.ds(pl.multiple_of(g * GROUP, GROUP), 1), :].astype(F32) * invf_ref[...]
        c0, s0 = jnp.cos(base), jnp.sin(base)
        cs = c0 * cd_ref[...] - s0 * sd_ref[...]
        sn = (s0 * sgn_ref[...]) * cd_ref[...] + (c0 * sgn_ref[...]) * sd_ref[...]
        for p in range(2):
            tq = p_ref[rows, C_RQ + p * LANES:C_RQ + (p + 1) * LANES]
            tk = p_ref[rows, C_RK + p * LANES:C_RK + (p + 1) * LANES] * (DK ** -0.5)
            rq = tq * cs + pltpu.roll(tq, LANES // 2, 1) * sn
            rk = tk * cs + pltpu.roll(tk, LANES // 2, 1) * sn
            q_m = [_mx(jnp.where(ret_half[i], rq, 0.0)) for i in range(2)]
            s = _dot_nt(jnp.concatenate(q_m, axis=0), _mx(rk)) * ret_d2[p]
            v2f = p_ref[rows, C_RV + 2 * p * DV:C_RV + (2 * p + 2) * DV]
            v2_kv = jnp.concatenate([v2f[:, :DV] * ret_ek[2 * p], v2f[:, DV:] * ret_ek[2 * p + 1]], axis=1)
            o_intra, o_inter = pair_products(HEADS + 2 * p, q_m, s, _mx(v2f), _mx(v2_kv), block_diag(rk),
                                             lambda c, i: ret_dec[2 * p + i])
            for i in range(2):
                h = 2 * p + i
                o = o_intra[:, i * DV:(i + 1) * DV] + ret_eb[h] * o_inter[:, i * DV:(i + 1) * DV]
                mu = jnp.mean(o, -1, keepdims=True)
                oc = o - mu
                var = jnp.mean(oc * oc, -1, keepdims=True)
                o = oc * lax.rsqrt(var + LN_EPS) * rng_ref[:, h * DV:(h + 1) * DV] \
                    + rnb_ref[:, h * DV:(h + 1) * DV]
                o = o * _silu(p_ref[rows, C_RG + h * DV:C_RG + (h + 1) * DV])
                o_ref[rows, (HEADS + h) * DV:(HEADS + h + 1) * DV] = o.astype(o_ref.dtype)
        return carry

    lax.fori_loop(0, n_groups, group_body, 0)


def _mixer_call(proj, pos, gw, gb, gn, rng, rnb, invf, sgn, cd, sd, bsz, seq, ts):
    t = bsz * seq
    nst = seq // ts
    const = lambda b, s: (0, 0)
    return pl.pallas_call(
        functools.partial(_mixer_kernel, n_groups=ts // GROUP),
        grid=(bsz, nst),
        in_specs=[
            pl.BlockSpec((ts, PROJ_W), lambda b, s: (b * nst + s, 0)),
            pl.BlockSpec((ts, 1), lambda b, s: (b * nst + s, 0)),
            pl.BlockSpec((LANES, 256), const),
            pl.BlockSpec((1, 256), const),
            pl.BlockSpec((1, DV), const),
            pl.BlockSpec((1, HEADS * DV), const),
            pl.BlockSpec((1, HEADS * DV), const),
            pl.BlockSpec((1, LANES), const),
            pl.BlockSpec((1, LANES), const),
            pl.BlockSpec((GROUP, LANES), const),
            pl.BlockSpec((GROUP, LANES), const),
        ],
        out_specs=pl.BlockSpec((ts, D_MODEL), lambda b, s: (b * nst + s, 0)),
        out_shape=jax.ShapeDtypeStruct((t, D_MODEL), MXU_DTYPE),
        scratch_shapes=[pltpu.VMEM((2 * HEADS, DV, LANES), F32)],
        compiler_params=pltpu.CompilerParams(
            dimension_semantics=("arbitrary", "arbitrary"), vmem_limit_bytes=VMEM_LIMIT),
        name="mixers",
    )(proj, pos, gw, gb, gn, rng, rnb, invf, sgn, cd, sd)


def _post_mix_kernel(o_ref, h0_ref, wo_ref, g_ref, b_ref, rwt_ref, rbt_ref,
                     h1_ref, h1t_ref, ri_ref, rg_ref, cnt_ref, triu_ref, carry_ref, *, tm):
    @pl.when(pl.program_id(0) == 0)
    def _():
        r = lax.broadcasted_iota(jnp.int32, (tm, tm), 0)
        c = lax.broadcasted_iota(jnp.int32, (tm, tm), 1)
        triu_ref[...] = (r < c).astype(MXU_DTYPE)
        carry_ref[...] = jnp.zeros_like(carry_ref)

    mix = _dot(o_ref[...], wo_ref[...])
    h1 = _layer_norm(DEEPNORM_ALPHA * h0_ref[...] + mix, g_ref[...], b_ref[...])
    h1_ref[...] = h1
    _store_token_tiles(h1t_ref, _pack_pairs(h1))

    l = (_dot_nt(rwt_ref[...], _mx(h1)) + rbt_ref[...])[:N_EXPERTS]
    erow = lax.broadcasted_iota(jnp.int32, (N_EXPERTS, tm), 0).astype(F32)
    vals, idxs = [], []
    for _ in range(TOP_K):
        m = jnp.max(l, 0, keepdims=True)
        i = jnp.min(jnp.where(l == m, erow, float(LANES)), 0, keepdims=True)
        vals.append(m)
        idxs.append(i)
        l = jnp.where(erow == i, -jnp.inf, l)
    exps = [jnp.exp(v - vals[0]) for v in vals]
    inv = 1.0 / (exps[0] + exps[1] + exps[2] + exps[3])

    onehot = jnp.zeros((N_EXPERTS, tm), F32)
    for i in idxs:
        onehot = onehot + (erow == i).astype(F32)
    before = _dot(_mx(onehot), triu_ref[...]) + carry_ref[:, :1]
    ranks = [jnp.sum(jnp.where(erow == i, before, 0.0), 0, keepdims=True) for i in idxs]
    ri_ref[...] = jnp.concatenate(idxs + ranks, axis=0).astype(jnp.int32)
    gates_t = jnp.concatenate([e * inv for e in exps] + [jnp.zeros((LANES - TOP_K, tm), F32)], axis=0)
    rg_ref[...] = jnp.transpose(gates_t)
    carry_ref[...] = carry_ref[...] + jnp.sum(onehot, 1, keepdims=True)
    cnt_ref[...] = carry_ref[...]


def _post_mix_call(o, h0, wo, g, b, rwt, rbt, tm):
    t = o.shape[0]
    const = lambda i: (0, 0)
    return pl.pallas_call(
        functools.partial(_post_mix_kernel, tm=tm),
        grid=(t // tm,),
        in_specs=[
            pl.BlockSpec((tm, D_MODEL), lambda i: (i, 0)),
            pl.BlockSpec((tm, D_MODEL), lambda i: (i, 0)),
            pl.BlockSpec((D_MODEL, D_MODEL), const),
            pl.BlockSpec((1, D_MODEL), const),
            pl.BlockSpec((1, D_MODEL), const),
            pl.BlockSpec((LANES, D_MODEL), const),
            pl.BlockSpec((LANES, 1), const),
        ],
        out_specs=[
            pl.BlockSpec((tm, D_MODEL), lambda i: (i, 0)),
            pl.BlockSpec((tm * PK_ROWS, LANES), lambda i: (i, 0)),
            pl.BlockSpec((2 * TOP_K, tm), lambda i: (0, i)),
            pl.BlockSpec((tm, LANES), lambda i: (i, 0)),
            pl.BlockSpec((N_EXPERTS, LANES), const),
        ],
        out_shape=[
            jax.ShapeDtypeStruct((t, D_MODEL), F32),
            jax.ShapeDtypeStruct((t * PK_ROWS, LANES), jnp.uint32),
            jax.ShapeDtypeStruct((2 * TOP_K, t), jnp.int32),
            jax.ShapeDtypeStruct((t, LANES), F32),
            jax.ShapeDtypeStruct((N_EXPERTS, LANES), F32),
        ],
        scratch_shapes=[pltpu.VMEM((tm, tm), MXU_DTYPE), pltpu.VMEM((N_EXPERTS, LANES), F32)],
        compiler_params=pltpu.CompilerParams(
            dimension_semantics=("arbitrary",), vmem_limit_bytes=VMEM_LIMIT),
        name="post_mix_router",
    )(o, h0, wo, g, b, rwt, rbt)


def _step_tables(dest, tm):
    k, t = dest.shape
    return dest.reshape(k, t // tm, tm).transpose(1, 0, 2).reshape(t // tm, 1, k * tm)


def _dispatch_kernel(pad_row_ref, pad_n_ref, dest_ref, h1p_ref, w1_ref, w2_ref,
                     xs_hbm, w1c_ref, w2c_ref, stage_ref, zero_ref, sem, *, tm):
    i = pl.program_id(0)

    def tile(ref, t):
        return ref.at[pl.ds(pl.multiple_of(t * PK_ROWS, PK_ROWS), PK_ROWS), :]

    slot = i % 2
    rows = tm * PK_ROWS
    stage = stage_ref.at[pl.ds(pl.multiple_of(slot * rows, rows), rows), :]
    stage[...] = h1p_ref[...]

    def body(it, c):
        for u in range(DMA_UNROLL):
            r = it * DMA_UNROLL + u
            for k in range(TOP_K):
                pltpu.make_async_copy(tile(stage, r), tile(xs_hbm, dest_ref[0, 0, k * tm + r]),
                                      sem.at[slot]).start(priority=k % 2)
        return c
    lax.fori_loop(0, tm // DMA_UNROLL, body, 0)
    w1c_ref[...] = _mx(w1_ref[...])
    w2c_ref[...] = _mx(w2_ref[...])

    def drain(s):
        for _ in range(TOP_K):
            pltpu.make_async_copy(h1p_ref, xs_hbm.at[pl.ds(0, rows), :], sem.at[s]).wait()

    @pl.when(i > 0)
    def _():
        drain(1 - slot)

    @pl.when(i == pl.num_programs(0) - 1)
    def _():
        drain(slot)
        zero_ref[...] = jnp.zeros_like(zero_ref)
        zero = zero_ref.at[pl.ds(0, PK_ROWS), :]

        def per_expert(e, c):
            def start(r, c2):
                pltpu.make_async_copy(zero, tile(xs_hbm, pad_row_ref[e] + r), sem.at[2]).start()
                return c2
            lax.fori_loop(0, pad_n_ref[e], start, 0)

            def wait(r, c2):
                pltpu.make_async_copy(zero, tile(xs_hbm, 0), sem.at[2]).wait()
                return c2
            lax.fori_loop(0, pad_n_ref[e], wait, 0)
            return c
        lax.fori_loop(0, pad_row_ref.shape[0], per_expert, 0)


def _dispatch_call(pad_row, pad_n, dest, h1p, w1, w2, n_slots, n_steps):
    t = h1p.shape[0] // PK_ROWS
    tm = t // n_steps
    r1, r2 = w1.shape[0] // n_steps, w2.shape[0] // n_steps
    step = lambda i, pr, pn: (i, 0)
    grid_spec = pltpu.PrefetchScalarGridSpec(
        num_scalar_prefetch=2,
        grid=(n_steps,),
        in_specs=[
            pl.BlockSpec((1, 1, TOP_K * tm), lambda i, pr, pn: (i, 0, 0), memory_space=pltpu.SMEM),
            pl.BlockSpec((tm * PK_ROWS, LANES), step),
            pl.BlockSpec((r1, w1.shape[1]), step),
            pl.BlockSpec((r2, w2.shape[1]), step),
        ],
        out_specs=[
            pl.BlockSpec(memory_space=pl.ANY),
            pl.BlockSpec((r1, w1.shape[1]), step),
            pl.BlockSpec((r2, w2.shape[1]), step),
        ],
        scratch_shapes=[pltpu.VMEM((2 * tm * PK_ROWS, LANES), jnp.uint32),
                        pltpu.VMEM((F32_ROWS, LANES), jnp.uint32), pltpu.SemaphoreType.DMA((3,))],
    )
    return pl.pallas_call(
        functools.partial(_dispatch_kernel, tm=tm),
        grid_spec=grid_spec,
        out_shape=[
            jax.ShapeDtypeStruct((n_slots * PK_ROWS, LANES), jnp.uint32),
            jax.ShapeDtypeStruct(w1.shape, MXU_DTYPE),
            jax.ShapeDtypeStruct(w2.shape, MXU_DTYPE),
        ],
        compiler_params=pltpu.CompilerParams(
            dimension_semantics=("arbitrary",), vmem_limit_bytes=VMEM_LIMIT),
        name="dispatch",
    )(pad_row, pad_n, _step_tables(dest, tm), h1p, w1, w2)


def _expert_kernel(be_ref, nr_ref, x_ref, *refs):
    y_ref = refs[-1]
    rows = MOE_BLOCK * PK_ROWS
    g0 = pl.program_id(0) * BLOCKS_PER_STEP
    fused = (be_ref[g0] == be_ref[g0 + 1]) & (g0 + 1 < nr_ref[0])

    def ffn(x_blk, y_blk, n, w1_ref, b1_ref, w2_ref, b2_ref):
        x = _mx(_unpack_pairs(_load_token_tiles(x_blk, n, PK_ROWS)))
        hh = _dot(x, w1_ref[0]) + b1_ref[0]
        x_glu = jnp.minimum(hh[:, :D_FF], SWIGLU_LIMIT)
        x_lin = jnp.clip(hh[:, D_FF:], -SWIGLU_LIMIT, SWIGLU_LIMIT)
        act = x_glu * (1.0 / (1.0 + jnp.exp(-SWIGLU_ALPHA * x_glu))) * (x_lin + 1.0)
        _store_token_tiles(y_blk, _pack_pairs(_dot(_mx(act), w2_ref[0]) + b2_ref[0]))

    @pl.when(fused)
    def _():
        ffn(x_ref, y_ref, BLOCKS_PER_STEP * MOE_BLOCK, *refs[0:4])

    @pl.when(jnp.logical_not(fused))
    def _():
        for half in range(BLOCKS_PER_STEP):
            weights = refs[4 * half:4 * half + 4]
            g = g0 + half
            x_blk = x_ref.at[pl.ds(half * rows, rows), :]
            y_blk = y_ref.at[pl.ds(half * rows, rows), :]

            @pl.when(g >= nr_ref[0])
            def _():
                y_blk[...] = jnp.zeros_like(y_blk)

            @pl.when(g < nr_ref[0])
            def _():
                ffn(x_blk, y_blk, MOE_BLOCK, *weights)


def _expert_call(block_expert, n_real, xs, w1, b1, w2, b2):
    n_blocks = block_expert.shape[0]
    assert BLOCKS_PER_STEP == 2 and n_blocks % BLOCKS_PER_STEP == 0
    step = lambda i, be, nr: (i, 0)
    rows = BLOCKS_PER_STEP * MOE_BLOCK * PK_ROWS
    expert_specs, expert_args = [], []
    for half in range(BLOCKS_PER_STEP):
        ex = functools.partial(lambda i, be, nr, half: (be[i * BLOCKS_PER_STEP + half], 0, 0), half=half)
        expert_specs += [pl.BlockSpec((1, D_MODEL, 2 * D_FF), ex), pl.BlockSpec((1, 1, 2 * D_FF), ex),
                         pl.BlockSpec((1, D_FF, D_MODEL), ex), pl.BlockSpec((1, 1, D_MODEL), ex)]
        expert_args += [w1, b1, w2, b2]
    grid_spec = pltpu.PrefetchScalarGridSpec(
        num_scalar_prefetch=2,
        grid=(n_blocks // BLOCKS_PER_STEP,),
        in_specs=[pl.BlockSpec((rows, LANES), step)] + expert_specs,
        out_specs=pl.BlockSpec((rows, LANES), step),
    )
    return pl.pallas_call(
        _expert_kernel,
        grid_spec=grid_spec,
        out_shape=jax.ShapeDtypeStruct((n_blocks * MOE_BLOCK * PK_ROWS, LANES), jnp.uint32),
        compiler_params=pltpu.CompilerParams(
            dimension_semantics=("arbitrary",), vmem_limit_bytes=VMEM_LIMIT),
        name="moe_experts",
    )(block_expert, n_real, xs, *expert_args)


def _gather_combine_kernel(dest0_ref, destn_ref, gate_ref, h1_ref, g_ref, b_ref, ys_hbm,
                           out_ref, ybuf, sem, *, tm):
    i = pl.program_id(0)
    slot = i % 2
    rows = tm * PK_ROWS

    def tile(ref, t):
        return ref.at[pl.ds(pl.multiple_of(t * PK_ROWS, PK_ROWS), PK_ROWS), :]

    def gather(tbl, s):
        def body(it, c):
            for u in range(DMA_UNROLL):
                r = it * DMA_UNROLL + u
                for k in range(TOP_K):
                    pltpu.make_async_copy(tile(ys_hbm, tbl[0, 0, k * tm + r]),
                                          tile(ybuf, (s * TOP_K + k) * tm + r), sem.at[s]).start(priority=k % 2)
            return c
        lax.fori_loop(0, tm // DMA_UNROLL, body, 0)

    @pl.when(i == 0)
    def _():
        gather(dest0_ref, 0)

    @pl.when(i + 1 < pl.num_programs(0))
    def _():
        gather(destn_ref, 1 - slot)

    for k in range(TOP_K):
        plane = ybuf.at[pl.ds(pl.multiple_of((slot * TOP_K + k) * rows, rows), rows), :]
        pltpu.make_async_copy(ys_hbm.at[pl.ds(0, rows), :], plane, sem.at[slot]).wait()

    gates = gate_ref[...]
    acc = DEEPNORM_ALPHA * h1_ref[...]
    for k in range(TOP_K):
        plane = ybuf.at[pl.ds(pl.multiple_of((slot * TOP_K + k) * rows, rows), rows), :]
        acc = acc + gates[:, k:k + 1] * _unpack_pairs(_load_token_tiles(plane, tm, PK_ROWS))
    out_ref[...] = _layer_norm(acc, g_ref[...], b_ref[...])


def _gather_combine_call(dest, gates, h1, g, b, ys, tm):
    t = h1.shape[0]
    nt = t // tm
    const = lambda i: (0, 0)
    smem = functools.partial(pl.BlockSpec, memory_space=pltpu.SMEM)
    dest = _step_tables(dest, tm)
    dest_next = jnp.concatenate([dest[1:], dest[:1]], axis=0)
    return pl.pallas_call(
        functools.partial(_gather_combine_kernel, tm=tm),
        grid=(nt,),
        in_specs=[
            smem((1, 1, TOP_K * tm), lambda i: (0, 0, 0)),
            smem((1, 1, TOP_K * tm), lambda i: (i, 0, 0)),
            pl.BlockSpec((tm, LANES), lambda i: (i, 0)),
            pl.BlockSpec((tm, D_MODEL), lambda i: (i, 0)),
            pl.BlockSpec((1, D_MODEL), const),
            pl.BlockSpec((1, D_MODEL), const),
            pl.BlockSpec(memory_space=pl.ANY),
        ],
        out_specs=pl.BlockSpec((tm, D_MODEL), lambda i: (i, 0)),
        out_shape=jax.ShapeDtypeStruct((t, D_MODEL), F32),
        scratch_shapes=[pltpu.VMEM((2 * TOP_K * tm * PK_ROWS, LANES), jnp.uint32),
                        pltpu.SemaphoreType.DMA((2,))],
        compiler_params=pltpu.CompilerParams(
            dimension_semantics=("arbitrary",), vmem_limit_bytes=VMEM_LIMIT),
        name="gather_combine_ln2",
    )(dest, dest_next, gates, h1, g, b, ys)


def _relayout_w_in(w):
    sizes = (256, 256, 512, 512, GATE_RANK, 256, 256, 512, 512)
    offs = np.concatenate([[0], np.cumsum(sizes)])
    gq, gk, gv, gg, glr, rq, rk, rv, rg = [w[:, offs[i]:offs[i + 1]] for i in range(9)]
    perm = np.zeros((256,), np.int32)
    for p in range(2):
        for l in range(LANES):
            part, hh, f = l // 64, (l % 64) // 32, l % 32
            perm[p * LANES + l] = (2 * p + hh) * DK + part * (DK // 2) + f
    glr = jnp.pad(glr, ((0, 0), (0, LANES - GATE_RANK)))
    return jnp.concatenate([gq, gk, gv, gg, glr, rq[:, perm], rk[:, perm], rv, rg], axis=1)


def kernel(x, positions, ln_in_g, ln_in_b, w_in, gla_gate_w, gla_gate_b, gla_norm_g, ret_norm_g,
           ret_norm_b, w_out, ln1_g, ln1_b, router_w, router_b, moe_w1, moe_b1, moe_w2, moe_b2,
           ln2_g, ln2_b):
    bsz, seq, d = x.shape
    assert d == D_MODEL and seq % GROUP == 0 and w_in.shape[0] == DEPTH == 1
    t = bsz * seq
    ts = SEQ_TILE if seq % SEQ_TILE == 0 else GROUP
    tm = min(ROW_TILE, t)
    assert seq % ts == 0 and t % tm == 0
    row = lambda v: v.reshape(1, -1).astype(F32)

    w_p = _relayout_w_in(w_in[0].astype(MXU_DTYPE))
    h0, proj = _ln_proj_call(x.reshape(t, d), row(ln_in_g), row(ln_in_b), w_p, tm)

    gw = jnp.pad(gla_gate_w[0], ((0, LANES - GATE_RANK), (0, 0))).astype(MXU_DTYPE)
    half = DK // 2
    inv_freq = 1.0 / (ROPE_BASE ** np.linspace(0.0, 1.0, half, dtype=np.float32))
    invf = jnp.asarray(np.tile(inv_freq, LANES // half).reshape(1, LANES), F32)
    sgn = jnp.asarray(np.where(np.arange(LANES) < LANES // 2, -1.0, 1.0).reshape(1, LANES), F32)
    step_ang = np.arange(GROUP, dtype=np.float64)[:, None] * np.tile(inv_freq, LANES // half)[None, :]
    cd = jnp.asarray(np.cos(step_ang), F32)
    sd = jnp.asarray(np.sin(step_ang), F32)
    o = _mixer_call(proj, positions.reshape(t, 1), gw, row(gla_gate_b[0]), row(gla_norm_g[0]),
                    row(ret_norm_g[0]), row(ret_norm_b[0]), invf, sgn, cd, sd, bsz, seq, ts)

    rwt = jnp.pad(router_w[0].T, ((0, LANES - N_EXPERTS), (0, 0))).astype(MXU_DTYPE)
    rbt = jnp.pad(router_b[0], (0, LANES - N_EXPERTS)).reshape(LANES, 1).astype(F32)
    h1, h1t, r_i, r_g, cnt = _post_mix_call(o, h0, w_out[0].astype(MXU_DTYPE), row(ln1_g[0]),
                                       row(ln1_b[0]), rwt, rbt, tm)

    e = r_i[:TOP_K].reshape(-1)
    rank = r_i[TOP_K:].reshape(-1)
    counts = cnt[:, 0].astype(jnp.int32)
    padded = ((counts + MOE_BLOCK - 1) // MOE_BLOCK) * MOE_BLOCK
    pad_end = jnp.cumsum(padded)
    pad_start = pad_end - padded
    tk = t * TOP_K
    n_blocks = (tk + MOE_BLOCK - 1) // MOE_BLOCK + N_EXPERTS
    i32 = jnp.int32
    dest = rank
    for j in range(N_EXPERTS):
        dest = dest + jnp.where(e == j, pad_start[j], 0)
    dest = dest.reshape(TOP_K, t)
    block_start = jnp.arange(n_blocks, dtype=i32) * MOE_BLOCK
    block_expert = jnp.minimum(jnp.sum(block_start[:, None] >= pad_end[None, :], -1),
                               N_EXPERTS - 1).astype(i32)
    n_real = (pad_end[-1:] // MOE_BLOCK).astype(i32)

    n_slots = n_blocks * MOE_BLOCK
    hole_row = jnp.concatenate([pad_start + counts, pad_end[-1:]]).astype(i32)
    hole_n = jnp.concatenate([padded - counts, n_slots - pad_end[-1:]]).astype(i32)
    xs, w1c, w2c = _dispatch_call(hole_row, hole_n, dest, h1t,
                                  moe_w1[0].reshape(N_EXPERTS * D_MODEL, 2 * D_FF),
                                  moe_w2[0].reshape(N_EXPERTS * D_FF, D_MODEL), n_slots, DISPATCH_STEPS)
    ys = _expert_call(block_expert, n_real, xs,
                      w1c.reshape(N_EXPERTS, D_MODEL, 2 * D_FF), moe_b1[0].reshape(N_EXPERTS, 1, 2 * D_FF),
                      w2c.reshape(N_EXPERTS, D_FF, D_MODEL), moe_b2[0].reshape(N_EXPERTS, 1, D_MODEL))

    out = _gather_combine_call(dest, r_g, h1, row(ln2_g[0]), row(ln2_b[0]), ys, min(COMBINE_TILE, t))
    return out.reshape(bsz, seq, d)
```

```python
import functools
import math

import numpy as np
import jax
import jax.numpy as jnp
from jax import lax
from jax.experimental import pallas as pl
from jax.experimental.pallas import tpu as pltpu

F32 = jnp.float32
MXU_DTYPE = jnp.bfloat16

D_MODEL = 1024
CHUNK = 64
GC = 4
GROUP = GC * CHUNK
HEADS = 4
DK = 64
DV = 128
GATE_RANK = 16
GATE_NORM = 16.0
ROPE_BASE = 10000.0
N_EXPERTS = 32
TOP_K = 4
D_FF = 1024
SWIGLU_ALPHA = 1.702
SWIGLU_LIMIT = 7.0
MOE_BLOCK = 256
ROW_TILE = 512
SEQ_TILE = 1024
COMBINE_TILE = 256
ROW_SLABS = 4
BLOCKS_PER_STEP = 2
DMA_UNROLL = 8
DISPATCH_STEPS = 64
LN_EPS = 1e-5
DEPTH = 1
DEEPNORM_ALPHA = (2.0 * DEPTH) ** 0.25

LANES = 128
VMEM_LIMIT = 56 * 1024 * 1024

C_GQ, C_GK, C_GV, C_GG, C_GLR = 0, 256, 512, 1024, 1536
C_RQ, C_RK, C_RV, C_RG = 1664, 1920, 2176, 2688
PROJ_W = 3200

LOG_GAMMA = [math.log1p(-(2.0 ** (-5.0 - h))) for h in range(HEADS)]


def _dot(a, b):
    return jnp.dot(a, b, preferred_element_type=F32)


def _dot_nt(a, b):
    return lax.dot_general(a, b, (((1,), (1,)), ((), ())), preferred_element_type=F32)


def _dot_tn(a, b):
    return lax.dot_general(a, b, (((0,), (0,)), ((), ())), preferred_element_type=F32)


def _mx(a):
    return a.astype(MXU_DTYPE)


def _layer_norm(x, g, b):
    mu = jnp.mean(x, -1, keepdims=True)
    xc = x - mu
    var = jnp.mean(xc * xc, -1, keepdims=True)
    return xc * lax.rsqrt(var + LN_EPS) * g + b


def _silu(x):
    return x * (1.0 / (1.0 + jnp.exp(-x)))


F32_ROWS = D_MODEL // LANES
PK_ROWS = D_MODEL // (2 * LANES)


def _load_token_tiles(ref, n, r):
    return jnp.concatenate([ref[pl.ds(j, n, stride=r), :] for j in range(r)], axis=1)


def _store_token_tiles(ref, rows):
    n, r = rows.shape[0], rows.shape[1] // LANES
    for j in range(r):
        ref[pl.ds(j, n, stride=r), :] = rows[:, j * LANES:(j + 1) * LANES]


def _pack_pairs(x):
    h = x.shape[1] // 2
    r = x.astype(jnp.bfloat16).astype(F32)
    lo = lax.bitcast_convert_type(r[:, :h], jnp.uint32) >> 16
    hi = lax.bitcast_convert_type(r[:, h:], jnp.uint32) & jnp.uint32(0xFFFF0000)
    return lo | hi


def _unpack_pairs(u):
    lo = lax.bitcast_convert_type(u << 16, F32)
    hi = lax.bitcast_convert_type(u & jnp.uint32(0xFFFF0000), F32)
    return jnp.concatenate([lo, hi], axis=1)


def _ln_proj_kernel(x_ref, g_ref, b_ref, w_ref, h_ref, p_ref):
    n = x_ref.shape[0] // ROW_SLABS
    for j in range(ROW_SLABS):
        rs = slice(j * n, (j + 1) * n)
        h = _layer_norm(x_ref[rs, :], g_ref[...], b_ref[...])
        h_ref[rs, :] = h
        p_ref[rs, :] = _dot(_mx(h), w_ref[...])


def _ln_proj_call(x2, g, b, w, tm):
    t = x2.shape[0]
    return pl.pallas_call(
        _ln_proj_kernel,
        grid=(t // tm,),
        in_specs=[
            pl.BlockSpec((tm, D_MODEL), lambda i: (i, 0)),
            pl.BlockSpec((1, D_MODEL), lambda i: (0, 0)),
            pl.BlockSpec((1, D_MODEL), lambda i: (0, 0)),
            pl.BlockSpec((D_MODEL, PROJ_W), lambda i: (0, 0)),
        ],
        out_specs=[
            pl.BlockSpec((tm, D_MODEL), lambda i: (i, 0)),
            pl.BlockSpec((tm, PROJ_W), lambda i: (i, 0)),
        ],
        out_shape=[
            jax.ShapeDtypeStruct((t, D_MODEL), F32),
            jax.ShapeDtypeStruct((t, PROJ_W), F32),
        ],
        compiler_params=pltpu.CompilerParams(
            dimension_semantics=("arbitrary",), vmem_limit_bytes=VMEM_LIMIT),
        name="ln_proj",
    )(x2, g, b, w)


def _mixer_kernel(p_ref, pos_ref, gw_ref, gb_ref, gn_ref, rng_ref, rnb_ref, invf_ref, sgn_ref,
                  cd_ref, sd_ref, o_ref, st_ref, *, n_groups):
    @pl.when(pl.program_id(1) == 0)
    def _():
        st_ref[...] = jnp.zeros_like(st_ref)

    rr = lax.broadcasted_iota(jnp.int32, (GROUP, GROUP), 0)
    cc = lax.broadcasted_iota(jnp.int32, (GROUP, GROUP), 1)
    same = (rr >> 6) == (cc >> 6)
    lower = same & (rr >= cc)
    upper = same & (rr < cc)
    tri = lower.astype(MXU_DTYPE)
    dist = jnp.abs(rr - cc).astype(F32)
    rowi = lax.broadcasted_iota(jnp.int32, (GROUP, LANES), 0)
    rin = (rowi & (CHUNK - 1)).astype(F32)
    in_chunk = [(rowi >> 6) == c for c in range(GC)]
    lane = lax.broadcasted_iota(jnp.int32, (GROUP, LANES), 1)
    gla_half = [((lane >> 6) & 1) == i for i in range(2)]
    ret_half = [((lane >> 5) & 1) == i for i in range(2)]
    ret_d = [jnp.where(same, jnp.exp(LOG_GAMMA[h] * dist), 0.0) for h in range(HEADS)]
    ret_eb = [jnp.exp(LOG_GAMMA[h] * (rin + 1.0)) for h in range(HEADS)]
    ret_ek = [jnp.exp(LOG_GAMMA[h] * (CHUNK - 1.0 - rin)) for h in range(HEADS)]
    ret_dec = [math.exp(LOG_GAMMA[h] * CHUNK) for h in range(HEADS)]

    def block_diag(x):
        return _mx(jnp.concatenate([jnp.where(in_chunk[c], x, 0.0) for c in range(GC)], axis=1))

    lower2 = jnp.concatenate([lower, lower], axis=0)
    upper2 = jnp.concatenate([upper, upper], axis=0)
    ret_d2 = [jnp.concatenate([ret_d[2 * p], ret_d[2 * p + 1]], axis=0) for p in range(2)]
    zeros_v = jnp.zeros((GROUP, DV), MXU_DTYPE)

    def pair_products(hs0, q_m, sc, v2, v2_kv, k_bd, decay):
        sc_b = _mx(sc)
        v_bd = jnp.concatenate([jnp.concatenate([v2[:, :DV], zeros_v], axis=1),
                                jnp.concatenate([zeros_v, v2[:, DV:]], axis=1)], axis=0)
        o_intra = _dot(jnp.concatenate([sc_b[:GROUP], sc_b[GROUP:]], axis=1), v_bd)
        kv_t = _dot_tn(v2_kv, k_bd)
        st = [st_ref[hs0], st_ref[hs0 + 1]]
        parts = []
        for c in range(GC):
            cr = slice(c * CHUNK, (c + 1) * CHUNK)
            qc = jnp.concatenate([q_m[0][cr], q_m[1][cr]], axis=0)
            r = _dot_nt(qc, _mx(jnp.concatenate(st, axis=0)))
            parts.append(jnp.concatenate([r[:CHUNK, :DV], r[CHUNK:, DV:]], axis=1))
            for i in range(2):
                st[i] = st[i] * decay(c, i) + kv_t[i * DV:(i + 1) * DV, c * LANES:(c + 1) * LANES]
        st_ref[hs0] = st[0]
        st_ref[hs0 + 1] = st[1]
        return o_intra, jnp.concatenate(parts, axis=0)

    def group_body(g, carry):
        rows = pl.ds(pl.multiple_of(g * GROUP, GROUP), GROUP)

        z = _dot(_mx(p_ref[rows, C_GLR:C_GLR + LANES]), gw_ref[...]) + gb_ref[...]
        la = (jnp.minimum(z, 0.0) - jnp.log1p(jnp.exp(-jnp.abs(z)))) * (1.0 / GATE_NORM)
        la_hi = _mx(la)
        r1 = la - la_hi.astype(F32)
        la_mid = _mx(r1)
        la_lo = _mx(r1 - la_mid.astype(F32))
        b3 = _dot(tri, jnp.concatenate([la_hi, la_mid, la_lo], axis=1))
        b = b3[:, :256] + b3[:, 256:512] + b3[:, 512:]
        b_ends = [b[(c + 1) * CHUNK - 1:(c + 1) * CHUNK, :] for c in range(GC)]
        b_last = jnp.concatenate([jnp.broadcast_to(e, (CHUNK, 256)) for e in b_ends], axis=0)
        decs = [jnp.exp(e) for e in b_ends]
        eb = jnp.exp(b)
        enb = jnp.exp(-b)
        ekv = jnp.exp(b_last - b)
        q = p_ref[rows, C_GQ:C_GQ + 256] * (DK ** -0.5)
        k = p_ref[rows, C_GK:C_GK + 256]
        qe, qn = q * eb, q * enb
        ke, kn, kk = k * eb, k * enb, k * ekv
        for p in range(2):
            ls = slice(p * LANES, (p + 1) * LANES)
            qe_m = [_mx(jnp.where(gla_half[i], qe[:, ls], 0.0)) for i in range(2)]
            qn_m = [_mx(jnp.where(gla_half[i], qn[:, ls], 0.0)) for i in range(2)]
            s_lo = _dot_nt(jnp.concatenate(qe_m, axis=0), _mx(kn[:, ls]))
            s_up = _dot_nt(jnp.concatenate(qn_m, axis=0), _mx(ke[:, ls]))
            sc = jnp.where(lower2, s_lo, jnp.where(upper2, s_up, 0.0))
            v2 = _mx(p_ref[rows, C_GV + 2 * p * DV:C_GV + (2 * p + 2) * DV])
            o_intra, o_inter = pair_products(2 * p, qe_m, sc, v2, v2, block_diag(kk[:, ls]),
                                             lambda c, i: decs[c][:, ls])
            o2 = o_intra + o_inter
            for i in range(2):
                h = 2 * p + i
                o = o2[:, i * DV:(i + 1) * DV]
                o = o * lax.rsqrt(jnp.mean(o * o, -1, keepdims=True) + LN_EPS) * gn_ref[...]
                o = o * _silu(p_ref[rows, C_GG + h * DV:C_GG + (h + 1) * DV])
                o_ref[rows, h * DV:(h + 1) * DV] = o.astype(o_ref.dtype)

        base = pos_ref[pl.ds(pl.multiple_of(g * GROUP, GROUP), 1), :].astype(F32) * invf_ref[...]
        c0, s0 = jnp.cos(base), jnp.sin(base)
        cs = c0 * cd_ref[...] - s0 * sd_ref[...]
        sn = (s0 * sgn_ref[...]) * cd_ref[...] + (c0 * sgn_ref[...]) * sd_ref[...]
        for p in range(2):
            tq = p_ref[rows, C_RQ + p * LANES:C_RQ + (p + 1) * LANES]
            tk = p_ref[rows, C_RK + p * LANES:C_RK + (p + 1) * LANES] * (DK ** -0.5)
            rq = tq * cs + pltpu.roll(tq, LANES // 2, 1) * sn
            rk = tk * cs + pltpu.roll(tk, LANES // 2, 1) * sn
            q_m = [_mx(jnp.where(ret_half[i], rq, 0.0)) for i in range(2)]
            s = _dot_nt(jnp.concatenate(q_m, axis=0), _mx(rk)) * ret_d2[p]
            v2f = p_ref[rows, C_RV + 2 * p * DV:C_RV + (2 * p + 2) * DV]
            v2_kv = jnp.concatenate([v2f[:, :DV] * ret_ek[2 * p], v2f[:, DV:] * ret_ek[2 * p + 1]], axis=1)
            o_intra, o_inter = pair_products(HEADS + 2 * p, q_m, s, _mx(v2f), _mx(v2_kv), block_diag(rk),
                                             lambda c, i: ret_dec[2 * p + i])
            for i in range(2):
                h = 2 * p + i
                o = o_intra[:, i * DV:(i + 1) * DV] + ret_eb[h] * o_inter[:, i * DV:(i + 1) * DV]
                mu = jnp.mean(o, -1, keepdims=True)
                oc = o - mu
                var = jnp.mean(oc * oc, -1, keepdims=True)
                o = oc * lax.rsqrt(var + LN_EPS) * rng_ref[:, h * DV:(h + 1) * DV] \
                    + rnb_ref[:, h * DV:(h + 1) * DV]
                o = o * _silu(p_ref[rows, C_RG + h * DV:C_RG + (h + 1) * DV])
                o_ref[rows, (HEADS + h) * DV:(HEADS + h + 1) * DV] = o.astype(o_ref.dtype)
        return carry

    lax.fori_loop(0, n_groups, group_body, 0)


def _mixer_call(proj, pos, gw, gb, gn, rng, rnb, invf, sgn, cd, sd, bsz, seq, ts):
    t = bsz * seq
    nst = seq // ts
    const = lambda b, s: (0, 0)
    return pl.pallas_call(
        functools.partial(_mixer_kernel, n_groups=ts // GROUP),
        grid=(bsz, nst),
        in_specs=[
            pl.BlockSpec((ts, PROJ_W), lambda b, s: (b * nst + s, 0)),
            pl.BlockSpec((ts, 1), lambda b, s: (b * nst + s, 0)),
            pl.BlockSpec((LANES, 256), const),
            pl.BlockSpec((1, 256), const),
            pl.BlockSpec((1, DV), const),
            pl.BlockSpec((1, HEADS * DV), const),
            pl.BlockSpec((1, HEADS * DV), const),
            pl.BlockSpec((1, LANES), const),
            pl.BlockSpec((1, LANES), const),
            pl.BlockSpec((GROUP, LANES), const),
            pl.BlockSpec((GROUP, LANES), const),
        ],
        out_specs=pl.BlockSpec((ts, D_MODEL), lambda b, s: (b * nst + s, 0)),
        out_shape=jax.ShapeDtypeStruct((t, D_MODEL), MXU_DTYPE),
        scratch_shapes=[pltpu.VMEM((2 * HEADS, DV, LANES), F32)],
        compiler_params=pltpu.CompilerParams(
            dimension_semantics=("arbitrary", "arbitrary"), vmem_limit_bytes=VMEM_LIMIT),
        name="mixers",
    )(proj, pos, gw, gb, gn, rng, rnb, invf, sgn, cd, sd)


def _post_mix_kernel(o_ref, h0_ref, wo_ref, g_ref, b_ref, rwt_ref, rbt_ref,
                     h1_ref, h1t_ref, ri_ref, rg_ref, cnt_ref, triu_ref, carry_ref, *, tm):
    @pl.when(pl.program_id(0) == 0)
    def _():
        r = lax.broadcasted_iota(jnp.int32, (tm, tm), 0)
        c = lax.broadcasted_iota(jnp.int32, (tm, tm), 1)
        triu_ref[...] = (r < c).astype(MXU_DTYPE)
        carry_ref[...] = jnp.zeros_like(carry_ref)

    n = tm // ROW_SLABS
    slabs = []
    for j in range(ROW_SLABS):
        rs = slice(j * n, (j + 1) * n)
        mix = _dot(o_ref[rs, :], wo_ref[...])
        hj = _layer_norm(DEEPNORM_ALPHA * h0_ref[rs, :] + mix, g_ref[...], b_ref[...])
        h1_ref[rs, :] = hj
        _store_token_tiles(h1t_ref.at[pl.ds(j * n * PK_ROWS, n * PK_ROWS), :], _pack_pairs(hj))
        slabs.append(hj)
    h1 = jnp.concatenate(slabs, axis=0)

    l = (_dot_nt(rwt_ref[...], _mx(h1)) + rbt_ref[...])[:N_EXPERTS]
    erow = lax.broadcasted_iota(jnp.int32, (N_EXPERTS, tm), 0).astype(F32)
    vals, idxs = [], []
    for _ in range(TOP_K):
        m = jnp.max(l, 0, keepdims=True)
        i = jnp.min(jnp.where(l == m, erow, float(LANES)), 0, keepdims=True)
        vals.append(m)
        idxs.append(i)
        l = jnp.where(erow == i, -jnp.inf, l)
    exps = [jnp.exp(v - vals[0]) for v in vals]
    inv = 1.0 / (exps[0] + exps[1] + exps[2] + exps[3])

    onehot = jnp.zeros((N_EXPERTS, tm), F32)
    for i in idxs:
        onehot = onehot + (erow == i).astype(F32)
    before = _dot(_mx(onehot), triu_ref[...]) + carry_ref[:, :1]
    ranks = [jnp.sum(jnp.where(erow == i, before, 0.0), 0, keepdims=True) for i in idxs]
    ri_ref[...] = jnp.concatenate(idxs + ranks, axis=0).astype(jnp.int32)
    gates_t = jnp.concatenate([e * inv for e in exps] + [jnp.zeros((LANES - TOP_K, tm), F32)], axis=0)
    rg_ref[...] = jnp.transpose(gates_t)
    carry_ref[...] = carry_ref[...] + jnp.sum(onehot, 1, keepdims=True)
    cnt_ref[...] = carry_ref[...]


def _post_mix_call(o, h0, wo, g, b, rwt, rbt, tm):
    t = o.shape[0]
    const = lambda i: (0, 0)
    return pl.pallas_call(
        functools.partial(_post_mix_kernel, tm=tm),
        grid=(t // tm,),
        in_specs=[
            pl.BlockSpec((tm, D_MODEL), lambda i: (i, 0)),
            pl.BlockSpec((tm, D_MODEL), lambda i: (i, 0)),
            pl.BlockSpec((D_MODEL, D_MODEL), const),
            pl.BlockSpec((1, D_MODEL), const),
            pl.BlockSpec((1, D_MODEL), const),
            pl.BlockSpec((LANES, D_MODEL), const),
            pl.BlockSpec((LANES, 1), const),
        ],
        out_specs=[
            pl.BlockSpec((tm, D_MODEL), lambda i: (i, 0)),
            pl.BlockSpec((tm * PK_ROWS, LANES), lambda i: (i, 0)),
            pl.BlockSpec((2 * TOP_K, tm), lambda i: (0, i)),
            pl.BlockSpec((tm, LANES), lambda i: (i, 0)),
            pl.BlockSpec((N_EXPERTS, LANES), const),
        ],
        out_shape=[
            jax.ShapeDtypeStruct((t, D_MODEL), F32),
            jax.ShapeDtypeStruct((t * PK_ROWS, LANES), jnp.uint32),
            jax.ShapeDtypeStruct((2 * TOP_K, t), jnp.int32),
            jax.ShapeDtypeStruct((t, LANES), F32),
            jax.ShapeDtypeStruct((N_EXPERTS, LANES), F32),
        ],
        scratch_shapes=[pltpu.VMEM((tm, tm), MXU_DTYPE), pltpu.VMEM((N_EXPERTS, LANES), F32)],
        compiler_params=pltpu.CompilerParams(
            dimension_semantics=("arbitrary",), vmem_limit_bytes=VMEM_LIMIT),
        name="post_mix_router",
    )(o, h0, wo, g, b, rwt, rbt)


def _step_tables(dest, tm):
    k, t = dest.shape
    return dest.reshape(k, t // tm, tm).transpose(1, 0, 2).reshape(t // tm, 1, k * tm)


def _dispatch_kernel(pad_row_ref, pad_n_ref, dest_ref, h1p_ref, w1_ref, w2_ref,
                     xs_hbm, w1c_ref, w2c_ref, stage_ref, zero_ref, sem, *, tm):
    i = pl.program_id(0)

    def tile(ref, t):
        return ref.at[pl.ds(pl.multiple_of(t * PK_ROWS, PK_ROWS), PK_ROWS), :]

    slot = i % 2
    rows = tm * PK_ROWS
    stage = stage_ref.at[pl.ds(pl.multiple_of(slot * rows, rows), rows), :]
    stage[...] = h1p_ref[...]

    def body(it, c):
        for u in range(DMA_UNROLL):
            r = it * DMA_UNROLL + u
            for k in range(TOP_K):
                pltpu.make_async_copy(tile(stage, r), tile(xs_hbm, dest_ref[0, 0, k * tm + r]),
                                      sem.at[slot]).start(priority=k % 2)
        return c
    lax.fori_loop(0, tm // DMA_UNROLL, body, 0)
    w1c_ref[...] = _mx(w1_ref[...])
    w2c_ref[...] = _mx(w2_ref[...])

    def drain(s):
        for _ in range(TOP_K):
            pltpu.make_async_copy(h1p_ref, xs_hbm.at[pl.ds(0, rows), :], sem.at[s]).wait()

    @pl.when(i > 0)
    def _():
        drain(1 - slot)

    @pl.when(i == pl.num_programs(0) - 1)
    def _():
        drain(slot)
        zero_ref[...] = jnp.zeros_like(zero_ref)
        zero = zero_ref.at[pl.ds(0, PK_ROWS), :]

        def per_expert(e, c):
            def start(r, c2):
                pltpu.make_async_copy(zero, tile(xs_hbm, pad_row_ref[e] + r), sem.at[2]).start()
                return c2
            lax.fori_loop(0, pad_n_ref[e], start, 0)

            def wait(r, c2):
                pltpu.make_async_copy(zero, tile(xs_hbm, 0), sem.at[2]).wait()
                return c2
            lax.fori_loop(0, pad_n_ref[e], wait, 0)
            return c
        lax.fori_loop(0, pad_row_ref.shape[0], per_expert, 0)


def _dispatch_call(pad_row, pad_n, dest, h1p, w1, w2, n_slots, n_steps):
    t = h1p.shape[0] // PK_ROWS
    tm = t // n_steps
    r1, r2 = w1.shape[0] // n_steps, w2.shape[0] // n_steps
    step = lambda i, pr, pn: (i, 0)
    grid_spec = pltpu.PrefetchScalarGridSpec(
        num_scalar_prefetch=2,
        grid=(n_steps,),
        in_specs=[
            pl.BlockSpec((1, 1, TOP_K * tm), lambda i, pr, pn: (i, 0, 0), memory_space=pltpu.SMEM),
            pl.BlockSpec((tm * PK_ROWS, LANES), step),
            pl.BlockSpec((r1, w1.shape[1]), step),
            pl.BlockSpec((r2, w2.shape[1]), step),
        ],
        out_specs=[
            pl.BlockSpec(memory_space=pl.ANY),
            pl.BlockSpec((r1, w1.shape[1]), step),
            pl.BlockSpec((r2, w2.shape[1]), step),
        ],
        scratch_shapes=[pltpu.VMEM((2 * tm * PK_ROWS, LANES), jnp.uint32),
                        pltpu.VMEM((F32_ROWS, LANES), jnp.uint32), pltpu.SemaphoreType.DMA((3,))],
    )
    return pl.pallas_call(
        functools.partial(_dispatch_kernel, tm=tm),
        grid_spec=grid_spec,
        out_shape=[
            jax.ShapeDtypeStruct((n_slots * PK_ROWS, LANES), jnp.uint32),
            jax.ShapeDtypeStruct(w1.shape, MXU_DTYPE),
            jax.ShapeDtypeStruct(w2.shape, MXU_DTYPE),
        ],
        compiler_params=pltpu.CompilerParams(
            dimension_semantics=("arbitrary",), vmem_limit_bytes=VMEM_LIMIT),
        name="dispatch",
    )(pad_row, pad_n, _step_tables(dest, tm), h1p, w1, w2)


def _expert_kernel(be_ref, nr_ref, x_ref, *refs):
    y_ref = refs[-1]
    rows = MOE_BLOCK * PK_ROWS
    g0 = pl.program_id(0) * BLOCKS_PER_STEP
    fused = (be_ref[g0] == be_ref[g0 + 1]) & (g0 + 1 < nr_ref[0])

    def ffn(x_blk, y_blk, n, w1_ref, b1_ref, w2_ref, b2_ref):
        x = _mx(_unpack_pairs(_load_token_tiles(x_blk, n, PK_ROWS)))
        hh = _dot(x, w1_ref[0]) + b1_ref[0]
        x_glu = jnp.minimum(hh[:, :D_FF], SWIGLU_LIMIT)
        x_lin = jnp.clip(hh[:, D_FF:], -SWIGLU_LIMIT, SWIGLU_LIMIT)
        act = x_glu * (1.0 / (1.0 + jnp.exp(-SWIGLU_ALPHA * x_glu))) * (x_lin + 1.0)
        _store_token_tiles(y_blk, _pack_pairs(_dot(_mx(act), w2_ref[0]) + b2_ref[0]))

    @pl.when(fused)
    def _():
        ffn(x_ref, y_ref, BLOCKS_PER_STEP * MOE_BLOCK, *refs[0:4])

    @pl.when(jnp.logical_not(fused))
    def _():
        for half in range(BLOCKS_PER_STEP):
            weights = refs[4 * half:4 * half + 4]
            g = g0 + half
            x_blk = x_ref.at[pl.ds(half * rows, rows), :]
            y_blk = y_ref.at[pl.ds(half * rows, rows), :]

            @pl.when(g >= nr_ref[0])
            def _():
                y_blk[...] = jnp.zeros_like(y_blk)

            @pl.when(g < nr_ref[0])
            def _():
                ffn(x_blk, y_blk, MOE_BLOCK, *weights)


def _expert_call(block_expert, n_real, xs, w1, b1, w2, b2):
    n_blocks = block_expert.shape[0]
    assert BLOCKS_PER_STEP == 2 and n_blocks % BLOCKS_PER_STEP == 0
    step = lambda i, be, nr: (i, 0)
    rows = BLOCKS_PER_STEP * MOE_BLOCK * PK_ROWS
    expert_specs, expert_args = [], []
    for half in range(BLOCKS_PER_STEP):
        ex = functools.partial(lambda i, be, nr, half: (be[i * BLOCKS_PER_STEP + half], 0, 0), half=half)
        expert_specs += [pl.BlockSpec((1, D_MODEL, 2 * D_FF), ex), pl.BlockSpec((1, 1, 2 * D_FF), ex),
                         pl.BlockSpec((1, D_FF, D_MODEL), ex), pl.BlockSpec((1, 1, D_MODEL), ex)]
        expert_args += [w1, b1, w2, b2]
    grid_spec = pltpu.PrefetchScalarGridSpec(
        num_scalar_prefetch=2,
        grid=(n_blocks // BLOCKS_PER_STEP,),
        in_specs=[pl.BlockSpec((rows, LANES), step)] + expert_specs,
        out_specs=pl.BlockSpec((rows, LANES), step),
    )
    return pl.pallas_call(
        _expert_kernel,
        grid_spec=grid_spec,
        out_shape=jax.ShapeDtypeStruct((n_blocks * MOE_BLOCK * PK_ROWS, LANES), jnp.uint32),
        compiler_params=pltpu.CompilerParams(
            dimension_semantics=("arbitrary",), vmem_limit_bytes=VMEM_LIMIT),
        name="moe_experts",
    )(block_expert, n_real, xs, *expert_args)


def _gather_combine_kernel(dest0_ref, destn_ref, gate_ref, h1_ref, g_ref, b_ref, ys_hbm,
                           out_ref, ybuf, sem, *, tm):
    i = pl.program_id(0)
    slot = i % 2
    rows = tm * PK_ROWS

    def tile(ref, t):
        return ref.at[pl.ds(pl.multiple_of(t * PK_ROWS, PK_ROWS), PK_ROWS), :]

    def gather(tbl, s):
        def body(it, c):
            for u in range(DMA_UNROLL):
                r = it * DMA_UNROLL + u
                for k in range(TOP_K):
                    pltpu.make_async_copy(tile(ys_hbm, tbl[0, 0, k * tm + r]),
                                          tile(ybuf, (s * TOP_K + k) * tm + r), sem.at[s]).start(priority=k % 2)
            return c
        lax.fori_loop(0, tm // DMA_UNROLL, body, 0)

    @pl.when(i == 0)
    def _():
        gather(dest0_ref, 0)

    @pl.when(i + 1 < pl.num_programs(0))
    def _():
        gather(destn_ref, 1 - slot)

    for k in range(TOP_K):
        plane = ybuf.at[pl.ds(pl.multiple_of((slot * TOP_K + k) * rows, rows), rows), :]
        pltpu.make_async_copy(ys_hbm.at[pl.ds(0, rows), :], plane, sem.at[slot]).wait()

    gates = gate_ref[...]
    acc = DEEPNORM_ALPHA * h1_ref[...]
    for k in range(TOP_K):
        plane = ybuf.at[pl.ds(pl.multiple_of((slot * TOP_K + k) * rows, rows), rows), :]
        acc = acc + gates[:, k:k + 1] * _unpack_pairs(_load_token_tiles(plane, tm, PK_ROWS))
    out_ref[...] = _layer_norm(acc, g_ref[...], b_ref[...])


def _gather_combine_call(dest, gates, h1, g, b, ys, tm):
    t = h1.shape[0]
    nt = t // tm
    const = lambda i: (0, 0)
    smem = functools.partial(pl.BlockSpec, memory_space=pltpu.SMEM)
    dest = _step_tables(dest, tm)
    dest_next = jnp.concatenate([dest[1:], dest[:1]], axis=0)
    return pl.pallas_call(
        functools.partial(_gather_combine_kernel, tm=tm),
        grid=(nt,),
        in_specs=[
            smem((1, 1, TOP_K * tm), lambda i: (0, 0, 0)),
            smem((1, 1, TOP_K * tm), lambda i: (i, 0, 0)),
            pl.BlockSpec((tm, LANES), lambda i: (i, 0)),
            pl.BlockSpec((tm, D_MODEL), lambda i: (i, 0)),
            pl.BlockSpec((1, D_MODEL), const),
            pl.BlockSpec((1, D_MODEL), const),
            pl.BlockSpec(memory_space=pl.ANY),
        ],
        out_specs=pl.BlockSpec((tm, D_MODEL), lambda i: (i, 0)),
        out_shape=jax.ShapeDtypeStruct((t, D_MODEL), F32),
        scratch_shapes=[pltpu.VMEM((2 * TOP_K * tm * PK_ROWS, LANES), jnp.uint32),
                        pltpu.SemaphoreType.DMA((2,))],
        compiler_params=pltpu.CompilerParams(
            dimension_semantics=("arbitrary",), vmem_limit_bytes=VMEM_LIMIT),
        name="gather_combine_ln2",
    )(dest, dest_next, gates, h1, g, b, ys)


def _relayout_w_in(w):
    sizes = (256, 256, 512, 512, GATE_RANK, 256, 256, 512, 512)
    offs = np.concatenate([[0], np.cumsum(sizes)])
    gq, gk, gv, gg, glr, rq, rk, rv, rg = [w[:, offs[i]:offs[i + 1]] for i in range(9)]
    perm = np.zeros((256,), np.int32)
    for p in range(2):
        for l in range(LANES):
            part, hh, f = l // 64, (l % 64) // 32, l % 32
            perm[p * LANES + l] = (2 * p + hh) * DK + part * (DK // 2) + f
    glr = jnp.pad(glr, ((0, 0), (0, LANES - GATE_RANK)))
    return jnp.concatenate([gq, gk, gv, gg, glr, rq[:, perm], rk[:, perm], rv, rg], axis=1)


def kernel(x, positions, ln_in_g, ln_in_b, w_in, gla_gate_w, gla_gate_b, gla_norm_g, ret_norm_g,
           ret_norm_b, w_out, ln1_g, ln1_b, router_w, router_b, moe_w1, moe_b1, moe_w2, moe_b2,
           ln2_g, ln2_b):
    bsz, seq, d = x.shape
    assert d == D_MODEL and seq % GROUP == 0 and w_in.shape[0] == DEPTH == 1
    t = bsz * seq
    ts = SEQ_TILE if seq % SEQ_TILE == 0 else GROUP
    tm = min(ROW_TILE, t)
    assert seq % ts == 0 and t % tm == 0
    row = lambda v: v.reshape(1, -1).astype(F32)

    w_p = _relayout_w_in(w_in[0].astype(MXU_DTYPE))
    h0, proj = _ln_proj_call(x.reshape(t, d), row(ln_in_g), row(ln_in_b), w_p, tm)

    gw = jnp.pad(gla_gate_w[0], ((0, LANES - GATE_RANK), (0, 0))).astype(MXU_DTYPE)
    half = DK // 2
    inv_freq = 1.0 / (ROPE_BASE ** np.linspace(0.0, 1.0, half, dtype=np.float32))
    invf = jnp.asarray(np.tile(inv_freq, LANES // half).reshape(1, LANES), F32)
    sgn = jnp.asarray(np.where(np.arange(LANES) < LANES // 2, -1.0, 1.0).reshape(1, LANES), F32)
    step_ang = np.arange(GROUP, dtype=np.float64)[:, None] * np.tile(inv_freq, LANES // half)[None, :]
    cd = jnp.asarray(np.cos(step_ang), F32)
    sd = jnp.asarray(np.sin(step_ang), F32)
    o = _mixer_call(proj, positions.reshape(t, 1), gw, row(gla_gate_b[0]), row(gla_norm_g[0]),
                    row(ret_norm_g[0]), row(ret_norm_b[0]), invf, sgn, cd, sd, bsz, seq, ts)

    rwt = jnp.pad(router_w[0].T, ((0, LANES - N_EXPERTS), (0, 0))).astype(MXU_DTYPE)
    rbt = jnp.pad(router_b[0], (0, LANES - N_EXPERTS)).reshape(LANES, 1).astype(F32)
    h1, h1t, r_i, r_g, cnt = _post_mix_call(o, h0, w_out[0].astype(MXU_DTYPE), row(ln1_g[0]),
                                       row(ln1_b[0]), rwt, rbt, tm)

    e = r_i[:TOP_K].reshape(-1)
    rank = r_i[TOP_K:].reshape(-1)
    counts = cnt[:, 0].astype(jnp.int32)
    padded = ((counts + MOE_BLOCK - 1) // MOE_BLOCK) * MOE_BLOCK
    pad_end = jnp.cumsum(padded)
    pad_start = pad_end - padded
    tk = t * TOP_K
    n_blocks = (tk + MOE_BLOCK - 1) // MOE_BLOCK + N_EXPERTS
    i32 = jnp.int32
    dest = rank
    for j in range(N_EXPERTS):
        dest = dest + jnp.where(e == j, pad_start[j], 0)
    dest = dest.reshape(TOP_K, t)
    block_start = jnp.arange(n_blocks, dtype=i32) * MOE_BLOCK
    block_expert = jnp.minimum(jnp.sum(block_start[:, None] >= pad_end[None, :], -1),
                               N_EXPERTS - 1).astype(i32)
    n_real = (pad_end[-1:] // MOE_BLOCK).astype(i32)

    n_slots = n_blocks * MOE_BLOCK
    hole_row = jnp.concatenate([pad_start + counts, pad_end[-1:]]).astype(i32)
    hole_n = jnp.concatenate([padded - counts, n_slots - pad_end[-1:]]).astype(i32)
    xs, w1c, w2c = _dispatch_call(hole_row, hole_n, dest, h1t,
                                  moe_w1[0].reshape(N_EXPERTS * D_MODEL, 2 * D_FF),
                                  moe_w2[0].reshape(N_EXPERTS * D_FF, D_MODEL), n_slots, DISPATCH_STEPS)
    ys = _expert_call(block_expert, n_real, xs,
                      w1c.reshape(N_EXPERTS, D_MODEL, 2 * D_FF), moe_b1[0].reshape(N_EXPERTS, 1, 2 * D_FF),
                      w2c.reshape(N_EXPERTS, D_FF, D_MODEL), moe_b2[0].reshape(N_EXPERTS, 1, D_MODEL))

    out = _gather_combine_call(dest, r_g, h1, row(ln2_g[0]), row(ln2_b[0]), ys, min(COMBINE_TILE, t))
    return out.reshape(bsz, seq, d)
```

```python
import functools
import math

import numpy as np
import jax
import jax.numpy as jnp
from jax import lax
from jax.experimental import pallas as pl
from jax.experimental.pallas import tpu as pltpu

F32 = jnp.float32
MXU_DTYPE = jnp.bfloat16

D_MODEL = 1024
CHUNK = 64
GC = 4
GROUP = GC * CHUNK
HEADS = 4
DK = 64
DV = 128
GATE_RANK = 16
GATE_NORM = 16.0
ROPE_BASE = 10000.0
N_EXPERTS = 32
TOP_K = 4
D_FF = 1024
SWIGLU_ALPHA = 1.702
SWIGLU_LIMIT = 7.0
MOE_BLOCK = 256
LN_PROJ_SLABS = 4
BLOCKS_PER_STEP = 2
DMA_UNROLL = 8
DISPATCH_STEPS = 64
LN_EPS = 1e-5
DEPTH = 1
DEEPNORM_ALPHA = (2.0 * DEPTH) ** 0.25

LANES = 128
VMEM_LIMIT = 56 * 1024 * 1024

C_GQ, C_GK, C_GV, C_GG, C_GLR = 0, 256, 512, 1024, 1536
C_RQ, C_RK, C_RV, C_RG = 1664, 1920, 2176, 2688
PROJ_W = 3200

LOG_GAMMA = [math.log1p(-(2.0 ** (-5.0 - h))) for h in range(HEADS)]


def _dot(a, b):
    return jnp.dot(a, b, preferred_element_type=F32)


def _dot_nt(a, b):
    return lax.dot_general(a, b, (((1,), (1,)), ((), ())), preferred_element_type=F32)


def _dot_tn(a, b):
    return lax.dot_general(a, b, (((0,), (0,)), ((), ())), preferred_element_type=F32)


def _mx(a):
    return a.astype(MXU_DTYPE)


def _layer_norm(x, g, b):
    mu = jnp.mean(x, -1, keepdims=True)
    xc = x - mu
    var = jnp.mean(xc * xc, -1, keepdims=True)
    return xc * lax.rsqrt(var + LN_EPS) * g + b


def _silu(x):
    return x * (1.0 / (1.0 + jnp.exp(-x)))


F32_ROWS = D_MODEL // LANES
PK_ROWS = D_MODEL // (2 * LANES)


def _load_token_tiles(ref, n, r):
    return jnp.concatenate([ref[pl.ds(j, n, stride=r), :] for j in range(r)], axis=1)


def _store_token_tiles(ref, rows):
    n, r = rows.shape[0], rows.shape[1] // LANES
    for j in range(r):
        ref[pl.ds(j, n, stride=r), :] = rows[:, j * LANES:(j + 1) * LANES]


def _pack_pairs(x):
    h = x.shape[1] // 2
    r = x.astype(jnp.bfloat16).astype(F32)
    lo = lax.bitcast_convert_type(r[:, :h], jnp.uint32) >> 16
    hi = lax.bitcast_convert_type(r[:, h:], jnp.uint32) & jnp.uint32(0xFFFF0000)
    return lo | hi


def _unpack_pairs(u):
    lo = lax.bitcast_convert_type(u << 16, F32)
    hi = lax.bitcast_convert_type(u & jnp.uint32(0xFFFF0000), F32)
    return jnp.concatenate([lo, hi], axis=1)


def _ln_proj_kernel(x_ref, g_ref, b_ref, w_ref, p_ref):
    n = x_ref.shape[0] // LN_PROJ_SLABS
    for j in range(LN_PROJ_SLABS):
        rs = slice(j * n, (j + 1) * n)
        h = _layer_norm(x_ref[rs, :], g_ref[...], b_ref[...])
        p_ref[rs, :] = _dot(_mx(h), w_ref[...])


def _ln_proj_call(x2, g, b, w, tm):
    t = x2.shape[0]
    return pl.pallas_call(
        _ln_proj_kernel,
        grid=(t // tm,),
        in_specs=[
            pl.BlockSpec((tm, D_MODEL), lambda i: (i, 0)),
            pl.BlockSpec((1, D_MODEL), lambda i: (0, 0)),
            pl.BlockSpec((1, D_MODEL), lambda i: (0, 0)),
            pl.BlockSpec((D_MODEL, PROJ_W), lambda i: (0, 0)),
        ],
        out_specs=pl.BlockSpec((tm, PROJ_W), lambda i: (i, 0)),
        out_shape=jax.ShapeDtypeStruct((t, PROJ_W), F32),
        compiler_params=pltpu.CompilerParams(
            dimension_semantics=("arbitrary",), vmem_limit_bytes=VMEM_LIMIT),
        name="ln_proj",
    )(x2, g, b, w)


def _mixer_kernel(p_ref, pos_ref, gw_ref, gb_ref, gn_ref, rng_ref, rnb_ref, invf_ref, sgn_ref,
                  cd_ref, sd_ref, o_ref, st_ref, *, n_groups):
    @pl.when(pl.program_id(1) == 0)
    def _():
        st_ref[...] = jnp.zeros_like(st_ref)

    rr = lax.broadcasted_iota(jnp.int32, (GROUP, GROUP), 0)
    cc = lax.broadcasted_iota(jnp.int32, (GROUP, GROUP), 1)
    same = (rr >> 6) == (cc >> 6)
    lower = same & (rr >= cc)
    upper = same & (rr < cc)
    tri = lower.astype(MXU_DTYPE)
    dist = jnp.abs(rr - cc).astype(F32)
    rowi = lax.broadcasted_iota(jnp.int32, (GROUP, LANES), 0)
    rin = (rowi & (CHUNK - 1)).astype(F32)
    in_chunk = [(rowi >> 6) == c for c in range(GC)]
    lane = lax.broadcasted_iota(jnp.int32, (GROUP, LANES), 1)
    gla_half = [((lane >> 6) & 1) == i for i in range(2)]
    ret_half = [((lane >> 5) & 1) == i for i in range(2)]
    ret_d = [jnp.where(same, jnp.exp(LOG_GAMMA[h] * dist), 0.0) for h in range(HEADS)]
    ret_eb = [jnp.exp(LOG_GAMMA[h] * (rin + 1.0)) for h in range(HEADS)]
    ret_ek = [jnp.exp(LOG_GAMMA[h] * (CHUNK - 1.0 - rin)) for h in range(HEADS)]
    ret_dec = [math.exp(LOG_GAMMA[h] * CHUNK) for h in range(HEADS)]

    def block_diag(x):
        return _mx(jnp.concatenate([jnp.where(in_chunk[c], x, 0.0) for c in range(GC)], axis=1))

    lower2 = jnp.concatenate([lower, lower], axis=0)
    upper2 = jnp.concatenate([upper, upper], axis=0)
    ret_d2 = [jnp.concatenate([ret_d[2 * p], ret_d[2 * p + 1]], axis=0) for p in range(2)]
    zeros_v = jnp.zeros((GROUP, DV), MXU_DTYPE)

    def pair_products(hs0, q_m, sc, v2, v2_kv, k_bd, decay):
        sc_b = _mx(sc)
        v_bd = jnp.concatenate([jnp.concatenate([v2[:, :DV], zeros_v], axis=1),
                                jnp.concatenate([zeros_v, v2[:, DV:]], axis=1)], axis=0)
        o_intra = _dot(jnp.concatenate([sc_b[:GROUP], sc_b[GROUP:]], axis=1), v_bd)
        kv_t = _dot_tn(v2_kv, k_bd)
        st = [st_ref[hs0], st_ref[hs0 + 1]]
        parts = []
        for c in range(GC):
            cr = slice(c * CHUNK, (c + 1) * CHUNK)
            qc = jnp.concatenate([q_m[0][cr], q_m[1][cr]], axis=0)
            r = _dot_nt(qc, _mx(jnp.concatenate(st, axis=0)))
            parts.append(jnp.concatenate([r[:CHUNK, :DV], r[CHUNK:, DV:]], axis=1))
            for i in range(2):
                st[i] = st[i] * decay(c, i) + kv_t[i * DV:(i + 1) * DV, c * LANES:(c + 1) * LANES]
        st_ref[hs0] = st[0]
        st_ref[hs0 + 1] = st[1]
        return o_intra, jnp.concatenate(parts, axis=0)

    def group_body(g, carry):
        rows = pl.ds(pl.multiple_of(g * GROUP, GROUP), GROUP)

        z = _dot(_mx(p_ref[rows, C_GLR:C_GLR + LANES]), gw_ref[...]) + gb_ref[...]
        la = (jnp.minimum(z, 0.0) - jnp.log1p(jnp.exp(-jnp.abs(z)))) * (1.0 / GATE_NORM)
        la_hi = _mx(la)
        r1 = la - la_hi.astype(F32)
        la_mid = _mx(r1)
        la_lo = _mx(r1 - la_mid.astype(F32))
        b3 = _dot(tri, jnp.concatenate([la_hi, la_mid, la_lo], axis=1))
        b = b3[:, :256] + b3[:, 256:512] + b3[:, 512:]
        b_ends = [b[(c + 1) * CHUNK - 1:(c + 1) * CHUNK, :] for c in range(GC)]
        b_last = jnp.concatenate([jnp.broadcast_to(e, (CHUNK, 256)) for e in b_ends], axis=0)
        decs = [jnp.exp(e) for e in b_ends]
        eb = jnp.exp(b)
        enb = jnp.exp(-b)
        ekv = jnp.exp(b_last - b)
        q = p_ref[rows, C_GQ:C_GQ + 256] * (DK ** -0.5)
        k = p_ref[rows, C_GK:C_GK + 256]
        qe, qn = q * eb, q * enb
        ke, kn, kk = k * eb, k * enb, k * ekv
        for p in range(2):
            ls = slice(p * LANES, (p + 1) * LANES)
            qe_m = [_mx(jnp.where(gla_half[i], qe[:, ls], 0.0)) for i in range(2)]
            qn_m = [_mx(jnp.where(gla_half[i], qn[:, ls], 0.0)) for i in range(2)]
            s_lo = _dot_nt(jnp.concatenate(qe_m, axis=0), _mx(kn[:, ls]))
            s_up = _dot_nt(jnp.concatenate(qn_m, axis=0), _mx(ke[:, ls]))
            sc = jnp.where(lower2, s_lo, jnp.where(upper2, s_up, 0.0))
            v2 = _mx(p_ref[rows, C_GV + 2 * p * DV:C_GV + (2 * p + 2) * DV])
            o_intra, o_inter = pair_products(2 * p, qe_m, sc, v2, v2, block_diag(kk[:, ls]),
                                             lambda c, i: decs[c][:, ls])
            o2 = o_intra + o_inter
            for i in range(2):
                h = 2 * p + i
                o = o2[:, i * DV:(i + 1) * DV]
                o = o * lax.rsqrt(jnp.mean(o * o, -1, keepdims=True) + LN_EPS) * gn_ref[...]
                o = o * _silu(p_ref[rows, C_GG + h * DV:C_GG + (h + 1) * DV])
                o_ref[rows, h * DV:(h + 1) * DV] = o.astype(o_ref.dtype)

        base = pos_ref[pl.ds(pl.multiple_of(g * GROUP, GROUP), 1), :].astype(F32) * invf_ref[...]
        c0, s0 = jnp.cos(base), jnp.sin(base)
        cs = c0 * cd_ref[...] - s0 * sd_ref[...]
        sn = (s0 * sgn_ref[...]) * cd_ref[...] + (c0 * sgn_ref[...]) * sd_ref[...]
        for p in range(2):
            tq = p_ref[rows, C_RQ + p * LANES:C_RQ + (p + 1) * LANES]
            tk = p_ref[rows, C_RK + p * LANES:C_RK + (p + 1) * LANES] * (DK ** -0.5)
            rq = tq * cs + pltpu.roll(tq, LANES // 2, 1) * sn
            rk = tk * cs + pltpu.roll(tk, LANES // 2, 1) * sn
            q_m = [_mx(jnp.where(ret_half[i], rq, 0.0)) for i in range(2)]
            s = _dot_nt(jnp.concatenate(q_m, axis=0), _mx(rk)) * ret_d2[p]
            v2f = p_ref[rows, C_RV + 2 * p * DV:C_RV + (2 * p + 2) * DV]
            v2_kv = jnp.concatenate([v2f[:, :DV] * ret_ek[2 * p], v2f[:, DV:] * ret_ek[2 * p + 1]], axis=1)
            o_intra, o_inter = pair_products(HEADS + 2 * p, q_m, s, _mx(v2f), _mx(v2_kv), block_diag(rk),
                                             lambda c, i: ret_dec[2 * p + i])
            for i in range(2):
                h = 2 * p + i
                o = o_intra[:, i * DV:(i + 1) * DV] + ret_eb[h] * o_inter[:, i * DV:(i + 1) * DV]
                mu = jnp.mean(o, -1, keepdims=True)
                oc = o - mu
                var = jnp.mean(oc * oc, -1, keepdims=True)
                o = oc * lax.rsqrt(var + LN_EPS) * rng_ref[:, h * DV:(h + 1) * DV] \
                    + rnb_ref[:, h * DV:(h + 1) * DV]
                o = o * _silu(p_ref[rows, C_RG + h * DV:C_RG + (h + 1) * DV])
                o_ref[rows, (HEADS + h) * DV:(HEADS + h + 1) * DV] = o.astype(o_ref.dtype)
        return carry

    lax.fori_loop(0, n_groups, group_body, 0)


def _mixer_call(proj, pos, gw, gb, gn, rng, rnb, invf, sgn, cd, sd, bsz, seq, ts):
    t = bsz * seq
    nst = seq // ts
    const = lambda b, s: (0, 0)
    return pl.pallas_call(
        functools.partial(_mixer_kernel, n_groups=ts // GROUP),
        grid=(bsz, nst),
        in_specs=[
            pl.BlockSpec((ts, PROJ_W), lambda b, s: (b * nst + s, 0)),
            pl.BlockSpec((ts, 1), lambda b, s: (b * nst + s, 0)),
            pl.BlockSpec((LANES, 256), const),
            pl.BlockSpec((1, 256), const),
            pl.BlockSpec((1, DV), const),
            pl.BlockSpec((1, HEADS * DV), const),
            pl.BlockSpec((1, HEADS * DV), const),
            pl.BlockSpec((1, LANES), const),
            pl.BlockSpec((1, LANES), const),
            pl.BlockSpec((GROUP, LANES), const),
            pl.BlockSpec((GROUP, LANES), const),
        ],
        out_specs=pl.BlockSpec((ts, D_MODEL), lambda b, s: (b * nst + s, 0)),
        out_shape=jax.ShapeDtypeStruct((t, D_MODEL), MXU_DTYPE),
        scratch_shapes=[pltpu.VMEM((2 * HEADS, DV, LANES), F32)],
        compiler_params=pltpu.CompilerParams(
            dimension_semantics=("arbitrary", "arbitrary"), vmem_limit_bytes=VMEM_LIMIT),
        name="mixers",
    )(proj, pos, gw, gb, gn, rng, rnb, invf, sgn, cd, sd)


def _post_mix_kernel(o_ref, x_ref, g0_ref, b0_ref, wo_ref, g_ref, b_ref, rwt_ref, rbt_ref,
                     h1_ref, h1t_ref, ri_ref, rg_ref, cnt_ref, triu_ref, carry_ref, *, tm):
    @pl.when(pl.program_id(0) == 0)
    def _():
        r = lax.broadcasted_iota(jnp.int32, (tm, tm), 0)
        c = lax.broadcasted_iota(jnp.int32, (tm, tm), 1)
        triu_ref[...] = (r < c).astype(MXU_DTYPE)
        carry_ref[...] = jnp.zeros_like(carry_ref)

    mix = _dot(o_ref[...], wo_ref[...])
    h0 = _layer_norm(x_ref[...], g0_ref[...], b0_ref[...])
    h1 = _layer_norm(DEEPNORM_ALPHA * h0 + mix, g_ref[...], b_ref[...])
    h1_ref[...] = h1
    _store_token_tiles(h1t_ref, _pack_pairs(h1))

    l = (_dot_nt(rwt_ref[...], _mx(h1)) + rbt_ref[...])[:N_EXPERTS]
    erow = lax.broadcasted_iota(jnp.int32, (N_EXPERTS, tm), 0).astype(F32)
    vals, idxs = [], []
    for _ in range(TOP_K):
        m = jnp.max(l, 0, keepdims=True)
        i = jnp.min(jnp.where(l == m, erow, float(LANES)), 0, keepdims=True)
        vals.append(m)
        idxs.append(i)
        l = jnp.where(erow == i, -jnp.inf, l)
    exps = [jnp.exp(v - vals[0]) for v in vals]
    inv = 1.0 / (exps[0] + exps[1] + exps[2] + exps[3])

    onehot = jnp.zeros((N_EXPERTS, tm), F32)
    for i in idxs:
        onehot = onehot + (erow == i).astype(F32)
    before = _dot(_mx(onehot), triu_ref[...]) + carry_ref[:, :1]
    ranks = [jnp.sum(jnp.where(erow == i, before, 0.0), 0, keepdims=True) for i in idxs]
    ri_ref[...] = jnp.concatenate(idxs + ranks, axis=0).astype(jnp.int32)
    gates_t = jnp.concatenate([e * inv for e in exps] + [jnp.zeros((LANES - TOP_K, tm), F32)], axis=0)
    rg_ref[...] = jnp.transpose(gates_t)
    carry_ref[...] = carry_ref[...] + jnp.sum(onehot, 1, keepdims=True)
    cnt_ref[...] = carry_ref[...]


def _post_mix_call(o, x2, g0, b0, wo, g, b, rwt, rbt, tm):
    t = o.shape[0]
    const = lambda i: (0, 0)
    return pl.pallas_call(
        functools.partial(_post_mix_kernel, tm=tm),
        grid=(t // tm,),
        in_specs=[
            pl.BlockSpec((tm, D_MODEL), lambda i: (i, 0)),
            pl.BlockSpec((tm, D_MODEL), lambda i: (i, 0)),
            pl.BlockSpec((1, D_MODEL), const),
            pl.BlockSpec((1, D_MODEL), const),
            pl.BlockSpec((D_MODEL, D_MODEL), const),
            pl.BlockSpec((1, D_MODEL), const),
            pl.BlockSpec((1, D_MODEL), const),
            pl.BlockSpec((LANES, D_MODEL), const),
            pl.BlockSpec((LANES, 1), const),
        ],
        out_specs=[
            pl.BlockSpec((tm, D_MODEL), lambda i: (i, 0)),
            pl.BlockSpec((tm * PK_ROWS, LANES), lambda i: (i, 0)),
            pl.BlockSpec((2 * TOP_K, tm), lambda i: (0, i)),
            pl.BlockSpec((tm, LANES), lambda i: (i, 0)),
            pl.BlockSpec((N_EXPERTS, LANES), const),
        ],
        out_shape=[
            jax.ShapeDtypeStruct((t, D_MODEL), F32),
            jax.ShapeDtypeStruct((t * PK_ROWS, LANES), jnp.uint32),
            jax.ShapeDtypeStruct((2 * TOP_K, t), jnp.int32),
            jax.ShapeDtypeStruct((t, LANES), F32),
            jax.ShapeDtypeStruct((N_EXPERTS, LANES), F32),
        ],
        scratch_shapes=[pltpu.VMEM((tm, tm), MXU_DTYPE), pltpu.VMEM((N_EXPERTS, LANES), F32)],
        compiler_params=pltpu.CompilerParams(
            dimension_semantics=("arbitrary",), vmem_limit_bytes=VMEM_LIMIT),
        name="post_mix_router",
    )(o, x2, g0, b0, wo, g, b, rwt, rbt)


def _step_tables(dest, tm):
    k, t = dest.shape
    return dest.reshape(k, t // tm, tm).transpose(1, 0, 2).reshape(t // tm, 1, k * tm)


def _dispatch_kernel(pad_row_ref, pad_n_ref, dest_ref, h1p_ref, w1_ref, w2_ref,
                     xs_hbm, w1c_ref, w2c_ref, stage_ref, zero_ref, sem, *, tm):
    i = pl.program_id(0)

    def tile(ref, t):
        return ref.at[pl.ds(pl.multiple_of(t * PK_ROWS, PK_ROWS), PK_ROWS), :]

    slot = i % 2
    rows = tm * PK_ROWS
    stage = stage_ref.at[pl.ds(pl.multiple_of(slot * rows, rows), rows), :]
    stage[...] = h1p_ref[...]

    def body(it, c):
        for u in range(DMA_UNROLL):
            r = it * DMA_UNROLL + u
            for k in range(TOP_K):
                pltpu.make_async_copy(tile(stage, r), tile(xs_hbm, dest_ref[0, 0, k * tm + r]),
                                      sem.at[slot]).start(priority=k % 2)
        return c
    lax.fori_loop(0, tm // DMA_UNROLL, body, 0)
    w1c_ref[...] = _mx(w1_ref[...])
    w2c_ref[...] = _mx(w2_ref[...])

    def drain(s):
        for _ in range(TOP_K):
            pltpu.make_async_copy(h1p_ref, xs_hbm.at[pl.ds(0, rows), :], sem.at[s]).wait()

    @pl.when(i > 0)
    def _():
        drain(1 - slot)

    @pl.when(i == pl.num_programs(0) - 1)
    def _():
        drain(slot)
        zero_ref[...] = jnp.zeros_like(zero_ref)
        zero = zero_ref.at[pl.ds(0, PK_ROWS), :]

        def per_expert(e, c):
            def start(r, c2):
                pltpu.make_async_copy(zero, tile(xs_hbm, pad_row_ref[e] + r), sem.at[2]).start()
                return c2
            lax.fori_loop(0, pad_n_ref[e], start, 0)

            def wait(r, c2):
                pltpu.make_async_copy(zero, tile(xs_hbm, 0), sem.at[2]).wait()
                return c2
            lax.fori_loop(0, pad_n_ref[e], wait, 0)
            return c
        lax.fori_loop(0, pad_row_ref.shape[0], per_expert, 0)


def _dispatch_call(pad_row, pad_n, dest, h1p, w1, w2, n_slots, n_steps):
    t = h1p.shape[0] // PK_ROWS
    tm = t // n_steps
    r1, r2 = w1.shape[0] // n_steps, w2.shape[0] // n_steps
    step = lambda i, pr, pn: (i, 0)
    grid_spec = pltpu.PrefetchScalarGridSpec(
        num_scalar_prefetch=2,
        grid=(n_steps,),
        in_specs=[
            pl.BlockSpec((1, 1, TOP_K * tm), lambda i, pr, pn: (i, 0, 0), memory_space=pltpu.SMEM),
            pl.BlockSpec((tm * PK_ROWS, LANES), step),
            pl.BlockSpec((r1, w1.shape[1]), step),
            pl.BlockSpec((r2, w2.shape[1]), step),
        ],
        out_specs=[
            pl.BlockSpec(memory_space=pl.ANY),
            pl.BlockSpec((r1, w1.shape[1]), step),
            pl.BlockSpec((r2, w2.shape[1]), step),
        ],
        scratch_shapes=[pltpu.VMEM((2 * tm * PK_ROWS, LANES), jnp.uint32),
                        pltpu.VMEM((F32_ROWS, LANES), jnp.uint32), pltpu.SemaphoreType.DMA((3,))],
    )
    return pl.pallas_call(
        functools.partial(_dispatch_kernel, tm=tm),
        grid_spec=grid_spec,
        out_shape=[
            jax.ShapeDtypeStruct((n_slots * PK_ROWS, LANES), jnp.uint32),
            jax.ShapeDtypeStruct(w1.shape, MXU_DTYPE),
            jax.ShapeDtypeStruct(w2.shape, MXU_DTYPE),
        ],
        compiler_params=pltpu.CompilerParams(
            dimension_semantics=("arbitrary",), vmem_limit_bytes=VMEM_LIMIT),
        name="dispatch",
    )(pad_row, pad_n, _step_tables(dest, tm), h1p, w1, w2)


def _expert_kernel(be_ref, nr_ref, x_ref, *refs):
    y_ref = refs[-1]
    rows = MOE_BLOCK * PK_ROWS
    g0 = pl.program_id(0) * BLOCKS_PER_STEP
    fused = (be_ref[g0] == be_ref[g0 + 1]) & (g0 + 1 < nr_ref[0])

    def ffn(x_blk, y_blk, n, w1_ref, b1_ref, w2_ref, b2_ref):
        x = _mx(_unpack_pairs(_load_token_tiles(x_blk, n, PK_ROWS)))
        hh = _dot(x, w1_ref[0]) + b1_ref[0]
        x_glu = jnp.minimum(hh[:, :D_FF], SWIGLU_LIMIT)
        x_lin = jnp.clip(hh[:, D_FF:], -SWIGLU_LIMIT, SWIGLU_LIMIT)
        act = x_glu * (1.0 / (1.0 + jnp.exp(-SWIGLU_ALPHA * x_glu))) * (x_lin + 1.0)
        _store_token_tiles(y_blk, _pack_pairs(_dot(_mx(act), w2_ref[0]) + b2_ref[0]))

    @pl.when(fused)
    def _():
        ffn(x_ref, y_ref, BLOCKS_PER_STEP * MOE_BLOCK, *refs[0:4])

    @pl.when(jnp.logical_not(fused))
    def _():
        for half in range(BLOCKS_PER_STEP):
            weights = refs[4 * half:4 * half + 4]
            g = g0 + half
            x_blk = x_ref.at[pl.ds(half * rows, rows), :]
            y_blk = y_ref.at[pl.ds(half * rows, rows), :]

            @pl.when(g >= nr_ref[0])
            def _():
                y_blk[...] = jnp.zeros_like(y_blk)

            @pl.when(g < nr_ref[0])
            def _():
                ffn(x_blk, y_blk, MOE_BLOCK, *weights)


def _expert_call(block_expert, n_real, xs, w1, b1, w2, b2):
    n_blocks = block_expert.shape[0]
    assert BLOCKS_PER_STEP == 2 and n_blocks % BLOCKS_PER_STEP == 0
    step = lambda i, be, nr: (i, 0)
    rows = BLOCKS_PER_STEP * MOE_BLOCK * PK_ROWS
    expert_specs, expert_args = [], []
    for half in range(BLOCKS_PER_STEP):
        ex = functools.partial(lambda i, be, nr, half: (be[i * BLOCKS_PER_STEP + half], 0, 0), half=half)
        expert_specs += [pl.BlockSpec((1, D_MODEL, 2 * D_FF), ex), pl.BlockSpec((1, 1, 2 * D_FF), ex),
                         pl.BlockSpec((1, D_FF, D_MODEL), ex), pl.BlockSpec((1, 1, D_MODEL), ex)]
        expert_args += [w1, b1, w2, b2]
    grid_spec = pltpu.PrefetchScalarGridSpec(
        num_scalar_prefetch=2,
        grid=(n_blocks // BLOCKS_PER_STEP,),
        in_specs=[pl.BlockSpec((rows, LANES), step)] + expert_specs,
        out_specs=pl.BlockSpec((rows, LANES), step),
    )
    return pl.pallas_call(
        _expert_kernel,
        grid_spec=grid_spec,
        out_shape=jax.ShapeDtypeStruct((n_blocks * MOE_BLOCK * PK_ROWS, LANES), jnp.uint32),
        compiler_params=pltpu.CompilerParams(
            dimension_semantics=("arbitrary",), vmem_limit_bytes=VMEM_LIMIT),
        name="moe_experts",
    )(block_expert, n_real, xs, *expert_args)


def _gather_combine_kernel(dest0_ref, destn_ref, gate_ref, h1_ref, g_ref, b_ref, ys_hbm,
                           out_ref, ybuf, sem, *, tm):
    i = pl.program_id(0)
    slot = i % 2
    rows = tm * PK_ROWS

    def tile(ref, t):
        return ref.at[pl.ds(pl.multiple_of(t * PK_ROWS, PK_ROWS), PK_ROWS), :]

    def gather(tbl, s):
        def body(it, c):
            for u in range(DMA_UNROLL):
                r = it * DMA_UNROLL + u
                for k in range(TOP_K):
                    pltpu.make_async_copy(tile(ys_hbm, tbl[0, 0, k * tm + r]),
                                          tile(ybuf, (s * TOP_K + k) * tm + r), sem.at[s]).start(priority=k % 2)
            return c
        lax.fori_loop(0, tm // DMA_UNROLL, body, 0)

    @pl.when(i == 0)
    def _():
        gather(dest0_ref, 0)

    @pl.when(i + 1 < pl.num_programs(0))
    def _():
        gather(destn_ref, 1 - slot)

    for k in range(TOP_K):
        plane = ybuf.at[pl.ds(pl.multiple_of((slot * TOP_K + k) * rows, rows), rows), :]
        pltpu.make_async_copy(ys_hbm.at[pl.ds(0, rows), :], plane, sem.at[slot]).wait()

    gates = gate_ref[...]
    acc = DEEPNORM_ALPHA * h1_ref[...]
    for k in range(TOP_K):
        plane = ybuf.at[pl.ds(pl.multiple_of((slot * TOP_K + k) * rows, rows), rows), :]
        acc = acc + gates[:, k:k + 1] * _unpack_pairs(_load_token_tiles(plane, tm, PK_ROWS))
    out_ref[...] = _layer_norm(acc, g_ref[...], b_ref[...])


def _gather_combine_call(dest, gates, h1, g, b, ys, tm):
    t = h1.shape[0]
    nt = t // tm
    const = lambda i: (0, 0)
    smem = functools.partial(pl.BlockSpec, memory_space=pltpu.SMEM)
    dest = _step_tables(dest, tm)
    dest_next = jnp.concatenate([dest[1:], dest[:1]], axis=0)
    return pl.pallas_call(
        functools.partial(_gather_combine_kernel, tm=tm),
        grid=(nt,),
        in_specs=[
            smem((1, 1, TOP_K * tm), lambda i: (0, 0, 0)),
            smem((1, 1, TOP_K * tm), lambda i: (i, 0, 0)),
            pl.BlockSpec((tm, LANES), lambda i: (i, 0)),
            pl.BlockSpec((tm, D_MODEL), lambda i: (i, 0)),
            pl.BlockSpec((1, D_MODEL), const),
            pl.BlockSpec((1, D_MODEL), const),
            pl.BlockSpec(memory_space=pl.ANY),
        ],
        out_specs=pl.BlockSpec((tm, D_MODEL), lambda i: (i, 0)),
        out_shape=jax.ShapeDtypeStruct((t, D_MODEL), F32),
        scratch_shapes=[pltpu.VMEM((2 * TOP_K * tm * PK_ROWS, LANES), jnp.uint32),
                        pltpu.SemaphoreType.DMA((2,))],
        compiler_params=pltpu.CompilerParams(
            dimension_semantics=("arbitrary",), vmem_limit_bytes=VMEM_LIMIT),
        name="gather_combine_ln2",
    )(dest, dest_next, gates, h1, g, b, ys)


def _relayout_w_in(w):
    sizes = (256, 256, 512, 512, GATE_RANK, 256, 256, 512, 512)
    offs = np.concatenate([[0], np.cumsum(sizes)])
    gq, gk, gv, gg, glr, rq, rk, rv, rg = [w[:, offs[i]:offs[i + 1]] for i in range(9)]
    perm = np.zeros((256,), np.int32)
    for p in range(2):
        for l in range(LANES):
            part, hh, f = l // 64, (l % 64) // 32, l % 32
            perm[p * LANES + l] = (2 * p + hh) * DK + part * (DK // 2) + f
    glr = jnp.pad(glr, ((0, 0), (0, LANES - GATE_RANK)))
    return jnp.concatenate([gq, gk, gv, gg, glr, rq[:, perm], rk[:, perm], rv, rg], axis=1)


def kernel(x, positions, ln_in_g, ln_in_b, w_in, gla_gate_w, gla_gate_b, gla_norm_g, ret_norm_g,
           ret_norm_b, w_out, ln1_g, ln1_b, router_w, router_b, moe_w1, moe_b1, moe_w2, moe_b2,
           ln2_g, ln2_b):
    bsz, seq, d = x.shape
    assert d == D_MODEL and seq % GROUP == 0 and w_in.shape[0] == DEPTH == 1
    t = bsz * seq
    ts = 1024 if seq % 1024 == 0 else GROUP
    tm = min(512, t)
    assert seq % ts == 0 and t % tm == 0
    row = lambda v: v.reshape(1, -1).astype(F32)

    w_p = _relayout_w_in(w_in[0]).astype(MXU_DTYPE)
    x2 = x.reshape(t, d)
    proj = _ln_proj_call(x2, row(ln_in_g), row(ln_in_b), w_p, tm)

    gw = jnp.pad(gla_gate_w[0], ((0, LANES - GATE_RANK), (0, 0))).astype(MXU_DTYPE)
    half = DK // 2
    inv_freq = 1.0 / (ROPE_BASE ** np.linspace(0.0, 1.0, half, dtype=np.float32))
    invf = jnp.asarray(np.tile(inv_freq, LANES // half).reshape(1, LANES), F32)
    sgn = jnp.asarray(np.where(np.arange(LANES) < LANES // 2, -1.0, 1.0).reshape(1, LANES), F32)
    step_ang = np.arange(GROUP, dtype=np.float64)[:, None] * np.tile(inv_freq, LANES // half)[None, :]
    cd = jnp.asarray(np.cos(step_ang), F32)
    sd = jnp.asarray(np.sin(step_ang), F32)
    o = _mixer_call(proj, positions.reshape(t, 1), gw, row(gla_gate_b[0]), row(gla_norm_g[0]),
                    row(ret_norm_g[0]), row(ret_norm_b[0]), invf, sgn, cd, sd, bsz, seq, ts)

    rwt = jnp.pad(router_w[0].T, ((0, LANES - N_EXPERTS), (0, 0))).astype(MXU_DTYPE)
    rbt = jnp.pad(router_b[0], (0, LANES - N_EXPERTS)).reshape(LANES, 1).astype(F32)
    h1, h1t, r_i, r_g, cnt = _post_mix_call(o, x2, row(ln_in_g), row(ln_in_b),
                                            w_out[0].astype(MXU_DTYPE), row(ln1_g[0]),
                                       row(ln1_b[0]), rwt, rbt, tm)

    e = r_i[:TOP_K].reshape(-1)
    rank = r_i[TOP_K:].reshape(-1)
    counts = cnt[:, 0].astype(jnp.int32)
    padded = ((counts + MOE_BLOCK - 1) // MOE_BLOCK) * MOE_BLOCK
    pad_end = jnp.cumsum(padded)
    pad_start = pad_end - padded
    tk = t * TOP_K
    n_blocks = (tk + MOE_BLOCK - 1) // MOE_BLOCK + N_EXPERTS
    i32 = jnp.int32
    dest = rank
    for j in range(N_EXPERTS):
        dest = dest + jnp.where(e == j, pad_start[j], 0)
    dest = dest.reshape(TOP_K, t)
    block_start = jnp.arange(n_blocks, dtype=i32) * MOE_BLOCK
    block_expert = jnp.minimum(jnp.sum(block_start[:, None] >= pad_end[None, :], -1),
                               N_EXPERTS - 1).astype(i32)
    n_real = (pad_end[-1:] // MOE_BLOCK).astype(i32)

    n_slots = n_blocks * MOE_BLOCK
    hole_row = jnp.concatenate([pad_start + counts, pad_end[-1:]]).astype(i32)
    hole_n = jnp.concatenate([padded - counts, n_slots - pad_end[-1:]]).astype(i32)
    xs, w1c, w2c = _dispatch_call(hole_row, hole_n, dest, h1t,
                                  moe_w1[0].reshape(N_EXPERTS * D_MODEL, 2 * D_FF),
                                  moe_w2[0].reshape(N_EXPERTS * D_FF, D_MODEL), n_slots, DISPATCH_STEPS)
    ys = _expert_call(block_expert, n_real, xs,
                      w1c.reshape(N_EXPERTS, D_MODEL, 2 * D_FF), moe_b1[0].reshape(N_EXPERTS, 1, 2 * D_FF),
                      w2c.reshape(N_EXPERTS, D_FF, D_MODEL), moe_b2[0].reshape(N_EXPERTS, 1, D_MODEL))

    out = _gather_combine_call(dest, r_g, h1, row(ln2_g[0]), row(ln2_b[0]), ys, min(256, t))
    return out.reshape(bsz, seq, d)
```
